```python
import math
import jax
import jax.numpy as jnp
from jax import lax
import numpy as np

D_MODEL = 2048
BATCH = 32
SEQ = 256
DEPTH = 4
DEC_BATCH = 8
DEC_SEQ = 1024
PAST_LEN = 512

GRID_W = 64
N_MIXERS = 3
N_DIFF = (DEPTH + 2) // 3
N_NA = (DEPTH + 1) // 3
N_RET = DEPTH // 3

DIFF_HEADS = 8
DIFF_QK = D_MODEL // (2 * DIFF_HEADS)
DIFF_V = 2 * DIFF_QK
NA_HEADS = 16
NA_HEAD_DIM = D_MODEL // NA_HEADS
NA_KH_MAX = 8
NA_KW = 16
RET_HEADS = 8
RET_QK = D_MODEL // RET_HEADS
RET_V = 2 * RET_QK
RET_CHUNK = 128
N_EXPERTS = 64
N_GROUPS = 8
EXPERTS_PER_GROUP = N_EXPERTS // N_GROUPS
TOPK_GROUPS = 4
TOP_K = 6
D_EXPERT = 512
D_SHARED = 512
ROUTED_SCALE = 2.5
MOE_BLOCK = 128
Q_BLOCK = 128
ROPE_BASE = 10000.0
LN_EPS = 1e-5
ALPHA = (2 * DEPTH) ** 0.25
BETA = (8 * DEPTH) ** -0.25
ADA_SCALE = 0.5

kernel_name = 'hybrid_diffusion_prefix_trunk_step'


def layer_norm(x, g, b):
    xf = x.astype(jnp.float32)
    mu = jnp.mean(xf, -1, keepdims=True)
    var = jnp.mean(jnp.square(xf - mu), -1, keepdims=True)
    return ((xf - mu) * lax.rsqrt(var + LN_EPS) * g + b).astype(x.dtype)


def head_norm(x):
    xf = x.astype(jnp.float32)
    mu = jnp.mean(xf, -1, keepdims=True)
    var = jnp.mean(jnp.square(xf - mu), -1, keepdims=True)
    return (xf - mu) * lax.rsqrt(var + LN_EPS)


def rms_norm(x, g):
    xf = x.astype(jnp.float32)
    return xf * lax.rsqrt(jnp.mean(jnp.square(xf), -1, keepdims=True) + LN_EPS) * g


def modulation(cond, w, b):
    m = jax.nn.silu(cond) @ w + b
    return jnp.split(m[..., None, :], 6, axis=-1)


def query_blocks(x):
    b, l = x.shape[:2]
    return jnp.moveaxis(x.reshape((b, l // Q_BLOCK, Q_BLOCK) + x.shape[2:]), 1, 0)


def merge_blocks(y):
    y = jnp.moveaxis(y, 0, 1)
    return y.reshape((y.shape[0], y.shape[1] * y.shape[2]) + y.shape[3:])


def axial_rope(x):
    l, d = x.shape[1], x.shape[-1]
    half = d // 2
    t = jnp.arange(l)
    inv = ROPE_BASE ** (-jnp.arange(0, half, 2, dtype=jnp.float32) / half)

    def rot(xp, pos):
        ang = pos.astype(jnp.float32)[:, None] * inv[None, :]
        cos = jnp.cos(ang)[None, :, None, :]
        sin = jnp.sin(ang)[None, :, None, :]
        x1, x2 = jnp.split(xp.astype(jnp.float32), 2, axis=-1)
        return jnp.concatenate([x1 * cos - x2 * sin, x1 * sin + x2 * cos], axis=-1)

    out = jnp.concatenate([rot(x[..., :half], t // GRID_W), rot(x[..., half:], t % GRID_W)], axis=-1)
    return out.astype(x.dtype)


def softmax_attend(q, k, v):
    scale = q.shape[-1] ** -0.5

    def block(qb):
        s = jnp.einsum('bqhd,bkhd->bhqk', qb, k).astype(jnp.float32) * scale
        return jnp.einsum('bhqk,bkhd->bqhd', jax.nn.softmax(s, axis=-1), v)

    return merge_blocks(lax.map(block, query_blocks(q)))


def diff_qkv(h, w_in):
    b, l, _ = h.shape
    q, k, v = jnp.split(h @ w_in, 3, axis=-1)
    return (q.reshape(b, l, DIFF_HEADS, 2 * DIFF_QK),
            k.reshape(b, l, DIFF_HEADS, 2 * DIFF_QK),
            v.reshape(b, l, DIFF_HEADS, DIFF_V))


def rope_pair(x):
    b, l, h, _ = x.shape
    return axial_rope(x.reshape(b, l, 2 * h, DIFF_QK)).reshape(b, l, h, 2 * DIFF_QK)


def diff_attend(q, k, v, lam_p, subln_g, lam_init):
    b, lq, h, _ = q.shape
    lp = lam_p.astype(jnp.float32)
    lam = jnp.exp(jnp.sum(lp[0] * lp[1])) - jnp.exp(jnp.sum(lp[2] * lp[3])) + lam_init
    kk = k.reshape(k.shape[:3] + (2, DIFF_QK))
    scale = DIFF_QK ** -0.5

    def block(qb):
        qb = qb.reshape(qb.shape[:3] + (2, DIFF_QK))
        s = jnp.einsum('bqhmd,bkhmd->bhmqk', qb, kk).astype(jnp.float32) * scale
        a = jax.nn.softmax(s, axis=-1)
        a = a[:, :, 0] - lam * a[:, :, 1]
        return jnp.einsum('bhqk,bkhe->bqhe', a, v)

    o = merge_blocks(lax.map(block, query_blocks(q)))
    o = rms_norm(o, subln_g) * (1.0 - lam_init)
    return o.reshape(b, lq, h * DIFF_V)


def diff_context(h, w_in, w_out, lam_p, subln_g, lam_init):
    q, k, v = diff_qkv(h, w_in)
    o = diff_attend(q, k, v, lam_p, subln_g, lam_init)
    return o.astype(h.dtype) @ w_out, k, v


def diff_latent(h, k_ctx, v_ctx, w_in, w_out, lam_p, subln_g, lam_init):
    q, k, v = diff_qkv(h, w_in)
    q = rope_pair(q)
    k = rope_pair(k)
    k_all = jnp.concatenate([k_ctx.astype(k.dtype), k], axis=1)
    v_all = jnp.concatenate([v_ctx.astype(v.dtype), v], axis=1)
    o = diff_attend(q, k_all, v_all, lam_p, subln_g, lam_init)
    return o.astype(h.dtype) @ w_out


def na_qkv(h, w_in):
    b, l, _ = h.shape
    q, k, v = jnp.split(h @ w_in, 3, axis=-1)
    shp = (b, l, NA_HEADS, NA_HEAD_DIM)
    return q.reshape(shp), k.reshape(shp), v.reshape(shp)


def na_context(h, w_in, w_out):
    b, l, _ = h.shape
    q, k, v = na_qkv(h, w_in)
    o = softmax_attend(q, k, v)
    return o.reshape(b, l, NA_HEADS * NA_HEAD_DIM).astype(h.dtype) @ w_out, k, v


def na_latent(h, k_ctx, v_ctx, w_in, w_out, rpb):
    b, l, _ = h.shape
    q, k, v = na_qkv(h, w_in)
    rows = l // GRID_W
    kh = min(NA_KH_MAX, rows)
    n_loc = kh * GRID_W
    scale = NA_HEAD_DIM ** -0.5
    k_grid = k.reshape(b, rows, GRID_W, NA_HEADS, NA_HEAD_DIM)
    v_grid = v.reshape(b, rows, GRID_W, NA_HEADS, NA_HEAD_DIM)
    k_ctx = k_ctx.astype(k.dtype)
    v_ctx = v_ctx.astype(v.dtype)
    col = jnp.arange(GRID_W)
    col_start = jnp.clip(col - NA_KW // 2, 0, GRID_W - NA_KW)
    col_ok = (col[None, :] >= col_start[:, None]) & (col[None, :] < col_start[:, None] + NA_KW)
    mask = jnp.broadcast_to(col_ok[:, None, :], (GRID_W, kh, GRID_W)).reshape(GRID_W, n_loc)
    dc_idx = jnp.clip(col[None, :] - col[:, None] + NA_KW - 1, 0, 2 * NA_KW - 2)

    def row_step(args):
        r, q_row = args
        r0 = jnp.clip(r - kh // 2, 0, rows - kh)
        k_win = lax.dynamic_slice_in_dim(k_grid, r0, kh, axis=1).reshape(b, n_loc, NA_HEADS, NA_HEAD_DIM)
        v_win = lax.dynamic_slice_in_dim(v_grid, r0, kh, axis=1).reshape(b, n_loc, NA_HEADS, NA_HEAD_DIM)
        dr_idx = r0 + jnp.arange(kh) - r + NA_KH_MAX - 1
        bias = rpb[:, dr_idx[None, :, None], dc_idx[:, None, :]].reshape(NA_HEADS, GRID_W, n_loc)
        s_loc = jnp.einsum('bqhd,bkhd->bhqk', q_row, k_win).astype(jnp.float32) * scale + bias
        s_loc = jnp.where(mask, s_loc, -jnp.inf)
        s_ctx = jnp.einsum('bqhd,bkhd->bhqk', q_row, k_ctx).astype(jnp.float32) * scale
        p = jax.nn.softmax(jnp.concatenate([s_loc, s_ctx], axis=-1), axis=-1)
        return (jnp.einsum('bhqk,bkhd->bqhd', p[..., :n_loc], v_win)
                + jnp.einsum('bhqk,bkhd->bqhd', p[..., n_loc:], v_ctx))

    q_rows = jnp.moveaxis(q.reshape(b, rows, GRID_W, NA_HEADS, NA_HEAD_DIM), 1, 0)
    o = lax.map(row_step, (jnp.arange(rows), q_rows))
    o = jnp.moveaxis(o, 0, 1).reshape(b, l, NA_HEADS * NA_HEAD_DIM)
    return o.astype(h.dtype) @ w_out


def retention_scan(q, k, v, log_g, s0):
    b, l, h, dk = q.shape
    dv = v.shape[-1]
    nc = l // RET_CHUNK
    pos = jnp.arange(RET_CHUNK, dtype=jnp.float32)
    diff = pos[:, None] - pos[None, :]
    dmask = jnp.where(diff >= 0, jnp.exp(jnp.maximum(diff, 0.0)[None] * log_g[:, None, None]), 0.0)
    q_decay = jnp.exp((pos + 1.0)[:, None] * log_g[None, :])
    k_decay = jnp.exp((RET_CHUNK - 1.0 - pos)[:, None] * log_g[None, :])
    chunk_decay = jnp.exp(RET_CHUNK * log_g)

    def chunks(x):
        return jnp.moveaxis(x.astype(jnp.float32).reshape((b, nc, RET_CHUNK) + x.shape[2:]), 1, 0)

    def step(s, inp):
        qi, ki, vi = inp
        inner = jnp.einsum('bihd,bjhd->bhij', qi, ki) * dmask[None]
        o = (jnp.einsum('bhij,bjhe->bihe', inner, vi)
             + jnp.einsum('bihd,bhde->bihe', qi, s) * q_decay[None, :, :, None])
        s = s * chunk_decay[None, :, None, None] + jnp.einsum('bjhd,bjhe->bhde', ki * k_decay[None, :, :, None], vi)
        return s, o

    s_fin, o = lax.scan(step, s0.astype(jnp.float32), (chunks(q), chunks(k), chunks(v)))
    return jnp.moveaxis(o, 0, 1).reshape(b, l, h, dv), s_fin


def ret_mix(h, w_in, w_out, decay_logit, s_fwd, s_bwd):
    b, l, _ = h.shape
    qd = RET_HEADS * RET_QK
    vd = RET_HEADS * RET_V
    q, k, v, g = jnp.split(h @ w_in, [qd, 2 * qd, 2 * qd + vd], axis=-1)
    q = q.reshape(b, l, RET_HEADS, RET_QK)
    k = k.reshape(b, l, RET_HEADS, RET_QK) * (RET_QK ** -0.5)
    v = v.reshape(b, l, RET_HEADS, RET_V)
    log_g = -jax.nn.softplus(-decay_logit.astype(jnp.float32))
    o_f, st_f = retention_scan(q, k, v, log_g[0], s_fwd)
    o_b, st_b = retention_scan(q[:, ::-1], k[:, ::-1], v[:, ::-1], log_g[1], s_bwd)
    o = head_norm(o_f + o_b[:, ::-1])
    o = o * jax.nn.silu(g.astype(jnp.float32)).reshape(b, l, RET_HEADS, RET_V)
    y = o.reshape(b, l, vd).astype(h.dtype) @ w_out
    return y, jnp.stack([st_f, st_b], axis=1)


def moe_ffn(h, w_router, b_router, w_e_gate, w_e_up, w_e_down, w_s_gate, w_s_up, w_s_down):
    lead = h.shape[:-1]
    x = h.reshape(-1, D_MODEL)
    n = x.shape[0]
    scores = jax.nn.sigmoid(jnp.dot(x, w_router).astype(jnp.float32))
    biased = scores + b_router.astype(jnp.float32)
    grp_score = lax.top_k(biased.reshape(n, N_GROUPS, EXPERTS_PER_GROUP), 2)[0].sum(-1)
    _, top_g = lax.top_k(grp_score, TOPK_GROUPS)
    g_mask = jax.nn.one_hot(top_g, N_GROUPS, dtype=jnp.float32).sum(1) > 0
    masked = jnp.where(jnp.repeat(g_mask, EXPERTS_PER_GROUP, axis=1), biased, -jnp.inf)
    _, top_e = lax.top_k(masked, TOP_K)
    w = jnp.take_along_axis(scores, top_e, axis=1)
    w = w / jnp.sum(w, -1, keepdims=True) * ROUTED_SCALE
    n_assign = n * TOP_K
    flat_e = top_e.reshape(-1)
    order = jnp.argsort(flat_e)
    sorted_e = flat_e[order]
    sorted_tok = order // TOP_K
    counts = jnp.bincount(flat_e, length=N_EXPERTS)
    padded = (counts + MOE_BLOCK - 1) // MOE_BLOCK * MOE_BLOCK
    pad_end = jnp.cumsum(padded)
    pad_start = pad_end - padded
    start = jnp.cumsum(counts) - counts
    dest = pad_start[sorted_e] + jnp.arange(n_assign) - start[sorted_e]
    cap = (n_assign + MOE_BLOCK - 1) // MOE_BLOCK * MOE_BLOCK + N_EXPERTS * MOE_BLOCK
    n_blocks = cap // MOE_BLOCK
    slot_tok = jnp.zeros((cap,), jnp.int32).at[dest].set(sorted_tok)
    block_e = jnp.minimum(jnp.searchsorted(pad_end, jnp.arange(n_blocks) * MOE_BLOCK, side='right'), N_EXPERTS - 1)

    def run_block(args):
        toks, e = args
        xb = x[toks]
        a = jax.nn.silu(xb @ w_e_gate[e]) * (xb @ w_e_up[e])
        return a @ w_e_down[e]

    y_slots = lax.map(run_block, (slot_tok.reshape(n_blocks, MOE_BLOCK), block_e)).reshape(cap, D_MODEL)
    contrib = y_slots[dest].astype(jnp.float32) * w.reshape(-1)[order][:, None]
    routed = jnp.zeros((n, D_MODEL), jnp.float32).at[sorted_tok].add(contrib)
    shared = (jax.nn.silu(x @ w_s_gate) * (x @ w_s_up)) @ w_s_down
    return (routed + shared).astype(h.dtype).reshape(lead + (D_MODEL,))


def setup_inputs(seed: int = 0) -> dict:
    key = jax.random.key(seed)
    ks = iter(jax.random.split(key, 40))

    def nrm(shape, scale):
        return jax.random.normal(next(ks), shape, jnp.float32) * scale

    d = D_MODEL
    ret_base = jnp.log(2.0 ** (5.0 + jnp.arange(RET_HEADS, dtype=jnp.float32)) - 1.0)
    return {
        'x_prompt': nrm((BATCH, SEQ, d), 1.0),
        'x_sample': nrm((DEC_BATCH, DEC_SEQ, d), 1.0),
        'cache_diff_k': nrm((DEC_BATCH, N_DIFF, PAST_LEN, DIFF_HEADS, 2 * DIFF_QK), 1.0),
        'cache_diff_v': nrm((DEC_BATCH, N_DIFF, PAST_LEN, DIFF_HEADS, DIFF_V), 1.0),
        'cache_na_k': nrm((DEC_BATCH, N_NA, PAST_LEN, NA_HEADS, NA_HEAD_DIM), 1.0),
        'cache_na_v': nrm((DEC_BATCH, N_NA, PAST_LEN, NA_HEADS, NA_HEAD_DIM), 1.0),
        'state_ret': nrm((DEC_BATCH, N_RET, 2, RET_HEADS, RET_QK, RET_V), 0.5),
        'c': nrm((DEC_BATCH, d), 1.0),
        'c_ctx': nrm((d,), 1.0),
        'ada_w': nrm((DEPTH, d, 6 * d), ADA_SCALE * d ** -0.5),
        'ada_b': nrm((DEPTH, 6 * d), 0.02),
        'ln_g': 1.0 + nrm((DEPTH, 2, d), 0.02),
        'ln_b': nrm((DEPTH, 2, d), 0.02),
        'diff_w_in': nrm((N_DIFF, d, 3 * DIFF_HEADS * 2 * DIFF_QK), d ** -0.5),
        'diff_w_out': nrm((N_DIFF, DIFF_HEADS * DIFF_V, d), BETA * (DIFF_HEADS * DIFF_V) ** -0.5),
        'diff_lambda': nrm((N_DIFF, 4, DIFF_QK), 0.1),
        'diff_subln_g': 1.0 + nrm((N_DIFF, DIFF_V), 0.02),
        'na_w_in': nrm((N_NA, d, 3 * NA_HEADS * NA_HEAD_DIM), d ** -0.5),
        'na_w_out': nrm((N_NA, NA_HEADS * NA_HEAD_DIM, d), BETA * (NA_HEADS * NA_HEAD_DIM) ** -0.5),
        'na_rpb': nrm((N_NA, NA_HEADS, 2 * NA_KH_MAX - 1, 2 * NA_KW - 1), 0.1),
        'ret_w_in': nrm((N_RET, d, 2 * RET_HEADS * RET_QK + 2 * RET_HEADS * RET_V), d ** -0.5),
        'ret_w_out': nrm((N_RET, RET_HEADS * RET_V, d), BETA * (RET_HEADS * RET_V) ** -0.5),
        'ret_decay': ret_base + nrm((N_RET, 2, RET_HEADS), 0.1),
        'moe_router': nrm((DEPTH, d, N_EXPERTS), d ** -0.5),
        'moe_router_bias': nrm((DEPTH, N_EXPERTS), 0.01),
        'moe_w_gate': nrm((DEPTH, N_EXPERTS, d, D_EXPERT), d ** -0.5),
        'moe_w_up': nrm((DEPTH, N_EXPERTS, d, D_EXPERT), d ** -0.5),
        'moe_w_down': nrm((DEPTH, N_EXPERTS, D_EXPERT, d), BETA * D_EXPERT ** -0.5),
        'shared_w_gate': nrm((DEPTH, d, D_SHARED), d ** -0.5),
        'shared_w_up': nrm((DEPTH, d, D_SHARED), d ** -0.5),
        'shared_w_down': nrm((DEPTH, D_SHARED, d), BETA * D_SHARED ** -0.5),
    }


def reference(x_prompt, x_sample, cache_diff_k, cache_diff_v, cache_na_k, cache_na_v, state_ret,
              c, c_ctx, ada_w, ada_b, ln_g, ln_b,
              diff_w_in, diff_w_out, diff_lambda, diff_subln_g,
              na_w_in, na_w_out, na_rpb,
              ret_w_in, ret_w_out, ret_decay,
              moe_router, moe_router_bias, moe_w_gate, moe_w_up, moe_w_down,
              shared_w_gate, shared_w_up, shared_w_down):
    xp, xs = x_prompt, x_sample
    s_zero = jnp.zeros((xp.shape[0], RET_HEADS, RET_QK, RET_V), jnp.float32)
    new_dk, new_dv, new_nk, new_nv, new_rs = [], [], [], [], []
    for i in range(DEPTH):
        j = i // N_MIXERS
        sh_p, sc_p, g_p, sh2_p, sc2_p, g2_p = modulation(c_ctx, ada_w[i], ada_b[i])
        sh_s, sc_s, g_s, sh2_s, sc2_s, g2_s = modulation(c, ada_w[i], ada_b[i])
        hp = xp * (1.0 + sc_p) + sh_p
        hs = xs * (1.0 + sc_s) + sh_s
        if i % N_MIXERS == 0:
            lam_init = 0.8 - 0.6 * math.exp(-0.3 * i)
            yp, kc, vc = diff_context(hp, diff_w_in[j], diff_w_out[j], diff_lambda[j], diff_subln_g[j], lam_init)
            ys = diff_latent(hs, cache_diff_k[:, j], cache_diff_v[:, j], diff_w_in[j], diff_w_out[j],
                             diff_lambda[j], diff_subln_g[j], lam_init)
            new_dk.append(kc)
            new_dv.append(vc)
        elif i % N_MIXERS == 1:
            yp, kc, vc = na_context(hp, na_w_in[j], na_w_out[j])
            ys = na_latent(hs, cache_na_k[:, j], cache_na_v[:, j], na_w_in[j], na_w_out[j], na_rpb[j])
            new_nk.append(kc)
            new_nv.append(vc)
        else:
            yp, st = ret_mix(hp, ret_w_in[j], ret_w_out[j], ret_decay[j], s_zero, s_zero)
            ys, _ = ret_mix(hs, ret_w_in[j], ret_w_out[j], ret_decay[j], state_ret[:, j, 0], state_ret[:, j, 1])
            new_rs.append(st)
        xp = layer_norm(ALPHA * xp + g_p * yp, ln_g[i, 0], ln_b[i, 0])
        xs = layer_norm(ALPHA * xs + g_s * ys, ln_g[i, 0], ln_b[i, 0])
        moe_w = (moe_router[i], moe_router_bias[i], moe_w_gate[i], moe_w_up[i], moe_w_down[i],
                 shared_w_gate[i], shared_w_up[i], shared_w_down[i])
        hp = xp * (1.0 + sc2_p) + sh2_p
        hs = xs * (1.0 + sc2_s) + sh2_s
        xp = layer_norm(ALPHA * xp + g2_p * moe_ffn(hp, *moe_w), ln_g[i, 1], ln_b[i, 1])
        xs = layer_norm(ALPHA * xs + g2_s * moe_ffn(hs, *moe_w), ln_g[i, 1], ln_b[i, 1])
    new_cache_diff_k = jnp.stack(new_dk, axis=1)
    new_cache_diff_v = jnp.stack(new_dv, axis=1)
    new_cache_na_k = jnp.stack(new_nk, axis=1)
    new_cache_na_v = jnp.stack(new_nv, axis=1)
    new_state_ret = jnp.stack(new_rs, axis=1)
    return (xp, xs, new_cache_diff_k, new_cache_diff_v, new_cache_na_k, new_cache_na_v, new_state_ret)
```

```python
import functools
import math

import jax
import jax.numpy as jnp
from jax import lax
from jax.experimental import pallas as pl
from jax.experimental.pallas import tpu as pltpu

F32 = jnp.float32
BF16 = jnp.bfloat16

D = 2048
BATCH = 32
SEQ = 256
DEPTH = 4
DEC_BATCH = 8
DEC_SEQ = 1024
PAST = 512
NP = BATCH * SEQ
NS = DEC_BATCH * DEC_SEQ
NT = NP + NS
GRID_W = 64
N_MIXERS = 3
DIFF_HEADS = 8
DIFF_QK = 128
DIFF_V = 256
NA_HEADS = 16
NA_DH = 128
NA_KH = 8
NA_KH_MAX = 8
NA_KW = 16
RET_HEADS = 8
RET_QK = 256
RET_V = 512
RET_CHUNK = 128
N_EXPERTS = 64
N_GROUPS = 8
EXPERTS_PER_GROUP = 8
TOPK_GROUPS = 4
TOP_K = 6
D_EXPERT = 512
D_SHARED = 512
ROUTED_SCALE = 2.5
ROPE_BASE = 10000.0
LN_EPS = 1e-5
ALPHA = (2 * DEPTH) ** 0.25
N_COND = 16

VMEM_LIMIT = 56 * 1024 * 1024

MOE_TB = 256
MOE_NBLK = (NT * TOP_K) // MOE_TB + N_EXPERTS
MOE_CAP = MOE_NBLK * MOE_TB
CMB_TM = 128


def _cparams(sem):
    return pltpu.CompilerParams(dimension_semantics=sem, vmem_limit_bytes=VMEM_LIMIT)


def _cond_row(row0):
    return jnp.where(row0 < NP, 0, 1 + (row0 - NP) // DEC_SEQ)


def _mod_spec(layer, chunk, tm, moff=0):
    return pl.BlockSpec((None, None, None, 1, D),
                        lambda m, *_: (layer, _cond_row((m + moff) * tm), chunk, 0, 0))


def _sigmoid(x):
    return 1.0 / (1.0 + jnp.exp(-x))


def _silu(x):
    return x * _sigmoid(x)


def _mod_kernel(cond_ref, w_ref, b_ref, o_ref):
    c = _silu(cond_ref[...]).astype(BF16)
    o_ref[...] = jnp.dot(c, w_ref[...].astype(BF16), preferred_element_type=F32) + b_ref[...]


def modulation_table(cond, ada_w, ada_b):
    tn = 1024
    n6 = 6 * D
    out = pl.pallas_call(
        _mod_kernel,
        grid=(DEPTH, n6 // tn),
        in_specs=[pl.BlockSpec((N_COND, D), lambda l, n: (0, 0)),
                  pl.BlockSpec((None, D, tn), lambda l, n: (l, 0, n)),
                  pl.BlockSpec((None, 1, tn), lambda l, n: (l, 0, n))],
        out_specs=pl.BlockSpec((None, N_COND, tn), lambda l, n: (l, 0, n)),
        out_shape=jax.ShapeDtypeStruct((DEPTH, N_COND, n6), F32),
        compiler_params=_cparams(("arbitrary", "arbitrary")),
    )(cond, ada_w, ada_b.reshape(DEPTH, 1, n6))
    return out.reshape(DEPTH, N_COND, 6, 1, D)


def _modulate_kernel(x_ref, sh_ref, sc_ref, o_ref):
    o_ref[...] = (x_ref[...] * (1.0 + sc_ref[...]) + sh_ref[...]).astype(o_ref.dtype)


def modulate(x, mods, layer):
    tm = 512
    return pl.pallas_call(
        _modulate_kernel,
        grid=(NT // tm,),
        in_specs=[pl.BlockSpec((tm, D), lambda m: (m, 0)),
                  _mod_spec(layer, 0, tm), _mod_spec(layer, 1, tm)],
        out_specs=pl.BlockSpec((tm, D), lambda m: (m, 0)),
        out_shape=jax.ShapeDtypeStruct((NT, D), BF16),
        compiler_params=_cparams(("arbitrary",)),
    )(x, mods, mods)


def _mm_kernel(a_ref, w_ref, o_ref):
    o_ref[...] = jnp.dot(a_ref[...], w_ref[...], preferred_element_type=F32).astype(o_ref.dtype)


def matmul(a, w, row0, nrows, out_dtype):
    k = a.shape[1]
    n = w.shape[1]
    tm, tn = 1024, 512
    moff = row0 // tm
    return pl.pallas_call(
        _mm_kernel,
        grid=(nrows // tm, n // tn),
        in_specs=[pl.BlockSpec((tm, k), lambda m, j: (m + moff, 0)),
                  pl.BlockSpec((k, tn), lambda m, j: (0, j))],
        out_specs=pl.BlockSpec((tm, tn), lambda m, j: (m, j)),
        out_shape=jax.ShapeDtypeStruct((nrows, n), out_dtype),
        compiler_params=_cparams(("arbitrary", "arbitrary")),
    )(a, w)


def _rope_tables():
    half = DIFF_QK // 2
    t = jnp.arange(DEC_SEQ)
    inv = ROPE_BASE ** (-jnp.arange(0, half, 2, dtype=F32) / half)

    def cs(pos):
        ang = pos.astype(F32)[:, None] * inv[None, :]
        return jnp.cos(ang), jnp.sin(ang)

    cr, sr = cs(t // GRID_W)
    cc, sc = cs(t % GRID_W)
    cos = jnp.concatenate([cr, cr, cc, cc], axis=-1)
    sin = jnp.concatenate([-sr, sr, -sc, sc], axis=-1)
    return cos, sin


def _rope_kernel(x_ref, cos_ref, sin_ref, o_ref):
    cos = cos_ref[...]
    sin = sin_ref[...]
    lane = lax.broadcasted_iota(jnp.int32, cos.shape, 1)
    first = (lane % (DIFF_QK // 2)) < (DIFF_QK // 4)
    for g in range(D // DIFF_QK):
        xg = x_ref[:, g * DIFF_QK:(g + 1) * DIFF_QK]
        sw = jnp.where(first, pltpu.roll(xg, DIFF_QK - DIFF_QK // 4, 1), pltpu.roll(xg, DIFF_QK // 4, 1))
        o_ref[:, g * DIFF_QK:(g + 1) * DIFF_QK] = (xg * cos + sw * sin).astype(o_ref.dtype)


def rope(x, cos, sin):
    tm = 512
    nt = DEC_SEQ // tm
    return pl.pallas_call(
        _rope_kernel,
        grid=(NS // tm,),
        in_specs=[pl.BlockSpec((tm, D), lambda m: (m, 0)),
                  pl.BlockSpec((tm, DIFF_QK), lambda m: (m % nt, 0)),
                  pl.BlockSpec((tm, DIFF_QK), lambda m: (m % nt, 0))],
        out_specs=pl.BlockSpec((tm, D), lambda m: (m, 0)),
        out_shape=jax.ShapeDtypeStruct((NS, D), BF16),
        compiler_params=_cparams(("arbitrary",)),
    )(x, cos, sin)


def _qkt(q, k):
    return lax.dot_general(q, k, (((1,), (1,)), ((), ())), preferred_element_type=F32)


def _softmax(s):
    m = jnp.max(s, axis=-1, keepdims=True)
    p = jnp.exp(s - m)
    return p / jnp.sum(p, axis=-1, keepdims=True)


def _diff_kernel(lam_init, lamp_ref, g_ref, q_ref, k_ref, v_ref, o_ref):
    lp = lamp_ref[...]
    lam = (jnp.exp(jnp.sum(lp[0:1] * lp[1:2], axis=-1, keepdims=True))
           - jnp.exp(jnp.sum(lp[2:3] * lp[3:4], axis=-1, keepdims=True)) + lam_init)
    scale = DIFF_QK ** -0.5
    g = g_ref[...]
    for h in range(DIFF_HEADS):
        c0 = h * DIFF_V
        q = q_ref[:, c0:c0 + DIFF_V].astype(BF16)
        k = k_ref[:, c0:c0 + DIFF_V].astype(BF16)
        v = v_ref[:, c0:c0 + DIFF_V].astype(BF16)
        a1 = _softmax(_qkt(q[:, :DIFF_QK], k[:, :DIFF_QK]) * scale)
        a2 = _softmax(_qkt(q[:, DIFF_QK:], k[:, DIFF_QK:]) * scale)
        a = (a1 - lam * a2).astype(BF16)
        o = jnp.dot(a, v, preferred_element_type=F32)
        o = o * lax.rsqrt(jnp.mean(jnp.square(o), axis=-1, keepdims=True) + LN_EPS) * g
        o_ref[:, c0:c0 + DIFF_V] = (o * (1.0 - lam_init)).astype(o_ref.dtype)


def diff_attention(q, k, v, lam_p, subln_g, lam_init, nb, lq, lk, tq):
    nq = lq // tq
    return pl.pallas_call(
        functools.partial(_diff_kernel, lam_init),
        grid=(nb, nq),
        in_specs=[pl.BlockSpec((4, DIFF_QK), lambda b, i: (0, 0)),
                  pl.BlockSpec((1, DIFF_V), lambda b, i: (0, 0)),
                  pl.BlockSpec((tq, D), lambda b, i: (b * nq + i, 0)),
                  pl.BlockSpec((lk, D), lambda b, i: (b, 0)),
                  pl.BlockSpec((lk, D), lambda b, i: (b, 0))],
        out_specs=pl.BlockSpec((tq, D), lambda b, i: (b * nq + i, 0)),
        out_shape=jax.ShapeDtypeStruct((nb * lq, D), BF16),
        compiler_params=_cparams(("arbitrary", "arbitrary")),
    )(lam_p, subln_g.reshape(1, DIFF_V), q, k, v)


def _na_ctx_kernel(q_ref, k_ref, v_ref, o_ref):
    scale = NA_DH ** -0.5
    for h in range(NA_HEADS):
        c0 = h * NA_DH
        q = q_ref[:, c0:c0 + NA_DH].astype(BF16)
        k = k_ref[:, c0:c0 + NA_DH].astype(BF16)
        v = v_ref[:, c0:c0 + NA_DH].astype(BF16)
        p = _softmax(_qkt(q, k) * scale).astype(BF16)
        o_ref[:, c0:c0 + NA_DH] = jnp.dot(p, v, preferred_element_type=F32).astype(o_ref.dtype)


def na_context_attention(q, k, v):
    spec = pl.BlockSpec((SEQ, D), lambda b: (b, 0))
    return pl.pallas_call(
        _na_ctx_kernel,
        grid=(BATCH,),
        in_specs=[spec, spec, spec],
        out_specs=spec,
        out_shape=jax.ShapeDtypeStruct((NP, D), BF16),
        compiler_params=_cparams(("arbitrary",)),
    )(q, k, v)


NA_ROWS = DEC_SEQ // GRID_W
NA_NLOC = NA_KH * GRID_W


def _na_window_row(r):
    return jnp.clip(r - NA_KH // 2, 0, NA_ROWS - NA_KH)


def _na_bias_tables(rpb):
    col = jnp.arange(GRID_W)
    col_start = jnp.clip(col - NA_KW // 2, 0, GRID_W - NA_KW)
    col_ok = (col[None, :] >= col_start[:, None]) & (col[None, :] < col_start[:, None] + NA_KW)
    mask = jnp.broadcast_to(col_ok[:, None, :], (GRID_W, NA_KH, GRID_W)).reshape(GRID_W, NA_NLOC)
    dc_idx = jnp.clip(col[None, :] - col[:, None] + NA_KW - 1, 0, 2 * NA_KW - 2)
    tabs = []
    for o in range(NA_KH):
        dr_idx = jnp.arange(NA_KH) - o + NA_KH_MAX - 1
        bias = rpb[:, dr_idx[None, :, None], dc_idx[:, None, :]].reshape(NA_HEADS, GRID_W, NA_NLOC)
        tabs.append(jnp.where(mask[None], bias, -jnp.inf))
    return jnp.stack(tabs)


def _na_lat_kernel(q_ref, k_ref, v_ref, kc_ref, vc_ref, bias_ref, o_ref):
    r = pl.program_id(1)
    start = pl.multiple_of(_na_window_row(r) * GRID_W, GRID_W)
    scale = NA_DH ** -0.5
    for h in range(NA_HEADS):
        c0 = h * NA_DH
        q = q_ref[:, c0:c0 + NA_DH]
        kw = k_ref[pl.ds(start, NA_NLOC), c0:c0 + NA_DH]
        vw = v_ref[pl.ds(start, NA_NLOC), c0:c0 + NA_DH]
        s_loc = _qkt(q, kw) * scale + bias_ref[h]
        s_ctx = _qkt(q, kc_ref[:, c0:c0 + NA_DH]) * scale
        m = jnp.maximum(jnp.max(s_loc, axis=-1, keepdims=True), jnp.max(s_ctx, axis=-1, keepdims=True))
        p_loc = jnp.exp(s_loc - m)
        p_ctx = jnp.exp(s_ctx - m)
        den = jnp.sum(p_loc, axis=-1, keepdims=True) + jnp.sum(p_ctx, axis=-1, keepdims=True)
        o = (jnp.dot((p_loc / den).astype(BF16), vw, preferred_element_type=F32)
             + jnp.dot((p_ctx / den).astype(BF16), vc_ref[:, c0:c0 + NA_DH], preferred_element_type=F32))
        o_ref[:, c0:c0 + NA_DH] = o.astype(o_ref.dtype)


def na_latent_attention(q, k, v, k_ctx, v_ctx, bias_tabs):
    return pl.pallas_call(
        _na_lat_kernel,
        grid=(DEC_BATCH, NA_ROWS),
        in_specs=[pl.BlockSpec((GRID_W, D), lambda b, r: (b * NA_ROWS + r, 0)),
                  pl.BlockSpec((DEC_SEQ, D), lambda b, r: (b, 0)),
                  pl.BlockSpec((DEC_SEQ, D), lambda b, r: (b, 0)),
                  pl.BlockSpec((None, PAST, D), lambda b, r: (b, 0, 0)),
                  pl.BlockSpec((None, PAST, D), lambda b, r: (b, 0, 0)),
                  pl.BlockSpec((None, NA_HEADS, GRID_W, NA_NLOC),
                               lambda b, r: (r - _na_window_row(r), 0, 0, 0))],
        out_specs=pl.BlockSpec((GRID_W, D), lambda b, r: (b * NA_ROWS + r, 0)),
        out_shape=jax.ShapeDtypeStruct((NS, D), BF16),
        compiler_params=_cparams(("arbitrary", "arbitrary")),
    )(q, k, v, k_ctx, v_ctx, bias_tabs)


def _ret_kernel(nc, has_state, emit_state, logg_ref, q_ref, k_ref, v_ref, g_ref, *rest):
    rest = list(rest)
    s0_ref = rest.pop(0) if has_state else None
    o_ref = rest.pop(0)
    st_ref = rest.pop(0) if emit_state else None
    oacc, sacc = rest
    h = pl.program_id(1)
    lgf = logg_ref[0, h]
    lgb = logg_ref[1, h]
    c = RET_CHUNK
    pos = lax.broadcasted_iota(jnp.int32, (c, 1), 0).astype(F32)
    diff = (lax.broadcasted_iota(jnp.int32, (c, c), 0) - lax.broadcasted_iota(jnp.int32, (c, c), 1)).astype(F32)
    dmask_f = jnp.where(diff >= 0, jnp.exp(jnp.maximum(diff, 0.0) * lgf), 0.0)
    dmask_b = jnp.where(diff <= 0, jnp.exp(jnp.maximum(-diff, 0.0) * lgb), 0.0)
    qdec_f = jnp.exp((pos + 1.0) * lgf)
    kdec_f = jnp.exp((c - 1.0 - pos) * lgf)
    qdec_b = jnp.exp((c - pos) * lgb)
    kdec_b = jnp.exp(pos * lgb)
    cd_f = jnp.exp(jnp.full((1, 1), c, F32) * lgf)
    cd_b = jnp.exp(jnp.full((1, 1), c, F32) * lgb)

    def chunk(i):
        sl = slice(i * c, (i + 1) * c)
        return q_ref[sl, :].astype(BF16), k_ref[sl, :] * (RET_QK ** -0.5), v_ref[sl, :].astype(BF16)

    def state_update(kc, kdec, vb, cd):
        kd = (kc * kdec).T.astype(BF16)
        sacc[...] = sacc[...] * cd + jnp.dot(kd, vb, preferred_element_type=F32)

    if has_state:
        sacc[...] = s0_ref[0]
    else:
        sacc[...] = jnp.zeros_like(sacc)
    for i in range(nc):
        qb, kc, vb = chunk(i)
        qk = _qkt(qb, kc.astype(BF16))
        o = (jnp.dot((qk * dmask_f).astype(BF16), vb, preferred_element_type=F32)
             + jnp.dot((qk * dmask_b).astype(BF16), vb, preferred_element_type=F32)
             + jnp.dot(qb, sacc[...].astype(BF16), preferred_element_type=F32) * qdec_f)
        oacc[i * c:(i + 1) * c, :] = o
        state_update(kc, kdec_f, vb, cd_f)
    if emit_state:
        st_ref[0] = sacc[...]

    if has_state:
        sacc[...] = s0_ref[1]
    else:
        sacc[...] = jnp.zeros_like(sacc)
    for i in reversed(range(nc)):
        qb, kc, vb = chunk(i)
        oacc[i * c:(i + 1) * c, :] += jnp.dot(qb, sacc[...].astype(BF16), preferred_element_type=F32) * qdec_b
        state_update(kc, kdec_b, vb, cd_b)
    if emit_state:
        st_ref[1] = sacc[...]

    o = oacc[...]
    mu = jnp.mean(o, axis=-1, keepdims=True)
    var = jnp.mean(jnp.square(o - mu), axis=-1, keepdims=True)
    o = (o - mu) * lax.rsqrt(var + LN_EPS)
    o_ref[...] = (o * _silu(g_ref[...])).astype(o_ref.dtype)


def retention(q, k, v, g, log_g, nb, l, state=None, emit_state=False):
    nc = l // RET_CHUNK
    has_state = state is not None
    in_specs = [pl.BlockSpec(memory_space=pltpu.SMEM),
                pl.BlockSpec((l, RET_QK), lambda b, h: (b, h)),
                pl.BlockSpec((l, RET_QK), lambda b, h: (b, h)),
                pl.BlockSpec((l, RET_V), lambda b, h: (b, h)),
                pl.BlockSpec((l, RET_V), lambda b, h: (b, h))]
    args = [log_g, q, k, v, g]
    st_spec = pl.BlockSpec((None, 2, None, RET_QK, RET_V), lambda b, h: (b, 0, h, 0, 0))
    if has_state:
        in_specs.append(st_spec)
        args.append(state)
    out_specs = [pl.BlockSpec((l, RET_V), lambda b, h: (b, h))]
    out_shape = [jax.ShapeDtypeStruct((nb * l, RET_HEADS * RET_V), BF16)]
    if emit_state:
        out_specs.append(st_spec)
        out_shape.append(jax.ShapeDtypeStruct((nb, 2, RET_HEADS, RET_QK, RET_V), F32))
    res = pl.pallas_call(
        functools.partial(_ret_kernel, nc, has_state, emit_state),
        grid=(nb, RET_HEADS),
        in_specs=in_specs,
        out_specs=out_specs,
        out_shape=out_shape,
        scratch_shapes=[pltpu.VMEM((l, RET_V), F32), pltpu.VMEM((RET_QK, RET_V), F32)],
        compiler_params=_cparams(("arbitrary", "arbitrary")),
    )(*args)
    return res if emit_state else res[0]


def _layer_norm(z, g, b):
    mu = jnp.mean(z, axis=-1, keepdims=True)
    var = jnp.mean(jnp.square(z - mu), axis=-1, keepdims=True)
    return (z - mu) * lax.rsqrt(var + LN_EPS) * g + b


def _outproj_kernel(nk, o_ref, w_ref, x_ref, gate_ref, lng_ref, lnb_ref, sh_ref, sc_ref, wr_ref,
                    xo_ref, ho_ref, so_ref, acc_ref):
    kk = pl.program_id(1)
    part = jnp.dot(o_ref[...], w_ref[...], preferred_element_type=F32)

    @pl.when(kk == 0)
    def _():
        acc_ref[...] = part

    @pl.when(kk > 0)
    def _():
        acc_ref[...] += part

    @pl.when(kk == nk - 1)
    def _():
        xn = _layer_norm(ALPHA * x_ref[...] + gate_ref[...] * acc_ref[...], lng_ref[...], lnb_ref[...])
        xo_ref[...] = xn
        hn = xn * (1.0 + sc_ref[...]) + sh_ref[...]
        ho_ref[...] = hn
        logits = jnp.dot(hn, wr_ref[...], preferred_element_type=F32, precision=lax.Precision.HIGHEST)
        so_ref[...] = _sigmoid(logits)


def outproj_ln(o, w_out, x, mods, ln_g, ln_b, w_router, layer):
    k = o.shape[1]
    tm, tk = 256, 2048
    nk = k // tk
    row = lambda m, kk: (m, 0)
    return pl.pallas_call(
        functools.partial(_outproj_kernel, nk),
        grid=(NT // tm, nk),
        in_specs=[pl.BlockSpec((tm, tk), lambda m, kk: (m, kk)),
                  pl.BlockSpec((tk, D), lambda m, kk: (kk, 0)),
                  pl.BlockSpec((tm, D), row),
                  _mod_spec(layer, 2, tm),
                  pl.BlockSpec((None, None, 1, D), lambda m, kk: (layer, 0, 0, 0)),
                  pl.BlockSpec((None, None, 1, D), lambda m, kk: (layer, 0, 0, 0)),
                  _mod_spec(layer, 3, tm), _mod_spec(layer, 4, tm),
                  pl.BlockSpec((D, N_EXPERTS), lambda m, kk: (0, 0))],
        out_specs=[pl.BlockSpec((tm, D), row), pl.BlockSpec((tm, D), row),
                   pl.BlockSpec((tm, N_EXPERTS), row)],
        out_shape=[jax.ShapeDtypeStruct((NT, D), F32), jax.ShapeDtypeStruct((NT, D), F32),
                   jax.ShapeDtypeStruct((NT, N_EXPERTS), F32)],
        scratch_shapes=[pltpu.VMEM((tm, D), F32)],
        compiler_params=_cparams(("arbitrary", "arbitrary")),
    )(o, w_out, x, mods, ln_g, ln_b, mods, mods, w_router)


def _moe_row_copy(h_hbm, xbuf, sem, tok_ref, slot, r):
    return pltpu.make_async_copy(h_hbm.at[pl.ds(tok_ref[0, 0, r], 1)], xbuf.at[slot, pl.ds(r, 1)], sem.at[slot])


def _moe_kernel(be_ref, first_ref, nused_ref, tok_ref, tokn_ref, h_hbm, wg_ref, wu_ref, wd_ref,
                y_ref, xbuf, wgb, wub, wdb, sem):
    i = pl.program_id(0)
    nused = nused_ref[0]
    slot = i % 2

    def gather(tref, s):
        def body(r, carry):
            _moe_row_copy(h_hbm, xbuf, sem, tref, s, r).start()
            return carry
        lax.fori_loop(0, MOE_TB, body, 0, unroll=8)

    @pl.when(i == 0)
    def _():
        gather(tok_ref, 0)

    @pl.when(i + 1 < nused)
    def _():
        gather(tokn_ref, 1 - slot)

    @pl.when(first_ref[i] == 1)
    def _():
        wgb[...] = wg_ref[...].astype(BF16)
        wub[...] = wu_ref[...].astype(BF16)
        wdb[...] = wd_ref[...].astype(BF16)

    @pl.when(i < nused)
    def _():
        pltpu.make_async_copy(h_hbm.at[pl.ds(0, MOE_TB)], xbuf.at[slot], sem.at[slot]).wait()
        x = xbuf[slot].astype(BF16)
        a = (_silu(jnp.dot(x, wgb[...], preferred_element_type=F32))
             * jnp.dot(x, wub[...], preferred_element_type=F32))
        y_ref[...] = jnp.dot(a.astype(BF16), wdb[...], preferred_element_type=F32)

    @pl.when(i >= nused)
    def _():
        y_ref[...] = jnp.zeros_like(y_ref)


def moe_experts(h2, slot_tok, block_e, first, nused, w_gate, w_up, w_down, layer):
    wspec = lambda shape: pl.BlockSpec((None, None) + shape, lambda i, be, fi, nu: (layer, be[i], 0, 0))
    grid_spec = pltpu.PrefetchScalarGridSpec(
        num_scalar_prefetch=3,
        grid=(MOE_NBLK,),
        in_specs=[pl.BlockSpec((1, 1, MOE_TB), lambda i, be, fi, nu: (i, 0, 0), memory_space=pltpu.SMEM),
                  pl.BlockSpec((1, 1, MOE_TB), lambda i, be, fi, nu: (jnp.minimum(i + 1, MOE_NBLK - 1), 0, 0),
                               memory_space=pltpu.SMEM),
                  pl.BlockSpec(memory_space=pl.ANY),
                  wspec((D, D_EXPERT)), wspec((D, D_EXPERT)), wspec((D_EXPERT, D))],
        out_specs=pl.BlockSpec((MOE_TB, D), lambda i, be, fi, nu: (i, 0)),
        scratch_shapes=[pltpu.VMEM((2, MOE_TB, D), F32),
                        pltpu.VMEM((D, D_EXPERT), BF16), pltpu.VMEM((D, D_EXPERT), BF16),
                        pltpu.VMEM((D_EXPERT, D), BF16),
                        pltpu.SemaphoreType.DMA((2,))],
    )
    return pl.pallas_call(
        _moe_kernel,
        grid_spec=grid_spec,
        out_shape=jax.ShapeDtypeStruct((MOE_CAP, D), F32),
        compiler_params=_cparams(("arbitrary",)),
    )(block_e, first, nused, slot_tok, slot_tok, h2, w_gate, w_up, w_down)


def _cmb_row_copy(y_hbm, buf, sem, dest_ref, slot, r):
    return pltpu.make_async_copy(y_hbm.at[pl.ds(dest_ref[0, 0, r], 1)], buf.at[slot, pl.ds(r, 1)], sem.at[slot])


def _combine_kernel(has_next, dest_ref, destn_ref, y_hbm, w_ref, h_ref, x_ref, wsg_ref, wsu_ref, wsd_ref,
                    gate_ref, lng_ref, lnb_ref, *rest):
    if has_next:
        sh_ref, sc_ref, xo_ref, ho_ref, buf, sem = rest
    else:
        xo_ref, buf, sem = rest
    i = pl.program_id(0)
    n = pl.num_programs(0)
    slot = i % 2
    nrow = TOP_K * CMB_TM

    def gather(dref, s):
        def body(r, carry):
            _cmb_row_copy(y_hbm, buf, sem, dref, s, r).start()
            return carry
        lax.fori_loop(0, nrow, body, 0, unroll=8)

    @pl.when(i == 0)
    def _():
        gather(dest_ref, 0)

    @pl.when(i + 1 < n)
    def _():
        gather(destn_ref, 1 - slot)

    hb = h_ref[...].astype(BF16)
    a = (_silu(jnp.dot(hb, wsg_ref[...], preferred_element_type=F32))
         * jnp.dot(hb, wsu_ref[...], preferred_element_type=F32))
    shared = jnp.dot(a.astype(BF16), wsd_ref[...], preferred_element_type=F32)

    pltpu.make_async_copy(y_hbm.at[pl.ds(0, nrow)], buf.at[slot], sem.at[slot]).wait()
    w = w_ref[...]
    routed = buf[slot, 0:CMB_TM, :] * w[:, 0:1]
    for k in range(1, TOP_K):
        routed = routed + buf[slot, k * CMB_TM:(k + 1) * CMB_TM, :] * w[:, k:k + 1]
    xn = _layer_norm(ALPHA * x_ref[...] + gate_ref[...] * (routed + shared), lng_ref[...], lnb_ref[...])
    xo_ref[...] = xn
    if has_next:
        ho_ref[...] = (xn * (1.0 + sc_ref[...]) + sh_ref[...]).astype(ho_ref.dtype)


def moe_combine(y_slots, dest, w_tok, h2, x1, ws_gate, ws_up, ws_down, mods, ln_g, ln_b, layer):
    has_next = layer + 1 < DEPTH
    tm = CMB_TM
    ntile = NT // tm
    row = lambda m: (m, 0)
    const = lambda m: (0, 0)
    in_specs = [pl.BlockSpec((1, 1, TOP_K * tm), lambda m: (m, 0, 0), memory_space=pltpu.SMEM),
                pl.BlockSpec((1, 1, TOP_K * tm), lambda m: (jnp.minimum(m + 1, ntile - 1), 0, 0),
                             memory_space=pltpu.SMEM),
                pl.BlockSpec(memory_space=pl.ANY),
                pl.BlockSpec((tm, 8), row),
                pl.BlockSpec((tm, D), row), pl.BlockSpec((tm, D), row),
                pl.BlockSpec((D, D_SHARED), const), pl.BlockSpec((D, D_SHARED), const),
                pl.BlockSpec((D_SHARED, D), const),
                _mod_spec(layer, 5, tm),
                pl.BlockSpec((None, None, 1, D), lambda m: (layer, 1, 0, 0)),
                pl.BlockSpec((None, None, 1, D), lambda m: (layer, 1, 0, 0))]
    args = [dest, dest, y_slots, w_tok, h2, x1, ws_gate, ws_up, ws_down, mods, ln_g, ln_b]
    out_specs = [pl.BlockSpec((tm, D), row)]
    out_shape = [jax.ShapeDtypeStruct((NT, D), F32)]
    if has_next:
        in_specs += [_mod_spec(layer + 1, 0, tm), _mod_spec(layer + 1, 1, tm)]
        args += [mods, mods]
        out_specs.append(pl.BlockSpec((tm, D), row))
        out_shape.append(jax.ShapeDtypeStruct((NT, D), BF16))
    res = pl.pallas_call(
        functools.partial(_combine_kernel, has_next),
        grid=(ntile,),
        in_specs=in_specs,
        out_specs=out_specs,
        out_shape=out_shape,
        scratch_shapes=[pltpu.VMEM((2, TOP_K * tm, D), F32), pltpu.SemaphoreType.DMA((2,))],
        compiler_params=_cparams(("arbitrary",)),
    )(*args)
    return (res[0], res[1]) if has_next else (res[0], None)


def _route(scores, b_router):
    n = scores.shape[0]
    biased = scores + b_router.astype(F32)
    grp_score = lax.top_k(biased.reshape(n, N_GROUPS, EXPERTS_PER_GROUP), 2)[0].sum(-1)
    _, top_g = lax.top_k(grp_score, TOPK_GROUPS)
    g_mask = jax.nn.one_hot(top_g, N_GROUPS, dtype=F32).sum(1) > 0
    masked = jnp.where(jnp.repeat(g_mask, EXPERTS_PER_GROUP, axis=1), biased, -jnp.inf)
    _, top_e = lax.top_k(masked, TOP_K)
    w = jnp.take_along_axis(scores, top_e, axis=1)
    w = w / jnp.sum(w, -1, keepdims=True) * ROUTED_SCALE
    n_assign = n * TOP_K
    flat_e = top_e.reshape(-1)
    order = jnp.argsort(flat_e)
    sorted_e = flat_e[order]
    sorted_tok = (order // TOP_K).astype(jnp.int32)
    counts = jnp.bincount(flat_e, length=N_EXPERTS)
    padded = (counts + MOE_TB - 1) // MOE_TB * MOE_TB
    pad_end = jnp.cumsum(padded)
    pad_start = pad_end - padded
    start = jnp.cumsum(counts) - counts
    dest = (pad_start[sorted_e] + jnp.arange(n_assign) - start[sorted_e]).astype(jnp.int32)
    slot_tok = jnp.zeros((MOE_CAP,), jnp.int32).at[dest].set(sorted_tok)
    nused = (pad_end[-1] // MOE_TB).astype(jnp.int32)
    blk = jnp.minimum(jnp.arange(MOE_NBLK), nused - 1) * MOE_TB
    block_e = jnp.minimum(jnp.searchsorted(pad_end, blk, side='right'), N_EXPERTS - 1).astype(jnp.int32)
    first = jnp.concatenate([jnp.ones((1,), jnp.int32), (block_e[1:] != block_e[:-1]).astype(jnp.int32)])
    dest_tok = jnp.zeros((n_assign,), jnp.int32).at[order].set(dest).reshape(n, TOP_K)
    dest_tiles = dest_tok.reshape(n // CMB_TM, CMB_TM, TOP_K).transpose(0, 2, 1).reshape(n // CMB_TM, 1, TOP_K * CMB_TM)
    w_tok = jnp.pad(w, ((0, 0), (0, 8 - TOP_K)))
    return (slot_tok.reshape(MOE_NBLK, 1, MOE_TB), block_e, first, nused.reshape(1), dest_tiles, w_tok)


def kernel(x_prompt, x_sample, cache_diff_k, cache_diff_v, cache_na_k, cache_na_v, state_ret, c, c_ctx, ada_w, ada_b, ln_g, ln_b, diff_w_in, diff_w_out, diff_lambda, diff_subln_g, na_w_in, na_w_out, na_rpb, ret_w_in, ret_w_out, ret_decay, moe_router, moe_router_bias, moe_w_gate, moe_w_up, moe_w_down, shared_w_gate, shared_w_up, shared_w_down):
    x = jnp.concatenate([x_prompt.reshape(NP, D), x_sample.reshape(NS, D)], axis=0)
    cond = jnp.concatenate([c_ctx[None], c, jnp.zeros((N_COND - 1 - DEC_BATCH, D), F32)], axis=0)
    mods = modulation_table(cond, ada_w, ada_b)
    ln_g4 = ln_g.reshape(DEPTH, 2, 1, D)
    ln_b4 = ln_b.reshape(DEPTH, 2, 1, D)
    cos, sin = _rope_tables()
    h = modulate(x, mods, 0)
    new_dk, new_dv, new_nk, new_nv, new_rs = [], [], [], [], []
    for i in range(DEPTH):
        j = i // N_MIXERS
        kind = i % N_MIXERS
        if kind == 0:
            w = diff_w_in[j].astype(BF16)
            wq, wk, wv = w[:, :D], w[:, D:2 * D], w[:, 2 * D:]
            qp = matmul(h, wq, 0, NP, BF16)
            kp = matmul(h, wk, 0, NP, F32)
            vp = matmul(h, wv, 0, NP, F32)
            qs = rope(matmul(h, wq, NP, NS, F32), cos, sin)
            ks = rope(matmul(h, wk, NP, NS, F32), cos, sin)
            vs = matmul(h, wv, NP, NS, BF16)
            lam_init = 0.8 - 0.6 * math.exp(-0.3 * i)
            op = diff_attention(qp, kp, vp, diff_lambda[j], diff_subln_g[j], lam_init, BATCH, SEQ, SEQ, SEQ)
            k_all = jnp.concatenate([cache_diff_k[:, j].reshape(DEC_BATCH, PAST, D).astype(BF16),
                                     ks.reshape(DEC_BATCH, DEC_SEQ, D)], axis=1).reshape(-1, D)
            v_all = jnp.concatenate([cache_diff_v[:, j].reshape(DEC_BATCH, PAST, D).astype(BF16),
                                     vs.reshape(DEC_BATCH, DEC_SEQ, D)], axis=1).reshape(-1, D)
            os_ = diff_attention(qs, k_all, v_all, diff_lambda[j], diff_subln_g[j], lam_init,
                                 DEC_BATCH, DEC_SEQ, PAST + DEC_SEQ, 256)
            new_dk.append(kp.reshape(BATCH, SEQ, DIFF_HEADS, 2 * DIFF_QK))
            new_dv.append(vp.reshape(BATCH, SEQ, DIFF_HEADS, DIFF_V))
            w_out = diff_w_out[j]
        elif kind == 1:
            w = na_w_in[j].astype(BF16)
            wq, wk, wv = w[:, :D], w[:, D:2 * D], w[:, 2 * D:]
            qp = matmul(h, wq, 0, NP, BF16)
            kp = matmul(h, wk, 0, NP, F32)
            vp = matmul(h, wv, 0, NP, F32)
            qs = matmul(h, wq, NP, NS, BF16)
            ks = matmul(h, wk, NP, NS, BF16)
            vs = matmul(h, wv, NP, NS, BF16)
            op = na_context_attention(qp, kp, vp)
            os_ = na_latent_attention(qs, ks, vs,
                                      cache_na_k[:, j].reshape(DEC_BATCH, PAST, D).astype(BF16),
                                      cache_na_v[:, j].reshape(DEC_BATCH, PAST, D).astype(BF16),
                                      _na_bias_tables(na_rpb[j]))
            new_nk.append(kp.reshape(BATCH, SEQ, NA_HEADS, NA_DH))
            new_nv.append(vp.reshape(BATCH, SEQ, NA_HEADS, NA_DH))
            w_out = na_w_out[j]
        else:
            w = ret_w_in[j].astype(BF16)
            qd = RET_HEADS * RET_QK
            vd = RET_HEADS * RET_V
            wq, wk, wv, wg = w[:, :qd], w[:, qd:2 * qd], w[:, 2 * qd:2 * qd + vd], w[:, 2 * qd + vd:]
            log_g = -jax.nn.softplus(-ret_decay[j].astype(F32))
            outs = []
            for row0, nrows, nb, l in ((0, NP, BATCH, SEQ), (NP, NS, DEC_BATCH, DEC_SEQ)):
                q_ = matmul(h, wq, row0, nrows, BF16)
                k_ = matmul(h, wk, row0, nrows, F32)
                v_ = matmul(h, wv, row0, nrows, BF16)
                g_ = matmul(h, wg, row0, nrows, F32)
                if row0 == 0:
                    o_, st = retention(q_, k_, v_, g_, log_g, nb, l, None, True)
                    new_rs.append(st)
                else:
                    o_ = retention(q_, k_, v_, g_, log_g, nb, l, state_ret[:, j], False)
                outs.append(o_)
            op, os_ = outs
            w_out = ret_w_out[j]
        o = jnp.concatenate([op, os_], axis=0)
        x1, h2, scores = outproj_ln(o, w_out.astype(BF16), x, mods, ln_g4, ln_b4, moe_router[i], i)
        slot_tok, block_e, first, nused, dest_tiles, w_tok = _route(scores, moe_router_bias[i])
        y_slots = moe_experts(h2, slot_tok, block_e, first, nused, moe_w_gate, moe_w_up, moe_w_down, i)
        x, h = moe_combine(y_slots, dest_tiles, w_tok, h2, x1,
                           shared_w_gate[i].astype(BF16), shared_w_up[i].astype(BF16),
                           shared_w_down[i].astype(BF16), mods, ln_g4, ln_b4, i)
    return (x[:NP].reshape(BATCH, SEQ, D), x[NP:].reshape(DEC_BATCH, DEC_SEQ, D),
            jnp.stack(new_dk, axis=1), jnp.stack(new_dv, axis=1),
            jnp.stack(new_nk, axis=1), jnp.stack(new_nv, axis=1), jnp.stack(new_rs, axis=1))
```

```python
import functools
import math

import jax
import jax.numpy as jnp
import numpy as np
from jax import lax
from jax.experimental import pallas as pl
from jax.experimental.pallas import tpu as pltpu

F32 = jnp.float32
BF16 = jnp.bfloat16
U32 = jnp.uint32
I32 = jnp.int32

D = 2048
BATCH = 32
SEQ = 256
DEPTH = 4
DEC_BATCH = 8
DEC_SEQ = 1024
PAST = 512
NP = BATCH * SEQ
NS = DEC_BATCH * DEC_SEQ
NT = NP + NS
GRID_W = 64
N_MIXERS = 3
DIFF_HEADS = 8
DIFF_QK = 128
DIFF_V = 256
NA_HEADS = 16
NA_DH = 128
NA_KH = 8
NA_KH_MAX = 8
NA_KW = 16
RET_HEADS = 8
RET_QK = 256
RET_V = 512
RET_CHUNK = 128
N_EXPERTS = 64
N_GROUPS = 8
EXPERTS_PER_GROUP = 8
TOPK_GROUPS = 4
TOP_K = 6
D_EXPERT = 512
D_SHARED = 512
ROUTED_SCALE = 2.5
ROPE_BASE = 10000.0
LN_EPS = 1e-5
ALPHA = (2 * DEPTH) ** 0.25
N_COND = 16

VMEM_LIMIT = 56 * 1024 * 1024

MOE_TB = 256
MOE_NBLK = (NT * TOP_K) // MOE_TB + N_EXPERTS
MOE_CAP = MOE_NBLK * MOE_TB
RT_TM = 256
CMB_TM = 128
PK_S = 8
PK_L = D // 2 // PK_S
Y_S = D // 128
HI_MASK = np.uint32(0xFFFF0000)


def _cparams(sem):
    return pltpu.CompilerParams(dimension_semantics=sem, vmem_limit_bytes=VMEM_LIMIT)


def _cond_row(row0):
    return jnp.where(row0 < NP, 0, 1 + (row0 - NP) // DEC_SEQ)


def _mod_spec(layer, chunk, tm, moff=0):
    return pl.BlockSpec((None, None, None, 1, D),
                        lambda m, *_: (layer, _cond_row((m + moff) * tm), chunk, 0, 0))


def _sigmoid(x):
    return 1.0 / (1.0 + jnp.exp(-x))


def _silu(x):
    return x * _sigmoid(x)


def _mod_kernel(cond_ref, w_ref, b_ref, o_ref):
    c = _silu(cond_ref[...]).astype(BF16)
    o_ref[...] = jnp.dot(c, w_ref[...].astype(BF16), preferred_element_type=F32) + b_ref[...]


def modulation_table(cond, ada_w, ada_b):
    tn = 1024
    n6 = 6 * D
    out = pl.pallas_call(
        _mod_kernel,
        grid=(DEPTH, n6 // tn),
        in_specs=[pl.BlockSpec((N_COND, D), lambda l, n: (0, 0)),
                  pl.BlockSpec((None, D, tn), lambda l, n: (l, 0, n)),
                  pl.BlockSpec((None, 1, tn), lambda l, n: (l, 0, n))],
        out_specs=pl.BlockSpec((None, N_COND, tn), lambda l, n: (l, 0, n)),
        out_shape=jax.ShapeDtypeStruct((DEPTH, N_COND, n6), F32),
        compiler_params=_cparams(("arbitrary", "arbitrary")),
        name="modulation_table",
    )(cond, ada_w, ada_b.reshape(DEPTH, 1, n6))
    return out.reshape(DEPTH, N_COND, 6, 1, D)


def _modulate_kernel(x_ref, sh_ref, sc_ref, o_ref):
    o_ref[...] = (x_ref[...] * (1.0 + sc_ref[...]) + sh_ref[...]).astype(o_ref.dtype)


def modulate(x, mods, layer):
    tm = 512
    return pl.pallas_call(
        _modulate_kernel,
        grid=(NT // tm,),
        in_specs=[pl.BlockSpec((tm, D), lambda m: (m, 0)),
                  _mod_spec(layer, 0, tm), _mod_spec(layer, 1, tm)],
        out_specs=pl.BlockSpec((tm, D), lambda m: (m, 0)),
        out_shape=jax.ShapeDtypeStruct((NT, D), BF16),
        compiler_params=_cparams(("arbitrary",)),
        name="modulate",
    )(x, mods, mods)


def _mm_kernel(a_ref, w_ref, o_ref):
    o_ref[...] = jnp.dot(a_ref[...], w_ref[...], preferred_element_type=F32).astype(o_ref.dtype)


def matmul(a, w, row0, nrows, out_dtype):
    k = a.shape[1]
    n = w.shape[1]
    tm, tn = 1024, 512
    moff = row0 // tm
    return pl.pallas_call(
        _mm_kernel,
        grid=(nrows // tm, n // tn),
        in_specs=[pl.BlockSpec((tm, k), lambda m, j: (m + moff, 0)),
                  pl.BlockSpec((k, tn), lambda m, j: (0, j))],
        out_specs=pl.BlockSpec((tm, tn), lambda m, j: (m, j)),
        out_shape=jax.ShapeDtypeStruct((nrows, n), out_dtype),
        compiler_params=_cparams(("arbitrary", "arbitrary")),
        name="in_proj",
    )(a, w)


def _rope_tables():
    half = DIFF_QK // 2
    t = jnp.arange(DEC_SEQ)
    inv = ROPE_BASE ** (-jnp.arange(0, half, 2, dtype=F32) / half)

    def cs(pos):
        ang = pos.astype(F32)[:, None] * inv[None, :]
        return jnp.cos(ang), jnp.sin(ang)

    cr, sr = cs(t // GRID_W)
    cc, sc = cs(t % GRID_W)
    cos = jnp.concatenate([cr, cr, cc, cc], axis=-1)
    sin = jnp.concatenate([-sr, sr, -sc, sc], axis=-1)
    return cos, sin


def _rope_kernel(x_ref, cos_ref, sin_ref, o_ref):
    cos = cos_ref[...]
    sin = sin_ref[...]
    lane = lax.broadcasted_iota(jnp.int32, cos.shape, 1)
    first = (lane % (DIFF_QK // 2)) < (DIFF_QK // 4)
    for g in range(D // DIFF_QK):
        xg = x_ref[:, g * DIFF_QK:(g + 1) * DIFF_QK]
        sw = jnp.where(first, pltpu.roll(xg, DIFF_QK - DIFF_QK // 4, 1), pltpu.roll(xg, DIFF_QK // 4, 1))
        o_ref[:, g * DIFF_QK:(g + 1) * DIFF_QK] = (xg * cos + sw * sin).astype(o_ref.dtype)


def rope(x, cos, sin):
    tm = 512
    nt = DEC_SEQ // tm
    return pl.pallas_call(
        _rope_kernel,
        grid=(NS // tm,),
        in_specs=[pl.BlockSpec((tm, D), lambda m: (m, 0)),
                  pl.BlockSpec((tm, DIFF_QK), lambda m: (m % nt, 0)),
                  pl.BlockSpec((tm, DIFF_QK), lambda m: (m % nt, 0))],
        out_specs=pl.BlockSpec((tm, D), lambda m: (m, 0)),
        out_shape=jax.ShapeDtypeStruct((NS, D), BF16),
        compiler_params=_cparams(("arbitrary",)),
        name="rope",
    )(x, cos, sin)


def _qkt(q, k):
    return lax.dot_general(q, k, (((1,), (1,)), ((), ())), preferred_element_type=F32)


def _softmax(s):
    m = jnp.max(s, axis=-1, keepdims=True)
    p = jnp.exp(s - m)
    return p / jnp.sum(p, axis=-1, keepdims=True)


def _diff_kernel(lam_init, lamp_ref, g_ref, q_ref, k_ref, v_ref, o_ref):
    lp = lamp_ref[...]
    lam = (jnp.exp(jnp.sum(lp[0:1] * lp[1:2], axis=-1, keepdims=True))
           - jnp.exp(jnp.sum(lp[2:3] * lp[3:4], axis=-1, keepdims=True)) + lam_init)
    scale = DIFF_QK ** -0.5
    g = g_ref[...]
    for h in range(DIFF_HEADS):
        c0 = h * DIFF_V
        q = q_ref[:, c0:c0 + DIFF_V].astype(BF16)
        k = k_ref[:, c0:c0 + DIFF_V].astype(BF16)
        v = v_ref[:, c0:c0 + DIFF_V].astype(BF16)
        a1 = _softmax(_qkt(q[:, :DIFF_QK], k[:, :DIFF_QK]) * scale)
        a2 = _softmax(_qkt(q[:, DIFF_QK:], k[:, DIFF_QK:]) * scale)
        a = (a1 - lam * a2).astype(BF16)
        o = jnp.dot(a, v, preferred_element_type=F32)
        o = o * lax.rsqrt(jnp.mean(jnp.square(o), axis=-1, keepdims=True) + LN_EPS) * g
        o_ref[:, c0:c0 + DIFF_V] = (o * (1.0 - lam_init)).astype(o_ref.dtype)


def diff_attention(q, k, v, lam_p, subln_g, lam_init, nb, lq, lk, tq):
    nq = lq // tq
    return pl.pallas_call(
        functools.partial(_diff_kernel, lam_init),
        grid=(nb, nq),
        in_specs=[pl.BlockSpec((4, DIFF_QK), lambda b, i: (0, 0)),
                  pl.BlockSpec((1, DIFF_V), lambda b, i: (0, 0)),
                  pl.BlockSpec((tq, D), lambda b, i: (b * nq + i, 0)),
                  pl.BlockSpec((lk, D), lambda b, i: (b, 0)),
                  pl.BlockSpec((lk, D), lambda b, i: (b, 0))],
        out_specs=pl.BlockSpec((tq, D), lambda b, i: (b * nq + i, 0)),
        out_shape=jax.ShapeDtypeStruct((nb * lq, D), BF16),
        compiler_params=_cparams(("arbitrary", "arbitrary")),
        name="diff_attention",
    )(lam_p, subln_g.reshape(1, DIFF_V), q, k, v)


def _na_ctx_kernel(q_ref, k_ref, v_ref, o_ref):
    scale = NA_DH ** -0.5
    for h in range(NA_HEADS):
        c0 = h * NA_DH
        q = q_ref[:, c0:c0 + NA_DH].astype(BF16)
        k = k_ref[:, c0:c0 + NA_DH].astype(BF16)
        v = v_ref[:, c0:c0 + NA_DH].astype(BF16)
        p = _softmax(_qkt(q, k) * scale).astype(BF16)
        o_ref[:, c0:c0 + NA_DH] = jnp.dot(p, v, preferred_element_type=F32).astype(o_ref.dtype)


def na_context_attention(q, k, v):
    spec = pl.BlockSpec((SEQ, D), lambda b: (b, 0))
    return pl.pallas_call(
        _na_ctx_kernel,
        grid=(BATCH,),
        in_specs=[spec, spec, spec],
        out_specs=spec,
        out_shape=jax.ShapeDtypeStruct((NP, D), BF16),
        compiler_params=_cparams(("arbitrary",)),
        name="na_context_attention",
    )(q, k, v)


NA_ROWS = DEC_SEQ // GRID_W
NA_NLOC = NA_KH * GRID_W


def _na_window_row(r):
    return jnp.clip(r - NA_KH // 2, 0, NA_ROWS - NA_KH)


def _na_bias_tables(rpb):
    col = jnp.arange(GRID_W)
    col_start = jnp.clip(col - NA_KW // 2, 0, GRID_W - NA_KW)
    col_ok = (col[None, :] >= col_start[:, None]) & (col[None, :] < col_start[:, None] + NA_KW)
    mask = jnp.broadcast_to(col_ok[:, None, :], (GRID_W, NA_KH, GRID_W)).reshape(GRID_W, NA_NLOC)
    dc_idx = jnp.clip(col[None, :] - col[:, None] + NA_KW - 1, 0, 2 * NA_KW - 2)
    tabs = []
    for o in range(NA_KH):
        dr_idx = jnp.arange(NA_KH) - o + NA_KH_MAX - 1
        bias = rpb[:, dr_idx[None, :, None], dc_idx[:, None, :]].reshape(NA_HEADS, GRID_W, NA_NLOC)
        tabs.append(jnp.where(mask[None], bias, -jnp.inf))
    return jnp.stack(tabs)


def _na_lat_kernel(q_ref, k_ref, v_ref, kc_ref, vc_ref, bias_ref, o_ref):
    r = pl.program_id(1)
    start = pl.multiple_of(_na_window_row(r) * GRID_W, GRID_W)
    scale = NA_DH ** -0.5
    for h in range(NA_HEADS):
        c0 = h * NA_DH
        q = q_ref[:, c0:c0 + NA_DH]
        kw = k_ref[pl.ds(start, NA_NLOC), c0:c0 + NA_DH]
        vw = v_ref[pl.ds(start, NA_NLOC), c0:c0 + NA_DH]
        s_loc = _qkt(q, kw) * scale + bias_ref[h]
        s_ctx = _qkt(q, kc_ref[:, c0:c0 + NA_DH]) * scale
        m = jnp.maximum(jnp.max(s_loc, axis=-1, keepdims=True), jnp.max(s_ctx, axis=-1, keepdims=True))
        p_loc = jnp.exp(s_loc - m)
        p_ctx = jnp.exp(s_ctx - m)
        den = jnp.sum(p_loc, axis=-1, keepdims=True) + jnp.sum(p_ctx, axis=-1, keepdims=True)
        o = (jnp.dot((p_loc / den).astype(BF16), vw, preferred_element_type=F32)
             + jnp.dot((p_ctx / den).astype(BF16), vc_ref[:, c0:c0 + NA_DH], preferred_element_type=F32))
        o_ref[:, c0:c0 + NA_DH] = o.astype(o_ref.dtype)


def na_latent_attention(q, k, v, k_ctx, v_ctx, bias_tabs):
    return pl.pallas_call(
        _na_lat_kernel,
        grid=(DEC_BATCH, NA_ROWS),
        in_specs=[pl.BlockSpec((GRID_W, D), lambda b, r: (b * NA_ROWS + r, 0)),
                  pl.BlockSpec((DEC_SEQ, D), lambda b, r: (b, 0)),
                  pl.BlockSpec((DEC_SEQ, D), lambda b, r: (b, 0)),
                  pl.BlockSpec((None, PAST, D), lambda b, r: (b, 0, 0)),
                  pl.BlockSpec((None, PAST, D), lambda b, r: (b, 0, 0)),
                  pl.BlockSpec((None, NA_HEADS, GRID_W, NA_NLOC),
                               lambda b, r: (r - _na_window_row(r), 0, 0, 0))],
        out_specs=pl.BlockSpec((GRID_W, D), lambda b, r: (b * NA_ROWS + r, 0)),
        out_shape=jax.ShapeDtypeStruct((NS, D), BF16),
        compiler_params=_cparams(("arbitrary", "arbitrary")),
        name="na_latent_attention",
    )(q, k, v, k_ctx, v_ctx, bias_tabs)


def _ret_kernel(nc, has_state, emit_state, logg_ref, q_ref, k_ref, v_ref, g_ref, *rest):
    rest = list(rest)
    s0_ref = rest.pop(0) if has_state else None
    o_ref = rest.pop(0)
    st_ref = rest.pop(0) if emit_state else None
    oacc, sacc = rest
    h = pl.program_id(1)
    lgf = logg_ref[0, h]
    lgb = logg_ref[1, h]
    c = RET_CHUNK
    pos = lax.broadcasted_iota(jnp.int32, (c, 1), 0).astype(F32)
    diff = (lax.broadcasted_iota(jnp.int32, (c, c), 0) - lax.broadcasted_iota(jnp.int32, (c, c), 1)).astype(F32)
    dmask_f = jnp.where(diff >= 0, jnp.exp(jnp.maximum(diff, 0.0) * lgf), 0.0)
    dmask_b = jnp.where(diff <= 0, jnp.exp(jnp.maximum(-diff, 0.0) * lgb), 0.0)
    qdec_f = jnp.exp((pos + 1.0) * lgf)
    kdec_f = jnp.exp((c - 1.0 - pos) * lgf)
    qdec_b = jnp.exp((c - pos) * lgb)
    kdec_b = jnp.exp(pos * lgb)
    cd_f = jnp.exp(jnp.full((1, 1), c, F32) * lgf)
    cd_b = jnp.exp(jnp.full((1, 1), c, F32) * lgb)

    def chunk(i):
        sl = slice(i * c, (i + 1) * c)
        return q_ref[sl, :].astype(BF16), k_ref[sl, :] * (RET_QK ** -0.5), v_ref[sl, :].astype(BF16)

    def state_update(kc, kdec, vb, cd):
        kd = (kc * kdec).T.astype(BF16)
        sacc[...] = sacc[...] * cd + jnp.dot(kd, vb, preferred_element_type=F32)

    if has_state:
        sacc[...] = s0_ref[0]
    else:
        sacc[...] = jnp.zeros_like(sacc)
    for i in range(nc):
        qb, kc, vb = chunk(i)
        qk = _qkt(qb, kc.astype(BF16))
        o = (jnp.dot((qk * dmask_f).astype(BF16), vb, preferred_element_type=F32)
             + jnp.dot((qk * dmask_b).astype(BF16), vb, preferred_element_type=F32)
             + jnp.dot(qb, sacc[...].astype(BF16), preferred_element_type=F32) * qdec_f)
        oacc[i * c:(i + 1) * c, :] = o
        state_update(kc, kdec_f, vb, cd_f)
    if emit_state:
        st_ref[0] = sacc[...]

    if has_state:
        sacc[...] = s0_ref[1]
    else:
        sacc[...] = jnp.zeros_like(sacc)
    for i in reversed(range(nc)):
        qb, kc, vb = chunk(i)
        oacc[i * c:(i + 1) * c, :] += jnp.dot(qb, sacc[...].astype(BF16), preferred_element_type=F32) * qdec_b
        state_update(kc, kdec_b, vb, cd_b)
    if emit_state:
        st_ref[1] = sacc[...]

    o = oacc[...]
    mu = jnp.mean(o, axis=-1, keepdims=True)
    var = jnp.mean(jnp.square(o - mu), axis=-1, keepdims=True)
    o = (o - mu) * lax.rsqrt(var + LN_EPS)
    o_ref[...] = (o * _silu(g_ref[...])).astype(o_ref.dtype)


def retention(q, k, v, g, log_g, nb, l, state=None, emit_state=False):
    nc = l // RET_CHUNK
    has_state = state is not None
    in_specs = [pl.BlockSpec(memory_space=pltpu.SMEM),
                pl.BlockSpec((l, RET_QK), lambda b, h: (b, h)),
                pl.BlockSpec((l, RET_QK), lambda b, h: (b, h)),
                pl.BlockSpec((l, RET_V), lambda b, h: (b, h)),
                pl.BlockSpec((l, RET_V), lambda b, h: (b, h))]
    args = [log_g, q, k, v, g]
    st_spec = pl.BlockSpec((None, 2, None, RET_QK, RET_V), lambda b, h: (b, 0, h, 0, 0))
    if has_state:
        in_specs.append(st_spec)
        args.append(state)
    out_specs = [pl.BlockSpec((l, RET_V), lambda b, h: (b, h))]
    out_shape = [jax.ShapeDtypeStruct((nb * l, RET_HEADS * RET_V), BF16)]
    if emit_state:
        out_specs.append(st_spec)
        out_shape.append(jax.ShapeDtypeStruct((nb, 2, RET_HEADS, RET_QK, RET_V), F32))
    res = pl.pallas_call(
        functools.partial(_ret_kernel, nc, has_state, emit_state),
        grid=(nb, RET_HEADS),
        in_specs=in_specs,
        out_specs=out_specs,
        out_shape=out_shape,
        scratch_shapes=[pltpu.VMEM((l, RET_V), F32), pltpu.VMEM((RET_QK, RET_V), F32)],
        compiler_params=_cparams(("arbitrary", "arbitrary")),
        name="retention",
    )(*args)
    return res if emit_state else res[0]


def _layer_norm(z, g, b):
    mu = jnp.mean(z, axis=-1, keepdims=True)
    var = jnp.mean(jnp.square(z - mu), axis=-1, keepdims=True)
    return (z - mu) * lax.rsqrt(var + LN_EPS) * g + b


def _pack_rows(hn, hp_ref):
    tm = hn.shape[0]
    half = D // 2
    lo = lax.bitcast_convert_type(hn[:, :half].astype(BF16).astype(F32), U32)
    hi = lax.bitcast_convert_type(hn[:, half:].astype(BF16).astype(F32), U32)
    w = (lo >> 16) | (hi & HI_MASK)
    for s in range(PK_S):
        hp_ref[pl.ds(s, tm, stride=PK_S), :] = w[:, s * PK_L:(s + 1) * PK_L]


def _unpack_rows(hp_ref, tm):
    lo, hi = [], []
    for s in range(PK_S):
        w = hp_ref[pl.ds(s, tm, stride=PK_S), :]
        lo.append(lax.bitcast_convert_type(w << 16, F32).astype(BF16))
        hi.append(lax.bitcast_convert_type(w & HI_MASK, F32).astype(BF16))
    return jnp.concatenate(lo + hi, axis=1)


def _route_tile(hn, wrt_ref, rb_ref, e_ref, w_ref, rank_ref, cnt_ref):
    tm = hn.shape[0]
    neg = -jnp.inf
    logits = lax.dot_general(wrt_ref[...], hn, (((1,), (1,)), ((), ())),
                             preferred_element_type=F32, precision=lax.Precision.HIGHEST)
    s = _sigmoid(logits)
    biased = s + rb_ref[...]
    io8 = lax.broadcasted_iota(I32, (EXPERTS_PER_GROUP, tm), 0).astype(F32)
    slabs, gscore = [], []
    for g in range(N_GROUPS):
        slab = biased[g * EXPERTS_PER_GROUP:(g + 1) * EXPERTS_PER_GROUP, :]
        m1 = jnp.max(slab, axis=0, keepdims=True)
        i1 = jnp.min(jnp.where(slab == m1, io8, float(EXPERTS_PER_GROUP)), axis=0, keepdims=True)
        m2 = jnp.max(jnp.where(io8 == i1, neg, slab), axis=0, keepdims=True)
        slabs.append(slab)
        gscore.append(m1 + m2)
    masked = []
    for g in range(N_GROUPS):
        ahead = jnp.zeros_like(gscore[g])
        for g2 in range(N_GROUPS):
            if g2 == g:
                continue
            better = (gscore[g2] > gscore[g]) | ((gscore[g2] == gscore[g]) if g2 < g else False)
            ahead = ahead + jnp.where(better, 1.0, 0.0)
        masked.append(jnp.where(ahead < float(TOPK_GROUPS), slabs[g], neg))
    v = jnp.concatenate(masked, axis=0)
    io = lax.broadcasted_iota(I32, (N_EXPERTS, tm), 0).astype(F32)
    idxs, ws, hots = [], [], []
    for _ in range(TOP_K):
        m = jnp.max(v, axis=0, keepdims=True)
        idx = jnp.min(jnp.where(v == m, io, float(N_EXPERTS)), axis=0, keepdims=True)
        hot = io == idx
        ws.append(jnp.sum(jnp.where(hot, s, 0.0), axis=0, keepdims=True))
        v = jnp.where(hot, neg, v)
        idxs.append(idx)
        hots.append(hot)
    wsum = ws[0]
    for k in range(1, TOP_K):
        wsum = wsum + ws[k]
    chosen = jnp.where(hots[0], 1.0, 0.0)
    for k in range(1, TOP_K):
        chosen = chosen + jnp.where(hots[k], 1.0, 0.0)
    upper = jnp.where(lax.broadcasted_iota(I32, (tm, tm), 0) < lax.broadcasted_iota(I32, (tm, tm), 1), 1.0, 0.0)
    before = jnp.dot(chosen.astype(BF16), upper.astype(BF16), preferred_element_type=F32)
    zero_row = jnp.zeros((1, tm), F32)
    for k in range(8):
        if k < TOP_K:
            e_ref[k:k + 1, :] = idxs[k].astype(I32)
            w_ref[k:k + 1, :] = ws[k] / wsum * ROUTED_SCALE
            rank_ref[k:k + 1, :] = jnp.sum(jnp.where(hots[k], before, 0.0), axis=0, keepdims=True).astype(I32)
        else:
            e_ref[k:k + 1, :] = zero_row.astype(I32)
            w_ref[k:k + 1, :] = zero_row
            rank_ref[k:k + 1, :] = zero_row.astype(I32)
    cnt = jnp.sum(chosen, axis=1, keepdims=True)
    cnt_ref[...] = jnp.broadcast_to(cnt, (N_EXPERTS, 128)).astype(I32)


def _outproj_kernel(nk, o_ref, w_ref, x_ref, gate_ref, lng_ref, lnb_ref, sh_ref, sc_ref, wrt_ref, rb_ref,
                    xo_ref, hp_ref, e_ref, wt_ref, rank_ref, cnt_ref, acc_ref):
    kk = pl.program_id(1)
    part = jnp.dot(o_ref[...], w_ref[...], preferred_element_type=F32)

    @pl.when(kk == 0)
    def _():
        acc_ref[...] = part

    @pl.when(kk > 0)
    def _():
        acc_ref[...] += part

    @pl.when(kk == nk - 1)
    def _():
        xn = _layer_norm(ALPHA * x_ref[...] + gate_ref[...] * acc_ref[...], lng_ref[...], lnb_ref[...])
        xo_ref[...] = xn
        hn = xn * (1.0 + sc_ref[...]) + sh_ref[...]
        _pack_rows(hn, hp_ref)
        _route_tile(hn, wrt_ref, rb_ref, e_ref, wt_ref, rank_ref, cnt_ref)


def outproj_ln(o, w_out, x, mods, ln_g, ln_b, w_router_t, b_router, layer):
    k = o.shape[1]
    tm, tk = RT_TM, 2048
    nk = k // tk
    ntile = NT // tm
    row = lambda m, kk: (m, 0)
    col = lambda m, kk: (0, m)
    return pl.pallas_call(
        functools.partial(_outproj_kernel, nk),
        grid=(ntile, nk),
        in_specs=[pl.BlockSpec((tm, tk), lambda m, kk: (m, kk)),
                  pl.BlockSpec((tk, D), lambda m, kk: (kk, 0)),
                  pl.BlockSpec((tm, D), row),
                  _mod_spec(layer, 2, tm),
                  pl.BlockSpec((None, None, 1, D), lambda m, kk: (layer, 0, 0, 0)),
                  pl.BlockSpec((None, None, 1, D), lambda m, kk: (layer, 0, 0, 0)),
                  _mod_spec(layer, 3, tm), _mod_spec(layer, 4, tm),
                  pl.BlockSpec((N_EXPERTS, D), lambda m, kk: (0, 0)),
                  pl.BlockSpec((N_EXPERTS, 1), lambda m, kk: (0, 0))],
        out_specs=[pl.BlockSpec((tm, D), row),
                   pl.BlockSpec((tm * PK_S, PK_L), row),
                   pl.BlockSpec((8, tm), col), pl.BlockSpec((8, tm), col), pl.BlockSpec((8, tm), col),
                   pl.BlockSpec((None, N_EXPERTS, 128), lambda m, kk: (m, 0, 0))],
        out_shape=[jax.ShapeDtypeStruct((NT, D), F32),
                   jax.ShapeDtypeStruct((NT * PK_S, PK_L), U32),
                   jax.ShapeDtypeStruct((8, NT), I32), jax.ShapeDtypeStruct((8, NT), F32),
                   jax.ShapeDtypeStruct((8, NT), I32),
                   jax.ShapeDtypeStruct((ntile, N_EXPERTS, 128), I32)],
        scratch_shapes=[pltpu.VMEM((tm, D), F32)],
        compiler_params=_cparams(("arbitrary", "arbitrary")),
        name="outproj_ln_route",
    )(o, w_out, x, mods, ln_g, ln_b, mods, mods, w_router_t, b_router.reshape(N_EXPERTS, 1))


def _slot_plan(e_t, rank_t, cnt):
    cnt = cnt[:, :, 0]
    tile_base = jnp.cumsum(cnt, axis=0) - cnt
    counts = jnp.sum(cnt, axis=0)
    padded = (counts + MOE_TB - 1) // MOE_TB * MOE_TB
    pad_end = jnp.cumsum(padded)
    pad_start = pad_end - padded
    offs = jnp.repeat(pad_start[None, :] + tile_base, RT_TM, axis=0)
    hot = e_t[:TOP_K, :, None] == jnp.arange(N_EXPERTS, dtype=I32)[None, None, :]
    dest = jnp.sum(jnp.where(hot, offs[None], 0), axis=-1) + rank_t[:TOP_K]
    ntile = NT // CMB_TM
    dest_tiles = dest.reshape(TOP_K, ntile, CMB_TM).transpose(1, 0, 2).reshape(ntile, 1, TOP_K * CMB_TM)
    nused = pad_end[-1] // MOE_TB
    blk = jnp.minimum(jnp.arange(MOE_NBLK, dtype=I32), nused - 1) * MOE_TB
    block_e = jnp.minimum(jnp.sum((pad_end[None, :] <= blk[:, None]).astype(I32), axis=1), N_EXPERTS - 1)
    first = jnp.concatenate([jnp.ones((1,), I32), (block_e[1:] != block_e[:-1]).astype(I32)])
    zero_start = pad_start + counts
    return (dest_tiles.astype(I32), block_e.astype(I32), first, nused.reshape(1).astype(I32),
            zero_start.astype(I32))


_DISPATCH_ROWS = TOP_K * CMB_TM


def _dispatch_kernel(zs_ref, nused_ref, dest_ref, hp_hbm, xs_hbm, zbuf, zsem, sem):
    i = pl.program_id(0)
    n = pl.num_programs(0)
    slot = i % 2

    @pl.when(i == 0)
    def _():
        zbuf[...] = jnp.zeros_like(zbuf)

        def zcopy(slot0, nslot):
            start = pl.multiple_of(slot0 * PK_S, PK_S)
            return pltpu.make_async_copy(zbuf.at[pl.ds(0, nslot * PK_S)],
                                         xs_hbm.at[pl.ds(start, nslot * PK_S)], zsem)

        def pad_pieces(e, wait):
            slot0 = zs_ref[e]
            npad = (MOE_TB - slot0 % MOE_TB) % MOE_TB
            piece = MOE_TB // 2
            while piece >= 1:
                @pl.when((npad & piece) != 0)
                def _(slot0=slot0, piece=piece):
                    cp = zcopy(slot0, piece)
                    cp.wait() if wait else cp.start()
                slot0 = slot0 + (npad & piece)
                piece //= 2

        def zstart(e, carry):
            pad_pieces(e, False)
            return carry

        def zwait(e, carry):
            pad_pieces(e, True)
            return carry

        def tstart(b, carry):
            zcopy(b * MOE_TB, MOE_TB).start()
            return carry

        def twait(b, carry):
            zcopy(b * MOE_TB, MOE_TB).wait()
            return carry

        lax.fori_loop(0, N_EXPERTS, zstart, 0)
        lax.fori_loop(nused_ref[0], MOE_NBLK + 1, tstart, 0)
        lax.fori_loop(0, N_EXPERTS, zwait, 0)
        lax.fori_loop(nused_ref[0], MOE_NBLK + 1, twait, 0)

    def issue(r, carry):
        for k in range(TOP_K):
            src = pl.multiple_of((i * CMB_TM + r) * PK_S, PK_S)
            dst = pl.multiple_of(dest_ref[0, 0, k * CMB_TM + r] * PK_S, PK_S)
            pltpu.make_async_copy(hp_hbm.at[pl.ds(src, PK_S)], xs_hbm.at[pl.ds(dst, PK_S)],
                                  sem.at[slot]).start()
        return carry

    lax.fori_loop(0, CMB_TM, issue, 0, unroll=2)

    def drain(s):
        pltpu.make_async_copy(hp_hbm.at[pl.ds(0, _DISPATCH_ROWS * PK_S)],
                              xs_hbm.at[pl.ds(0, _DISPATCH_ROWS * PK_S)], sem.at[s]).wait()

    @pl.when(i > 0)
    def _():
        drain(1 - slot)

    @pl.when(i == n - 1)
    def _():
        drain(slot)


def moe_dispatch(hp, dest_tiles, zero_start, nused):
    ntile = NT // CMB_TM
    grid_spec = pltpu.PrefetchScalarGridSpec(
        num_scalar_prefetch=2,
        grid=(ntile,),
        in_specs=[pl.BlockSpec((1, 1, _DISPATCH_ROWS), lambda i, zs, nu: (i, 0, 0), memory_space=pltpu.SMEM),
                  pl.BlockSpec(memory_space=pl.ANY)],
        out_specs=pl.BlockSpec(memory_space=pl.ANY),
        scratch_shapes=[pltpu.VMEM((MOE_TB * PK_S, PK_L), U32),
                        pltpu.SemaphoreType.DMA(()), pltpu.SemaphoreType.DMA((2,))],
    )
    return pl.pallas_call(
        _dispatch_kernel,
        grid_spec=grid_spec,
        out_shape=jax.ShapeDtypeStruct(((MOE_CAP + MOE_TB) * PK_S, PK_L), U32),
        compiler_params=_cparams(("arbitrary",)),
        name="moe_dispatch",
    )(zero_start, nused, dest_tiles, hp)


def _moe_kernel(be_ref, first_ref, nused_ref, xs_ref, wg_ref, wu_ref, wd_ref, y_ref, wgb, wub, wdb):
    i = pl.program_id(0)

    @pl.when(first_ref[i] == 1)
    def _():
        wgb[...] = wg_ref[...].astype(BF16)
        wub[...] = wu_ref[...].astype(BF16)
        wdb[...] = wd_ref[...].astype(BF16)

    @pl.when(i < nused_ref[0])
    def _():
        x = _unpack_rows(xs_ref, MOE_TB)
        a = (_silu(jnp.dot(x, wgb[...], preferred_element_type=F32))
             * jnp.dot(x, wub[...], preferred_element_type=F32))
        y = jnp.dot(a.astype(BF16), wdb[...], preferred_element_type=F32)
        for j in range(Y_S):
            y_ref[pl.ds(j, MOE_TB, stride=Y_S), :] = y[:, j * 128:(j + 1) * 128]

    @pl.when(i >= nused_ref[0])
    def _():
        y_ref[...] = jnp.zeros_like(y_ref)


def moe_experts(xs, block_e, first, nused, w_gate, w_up, w_down, layer):
    wspec = lambda shape: pl.BlockSpec((None, None) + shape, lambda i, be, fi, nu: (layer, be[i], 0, 0))
    blk = lambda i, be, fi, nu: (jnp.minimum(i, nu[0] - 1), 0)
    grid_spec = pltpu.PrefetchScalarGridSpec(
        num_scalar_prefetch=3,
        grid=(MOE_NBLK,),
        in_specs=[pl.BlockSpec((MOE_TB * PK_S, PK_L), blk),
                  wspec((D, D_EXPERT)), wspec((D, D_EXPERT)), wspec((D_EXPERT, D))],
        out_specs=pl.BlockSpec((MOE_TB * Y_S, 128), lambda i, be, fi, nu: (i, 0)),
        scratch_shapes=[pltpu.VMEM((D, D_EXPERT), BF16), pltpu.VMEM((D, D_EXPERT), BF16),
                        pltpu.VMEM((D_EXPERT, D), BF16)],
    )
    return pl.pallas_call(
        _moe_kernel,
        grid_spec=grid_spec,
        out_shape=jax.ShapeDtypeStruct((MOE_CAP * Y_S, 128), F32),
        compiler_params=_cparams(("arbitrary",)),
        name="moe_experts",
    )(block_e, first, nused, xs, w_gate, w_up, w_down)


def _combine_kernel(has_next, dest_ref, destn_ref, y_hbm, w_ref, hp_ref, x_ref, wsg_ref, wsu_ref, wsd_ref,
                    gate_ref, lng_ref, lnb_ref, *rest):
    if has_next:
        sh_ref, sc_ref, xo_ref, ho_ref, buf, sem = rest
    else:
        xo_ref, buf, sem = rest
    i = pl.program_id(0)
    n = pl.num_programs(0)
    slot = i % 2
    nrow = TOP_K * CMB_TM

    def gather(dref, s):
        def body(r, carry):
            for k in range(TOP_K):
                src = pl.multiple_of(dref[0, 0, k * CMB_TM + r] * Y_S, Y_S)
                dst = pl.multiple_of((k * CMB_TM + r) * Y_S, Y_S)
                pltpu.make_async_copy(y_hbm.at[pl.ds(src, Y_S)], buf.at[s, pl.ds(dst, Y_S)],
                                      sem.at[s]).start(priority=k % 2)
            return carry
        lax.fori_loop(0, CMB_TM, body, 0, unroll=2)

    @pl.when(i == 0)
    def _():
        gather(dest_ref, 0)

    @pl.when(i + 1 < n)
    def _():
        gather(destn_ref, 1 - slot)

    hb = _unpack_rows(hp_ref, CMB_TM)
    a = (_silu(jnp.dot(hb, wsg_ref[...], preferred_element_type=F32))
         * jnp.dot(hb, wsu_ref[...], preferred_element_type=F32))
    shared = jnp.dot(a.astype(BF16), wsd_ref[...], preferred_element_type=F32)

    pltpu.make_async_copy(y_hbm.at[pl.ds(0, nrow * Y_S)], buf.at[slot], sem.at[slot]).wait()
    w = w_ref[...]
    pieces = []
    for j in range(Y_S):
        acc = buf[slot, pl.ds(j, CMB_TM, stride=Y_S), :] * w[:, 0:1]
        for k in range(1, TOP_K):
            acc = acc + buf[slot, pl.ds(k * CMB_TM * Y_S + j, CMB_TM, stride=Y_S), :] * w[:, k:k + 1]
        pieces.append(acc)
    routed = jnp.concatenate(pieces, axis=1)
    xn = _layer_norm(ALPHA * x_ref[...] + gate_ref[...] * (routed + shared), lng_ref[...], lnb_ref[...])
    xo_ref[...] = xn
    if has_next:
        ho_ref[...] = (xn * (1.0 + sc_ref[...]) + sh_ref[...]).astype(ho_ref.dtype)


def moe_combine(y_slots, dest_tiles, w_tok, hp, x1, ws_gate, ws_up, ws_down, mods, ln_g, ln_b, layer):
    has_next = layer + 1 < DEPTH
    tm = CMB_TM
    ntile = NT // tm
    row = lambda m: (m, 0)
    const = lambda m: (0, 0)
    in_specs = [pl.BlockSpec((1, 1, TOP_K * tm), lambda m: (m, 0, 0), memory_space=pltpu.SMEM),
                pl.BlockSpec((1, 1, TOP_K * tm), lambda m: (jnp.minimum(m + 1, ntile - 1), 0, 0),
                             memory_space=pltpu.SMEM),
                pl.BlockSpec(memory_space=pl.ANY),
                pl.BlockSpec((tm, 8), row),
                pl.BlockSpec((tm * PK_S, PK_L), row), pl.BlockSpec((tm, D), row),
                pl.BlockSpec((D, D_SHARED), const), pl.BlockSpec((D, D_SHARED), const),
                pl.BlockSpec((D_SHARED, D), const),
                _mod_spec(layer, 5, tm),
                pl.BlockSpec((None, None, 1, D), lambda m: (layer, 1, 0, 0)),
                pl.BlockSpec((None, None, 1, D), lambda m: (layer, 1, 0, 0))]
    args = [dest_tiles, dest_tiles, y_slots, w_tok, hp, x1, ws_gate, ws_up, ws_down, mods, ln_g, ln_b]
    out_specs = [pl.BlockSpec((tm, D), row)]
    out_shape = [jax.ShapeDtypeStruct((NT, D), F32)]
    if has_next:
        in_specs += [_mod_spec(layer + 1, 0, tm), _mod_spec(layer + 1, 1, tm)]
        args += [mods, mods]
        out_specs.append(pl.BlockSpec((tm, D), row))
        out_shape.append(jax.ShapeDtypeStruct((NT, D), BF16))
    res = pl.pallas_call(
        functools.partial(_combine_kernel, has_next),
        grid=(ntile,),
        in_specs=in_specs,
        out_specs=out_specs,
        out_shape=out_shape,
        scratch_shapes=[pltpu.VMEM((2, TOP_K * tm * Y_S, 128), F32), pltpu.SemaphoreType.DMA((2,))],
        compiler_params=_cparams(("arbitrary",)),
        name="moe_combine",
    )(*args)
    return (res[0], res[1]) if has_next else (res[0], None)


def kernel(x_prompt, x_sample, cache_diff_k, cache_diff_v, cache_na_k, cache_na_v, state_ret, c, c_ctx, ada_w, ada_b, ln_g, ln_b, diff_w_in, diff_w_out, diff_lambda, diff_subln_g, na_w_in, na_w_out, na_rpb, ret_w_in, ret_w_out, ret_decay, moe_router, moe_router_bias, moe_w_gate, moe_w_up, moe_w_down, shared_w_gate, shared_w_up, shared_w_down):
    x = jnp.concatenate([x_prompt.reshape(NP, D), x_sample.reshape(NS, D)], axis=0)
    cond = jnp.concatenate([c_ctx[None], c, jnp.zeros((N_COND - 1 - DEC_BATCH, D), F32)], axis=0)
    mods = modulation_table(cond, ada_w, ada_b)
    ln_g4 = ln_g.reshape(DEPTH, 2, 1, D)
    ln_b4 = ln_b.reshape(DEPTH, 2, 1, D)
    cos, sin = _rope_tables()
    h = modulate(x, mods, 0)
    new_dk, new_dv, new_nk, new_nv, new_rs = [], [], [], [], []
    for i in range(DEPTH):
        j = i // N_MIXERS
        kind = i % N_MIXERS
        if kind == 0:
            w = diff_w_in[j].astype(BF16)
            wq, wk, wv = w[:, :D], w[:, D:2 * D], w[:, 2 * D:]
            qp = matmul(h, wq, 0, NP, BF16)
            kp = matmul(h, wk, 0, NP, F32)
            vp = matmul(h, wv, 0, NP, F32)
            qs = rope(matmul(h, wq, NP, NS, F32), cos, sin)
            ks = rope(matmul(h, wk, NP, NS, F32), cos, sin)
            vs = matmul(h, wv, NP, NS, BF16)
            lam_init = 0.8 - 0.6 * math.exp(-0.3 * i)
            op = diff_attention(qp, kp, vp, diff_lambda[j], diff_subln_g[j], lam_init, BATCH, SEQ, SEQ, SEQ)
            k_all = jnp.concatenate([cache_diff_k[:, j].reshape(DEC_BATCH, PAST, D).astype(BF16),
                                     ks.reshape(DEC_BATCH, DEC_SEQ, D)], axis=1).reshape(-1, D)
            v_all = jnp.concatenate([cache_diff_v[:, j].reshape(DEC_BATCH, PAST, D).astype(BF16),
                                     vs.reshape(DEC_BATCH, DEC_SEQ, D)], axis=1).reshape(-1, D)
            os_ = diff_attention(qs, k_all, v_all, diff_lambda[j], diff_subln_g[j], lam_init,
                                 DEC_BATCH, DEC_SEQ, PAST + DEC_SEQ, 256)
            new_dk.append(kp.reshape(BATCH, SEQ, DIFF_HEADS, 2 * DIFF_QK))
            new_dv.append(vp.reshape(BATCH, SEQ, DIFF_HEADS, DIFF_V))
            w_out = diff_w_out[j]
        elif kind == 1:
            w = na_w_in[j].astype(BF16)
            wq, wk, wv = w[:, :D], w[:, D:2 * D], w[:, 2 * D:]
            qp = matmul(h, wq, 0, NP, BF16)
            kp = matmul(h, wk, 0, NP, F32)
            vp = matmul(h, wv, 0, NP, F32)
            qs = matmul(h, wq, NP, NS, BF16)
            ks = matmul(h, wk, NP, NS, BF16)
            vs = matmul(h, wv, NP, NS, BF16)
            op = na_context_attention(qp, kp, vp)
            os_ = na_latent_attention(qs, ks, vs,
                                      cache_na_k[:, j].reshape(DEC_BATCH, PAST, D).astype(BF16),
                                      cache_na_v[:, j].reshape(DEC_BATCH, PAST, D).astype(BF16),
                                      _na_bias_tables(na_rpb[j]))
            new_nk.append(kp.reshape(BATCH, SEQ, NA_HEADS, NA_DH))
            new_nv.append(vp.reshape(BATCH, SEQ, NA_HEADS, NA_DH))
            w_out = na_w_out[j]
        else:
            w = ret_w_in[j].astype(BF16)
            qd = RET_HEADS * RET_QK
            vd = RET_HEADS * RET_V
            wq, wk, wv, wg = w[:, :qd], w[:, qd:2 * qd], w[:, 2 * qd:2 * qd + vd], w[:, 2 * qd + vd:]
            log_g = -jax.nn.softplus(-ret_decay[j].astype(F32))
            outs = []
            for row0, nrows, nb, l in ((0, NP, BATCH, SEQ), (NP, NS, DEC_BATCH, DEC_SEQ)):
                q_ = matmul(h, wq, row0, nrows, BF16)
                k_ = matmul(h, wk, row0, nrows, F32)
                v_ = matmul(h, wv, row0, nrows, BF16)
                g_ = matmul(h, wg, row0, nrows, F32)
                if row0 == 0:
                    o_, st = retention(q_, k_, v_, g_, log_g, nb, l, None, True)
                    new_rs.append(st)
                else:
                    o_ = retention(q_, k_, v_, g_, log_g, nb, l, state_ret[:, j], False)
                outs.append(o_)
            op, os_ = outs
            w_out = ret_w_out[j]
        o = jnp.concatenate([op, os_], axis=0)
        x1, hp, e_t, w_t, rank_t, cnt = outproj_ln(o, w_out.astype(BF16), x, mods, ln_g4, ln_b4,
                                                   moe_router[i].T, moe_router_bias[i].astype(F32), i)
        dest_tiles, block_e, first, nused, zero_start = _slot_plan(e_t, rank_t, cnt)
        xs = moe_dispatch(hp, dest_tiles, zero_start, nused)
        y_slots = moe_experts(xs, block_e, first, nused, moe_w_gate, moe_w_up, moe_w_down, i)
        x, h = moe_combine(y_slots, dest_tiles, w_t.T, hp, x1,
                           shared_w_gate[i].astype(BF16), shared_w_up[i].astype(BF16),
                           shared_w_down[i].astype(BF16), mods, ln_g4, ln_b4, i)
    return (x[:NP].reshape(BATCH, SEQ, D), x[NP:].reshape(DEC_BATCH, DEC_SEQ, D),
            jnp.stack(new_dk, axis=1), jnp.stack(new_dv, axis=1),
            jnp.stack(new_nk, axis=1), jnp.stack(new_nv, axis=1), jnp.stack(new_rs, axis=1))
```

```python
import functools
import math

import jax
import jax.numpy as jnp
import numpy as np
from jax import lax
from jax.experimental import pallas as pl
from jax.experimental.pallas import tpu as pltpu

F32 = jnp.float32
BF16 = jnp.bfloat16
U32 = jnp.uint32
I32 = jnp.int32

D = 2048
BATCH = 32
SEQ = 256
DEPTH = 4
DEC_BATCH = 8
DEC_SEQ = 1024
PAST = 512
NP = BATCH * SEQ
NS = DEC_BATCH * DEC_SEQ
NT = NP + NS
GRID_W = 64
N_MIXERS = 3
DIFF_HEADS = 8
DIFF_QK = 128
DIFF_V = 256
NA_HEADS = 16
NA_DH = 128
NA_KH = 8
NA_KH_MAX = 8
NA_KW = 16
RET_HEADS = 8
RET_QK = 256
RET_V = 512
RET_CHUNK = 128
N_EXPERTS = 64
N_GROUPS = 8
EXPERTS_PER_GROUP = 8
TOPK_GROUPS = 4
TOP_K = 6
D_EXPERT = 512
D_SHARED = 512
ROUTED_SCALE = 2.5
ROPE_BASE = 10000.0
LN_EPS = 1e-5
ALPHA = (2 * DEPTH) ** 0.25
N_COND = 16

VMEM_LIMIT = 56 * 1024 * 1024

MOE_TB = 256
MOE_NBLK = (NT * TOP_K) // MOE_TB + N_EXPERTS
MOE_CAP = MOE_NBLK * MOE_TB
RT_TM = 512
CMB_TM = 128
PK_S = 8
PK_L = D // 2 // PK_S
Y_S = D // 128
HI_MASK = np.uint32(0xFFFF0000)


def _cparams(sem):
    return pltpu.CompilerParams(dimension_semantics=sem, vmem_limit_bytes=VMEM_LIMIT)


def _cond_row(row0):
    return jnp.where(row0 < NP, 0, 1 + (row0 - NP) // DEC_SEQ)


def _mod_spec(layer, chunk, tm, moff=0):
    return pl.BlockSpec((None, None, None, 1, D),
                        lambda m, *_: (layer, _cond_row((m + moff) * tm), chunk, 0, 0))


def _sigmoid(x):
    return 1.0 / (1.0 + jnp.exp(-x))


def _silu(x):
    return x * _sigmoid(x)


def _mod_kernel(cond_ref, w_ref, b_ref, o_ref):
    c = _silu(cond_ref[...]).astype(BF16)
    o_ref[...] = jnp.dot(c, w_ref[...].astype(BF16), preferred_element_type=F32) + b_ref[...]


def modulation_table(cond, ada_w, ada_b):
    tn = 1024
    n6 = 6 * D
    out = pl.pallas_call(
        _mod_kernel,
        grid=(DEPTH, n6 // tn),
        in_specs=[pl.BlockSpec((N_COND, D), lambda l, n: (0, 0)),
                  pl.BlockSpec((None, D, tn), lambda l, n: (l, 0, n)),
                  pl.BlockSpec((None, 1, tn), lambda l, n: (l, 0, n))],
        out_specs=pl.BlockSpec((None, N_COND, tn), lambda l, n: (l, 0, n)),
        out_shape=jax.ShapeDtypeStruct((DEPTH, N_COND, n6), F32),
        compiler_params=_cparams(("arbitrary", "arbitrary")),
        name="modulation_table",
    )(cond, ada_w, ada_b.reshape(DEPTH, 1, n6))
    return out.reshape(DEPTH, N_COND, 6, 1, D)


def _modulate_kernel(x_ref, sh_ref, sc_ref, o_ref):
    o_ref[...] = (x_ref[...] * (1.0 + sc_ref[...]) + sh_ref[...]).astype(o_ref.dtype)


def modulate(x, mods, layer):
    tm = 512
    return pl.pallas_call(
        _modulate_kernel,
        grid=(NT // tm,),
        in_specs=[pl.BlockSpec((tm, D), lambda m: (m, 0)),
                  _mod_spec(layer, 0, tm), _mod_spec(layer, 1, tm)],
        out_specs=pl.BlockSpec((tm, D), lambda m: (m, 0)),
        out_shape=jax.ShapeDtypeStruct((NT, D), BF16),
        compiler_params=_cparams(("arbitrary",)),
        name="modulate",
    )(x, mods, mods)


def _mm_kernel(a_ref, w_ref, o_ref):
    o_ref[...] = jnp.dot(a_ref[...], w_ref[...], preferred_element_type=F32).astype(o_ref.dtype)


def matmul(a, w, row0, nrows, out_dtype):
    k = a.shape[1]
    n = w.shape[1]
    tm, tn = 1024, 512
    moff = row0 // tm
    return pl.pallas_call(
        _mm_kernel,
        grid=(nrows // tm, n // tn),
        in_specs=[pl.BlockSpec((tm, k), lambda m, j: (m + moff, 0)),
                  pl.BlockSpec((k, tn), lambda m, j: (0, j))],
        out_specs=pl.BlockSpec((tm, tn), lambda m, j: (m, j)),
        out_shape=jax.ShapeDtypeStruct((nrows, n), out_dtype),
        compiler_params=_cparams(("arbitrary", "arbitrary")),
        name="in_proj",
    )(a, w)


def _rope_tables():
    half = DIFF_QK // 2
    t = jnp.arange(DEC_SEQ)
    inv = ROPE_BASE ** (-jnp.arange(0, half, 2, dtype=F32) / half)

    def cs(pos):
        ang = pos.astype(F32)[:, None] * inv[None, :]
        return jnp.cos(ang), jnp.sin(ang)

    cr, sr = cs(t // GRID_W)
    cc, sc = cs(t % GRID_W)
    cos = jnp.concatenate([cr, cr, cc, cc], axis=-1)
    sin = jnp.concatenate([-sr, sr, -sc, sc], axis=-1)
    return cos, sin


def _rope_kernel(x_ref, cos_ref, sin_ref, o_ref):
    cos = cos_ref[...]
    sin = sin_ref[...]
    lane = lax.broadcasted_iota(jnp.int32, cos.shape, 1)
    first = (lane % (DIFF_QK // 2)) < (DIFF_QK // 4)
    for g in range(D // DIFF_QK):
        xg = x_ref[:, g * DIFF_QK:(g + 1) * DIFF_QK]
        sw = jnp.where(first, pltpu.roll(xg, DIFF_QK - DIFF_QK // 4, 1), pltpu.roll(xg, DIFF_QK // 4, 1))
        o_ref[:, g * DIFF_QK:(g + 1) * DIFF_QK] = (xg * cos + sw * sin).astype(o_ref.dtype)


def rope(x, cos, sin):
    tm = 512
    nt = DEC_SEQ // tm
    return pl.pallas_call(
        _rope_kernel,
        grid=(NS // tm,),
        in_specs=[pl.BlockSpec((tm, D), lambda m: (m, 0)),
                  pl.BlockSpec((tm, DIFF_QK), lambda m: (m % nt, 0)),
                  pl.BlockSpec((tm, DIFF_QK), lambda m: (m % nt, 0))],
        out_specs=pl.BlockSpec((tm, D), lambda m: (m, 0)),
        out_shape=jax.ShapeDtypeStruct((NS, D), BF16),
        compiler_params=_cparams(("arbitrary",)),
        name="rope",
    )(x, cos, sin)


def _qkt(q, k):
    return lax.dot_general(q, k, (((1,), (1,)), ((), ())), preferred_element_type=F32)


def _softmax(s):
    m = jnp.max(s, axis=-1, keepdims=True)
    p = jnp.exp(s - m)
    return p / jnp.sum(p, axis=-1, keepdims=True)


def _diff_kernel(lam_init, lamp_ref, g_ref, q_ref, k_ref, v_ref, o_ref):
    lp = lamp_ref[...]
    lam = (jnp.exp(jnp.sum(lp[0:1] * lp[1:2], axis=-1, keepdims=True))
           - jnp.exp(jnp.sum(lp[2:3] * lp[3:4], axis=-1, keepdims=True)) + lam_init)
    scale = DIFF_QK ** -0.5
    g = g_ref[...]
    for h in range(DIFF_HEADS):
        c0 = h * DIFF_V
        q = q_ref[:, c0:c0 + DIFF_V].astype(BF16)
        k = k_ref[:, c0:c0 + DIFF_V].astype(BF16)
        v = v_ref[:, c0:c0 + DIFF_V].astype(BF16)
        a1 = _softmax(_qkt(q[:, :DIFF_QK], k[:, :DIFF_QK]) * scale)
        a2 = _softmax(_qkt(q[:, DIFF_QK:], k[:, DIFF_QK:]) * scale)
        a = (a1 - lam * a2).astype(BF16)
        o = jnp.dot(a, v, preferred_element_type=F32)
        o = o * lax.rsqrt(jnp.mean(jnp.square(o), axis=-1, keepdims=True) + LN_EPS) * g
        o_ref[:, c0:c0 + DIFF_V] = (o * (1.0 - lam_init)).astype(o_ref.dtype)


def diff_attention(q, k, v, lam_p, subln_g, lam_init, nb, lq, lk, tq):
    nq = lq // tq
    return pl.pallas_call(
        functools.partial(_diff_kernel, lam_init),
        grid=(nb, nq),
        in_specs=[pl.BlockSpec((4, DIFF_QK), lambda b, i: (0, 0)),
                  pl.BlockSpec((1, DIFF_V), lambda b, i: (0, 0)),
                  pl.BlockSpec((tq, D), lambda b, i: (b * nq + i, 0)),
                  pl.BlockSpec((lk, D), lambda b, i: (b, 0)),
                  pl.BlockSpec((lk, D), lambda b, i: (b, 0))],
        out_specs=pl.BlockSpec((tq, D), lambda b, i: (b * nq + i, 0)),
        out_shape=jax.ShapeDtypeStruct((nb * lq, D), BF16),
        compiler_params=_cparams(("arbitrary", "arbitrary")),
        name="diff_attention",
    )(lam_p, subln_g.reshape(1, DIFF_V), q, k, v)


def _na_ctx_kernel(q_ref, k_ref, v_ref, o_ref):
    scale = NA_DH ** -0.5
    for h in range(NA_HEADS):
        c0 = h * NA_DH
        q = q_ref[:, c0:c0 + NA_DH].astype(BF16)
        k = k_ref[:, c0:c0 + NA_DH].astype(BF16)
        v = v_ref[:, c0:c0 + NA_DH].astype(BF16)
        p = _softmax(_qkt(q, k) * scale).astype(BF16)
        o_ref[:, c0:c0 + NA_DH] = jnp.dot(p, v, preferred_element_type=F32).astype(o_ref.dtype)


def na_context_attention(q, k, v):
    spec = pl.BlockSpec((SEQ, D), lambda b: (b, 0))
    return pl.pallas_call(
        _na_ctx_kernel,
        grid=(BATCH,),
        in_specs=[spec, spec, spec],
        out_specs=spec,
        out_shape=jax.ShapeDtypeStruct((NP, D), BF16),
        compiler_params=_cparams(("arbitrary",)),
        name="na_context_attention",
    )(q, k, v)


NA_ROWS = DEC_SEQ // GRID_W
NA_NLOC = NA_KH * GRID_W


def _na_window_row(r):
    return jnp.clip(r - NA_KH // 2, 0, NA_ROWS - NA_KH)


def _na_bias_tables(rpb):
    col = jnp.arange(GRID_W)
    col_start = jnp.clip(col - NA_KW // 2, 0, GRID_W - NA_KW)
    col_ok = (col[None, :] >= col_start[:, None]) & (col[None, :] < col_start[:, None] + NA_KW)
    dc_idx = jnp.clip(col[None, :] - col[:, None] + NA_KW - 1, 0, 2 * NA_KW - 2)
    hot = (dc_idx[:, :, None] == jnp.arange(2 * NA_KW - 1)[None, None, :]).astype(F32)
    full = jnp.einsum('hrd,qkd->hqrk', rpb.astype(F32), hot, precision=lax.Precision.HIGHEST)
    full = jnp.where(col_ok[None, :, None, :], full, -jnp.inf)
    full = full.reshape(NA_HEADS, GRID_W, (2 * NA_KH_MAX - 1) * GRID_W)
    tabs = [full[:, :, (NA_KH_MAX - 1 - o) * GRID_W:(NA_KH_MAX - 1 - o) * GRID_W + NA_NLOC]
            for o in range(NA_KH)]
    return jnp.stack(tabs)


def _na_lat_kernel(q_ref, k_ref, v_ref, kc_ref, vc_ref, bias_ref, o_ref):
    r = pl.program_id(1)
    start = pl.multiple_of(_na_window_row(r) * GRID_W, GRID_W)
    scale = NA_DH ** -0.5
    for h in range(NA_HEADS):
        c0 = h * NA_DH
        q = q_ref[:, c0:c0 + NA_DH]
        kw = k_ref[pl.ds(start, NA_NLOC), c0:c0 + NA_DH]
        vw = v_ref[pl.ds(start, NA_NLOC), c0:c0 + NA_DH]
        s_loc = _qkt(q, kw) * scale + bias_ref[h]
        s_ctx = _qkt(q, kc_ref[:, c0:c0 + NA_DH]) * scale
        m = jnp.maximum(jnp.max(s_loc, axis=-1, keepdims=True), jnp.max(s_ctx, axis=-1, keepdims=True))
        p_loc = jnp.exp(s_loc - m)
        p_ctx = jnp.exp(s_ctx - m)
        den = jnp.sum(p_loc, axis=-1, keepdims=True) + jnp.sum(p_ctx, axis=-1, keepdims=True)
        o = (jnp.dot((p_loc / den).astype(BF16), vw, preferred_element_type=F32)
             + jnp.dot((p_ctx / den).astype(BF16), vc_ref[:, c0:c0 + NA_DH], preferred_element_type=F32))
        o_ref[:, c0:c0 + NA_DH] = o.astype(o_ref.dtype)


def na_latent_attention(q, k, v, k_ctx, v_ctx, bias_tabs):
    return pl.pallas_call(
        _na_lat_kernel,
        grid=(DEC_BATCH, NA_ROWS),
        in_specs=[pl.BlockSpec((GRID_W, D), lambda b, r: (b * NA_ROWS + r, 0)),
                  pl.BlockSpec((DEC_SEQ, D), lambda b, r: (b, 0)),
                  pl.BlockSpec((DEC_SEQ, D), lambda b, r: (b, 0)),
                  pl.BlockSpec((None, PAST, D), lambda b, r: (b, 0, 0)),
                  pl.BlockSpec((None, PAST, D), lambda b, r: (b, 0, 0)),
                  pl.BlockSpec((None, NA_HEADS, GRID_W, NA_NLOC),
                               lambda b, r: (r - _na_window_row(r), 0, 0, 0))],
        out_specs=pl.BlockSpec((GRID_W, D), lambda b, r: (b * NA_ROWS + r, 0)),
        out_shape=jax.ShapeDtypeStruct((NS, D), BF16),
        compiler_params=_cparams(("arbitrary", "arbitrary")),
        name="na_latent_attention",
    )(q, k, v, k_ctx, v_ctx, bias_tabs)


def _ret_kernel(nc, has_state, emit_state, logg_ref, q_ref, k_ref, v_ref, g_ref, *rest):
    rest = list(rest)
    s0_ref = rest.pop(0) if has_state else None
    o_ref = rest.pop(0)
    st_ref = rest.pop(0) if emit_state else None
    oacc, sacc = rest
    h = pl.program_id(1)
    lgf = logg_ref[0, h]
    lgb = logg_ref[1, h]
    c = RET_CHUNK
    pos = lax.broadcasted_iota(jnp.int32, (c, 1), 0).astype(F32)
    diff = (lax.broadcasted_iota(jnp.int32, (c, c), 0) - lax.broadcasted_iota(jnp.int32, (c, c), 1)).astype(F32)
    dmask_f = jnp.where(diff >= 0, jnp.exp(jnp.maximum(diff, 0.0) * lgf), 0.0)
    dmask_b = jnp.where(diff <= 0, jnp.exp(jnp.maximum(-diff, 0.0) * lgb), 0.0)
    qdec_f = jnp.exp((pos + 1.0) * lgf)
    kdec_f = jnp.exp((c - 1.0 - pos) * lgf)
    qdec_b = jnp.exp((c - pos) * lgb)
    kdec_b = jnp.exp(pos * lgb)
    cd_f = jnp.exp(jnp.full((1, 1), c, F32) * lgf)
    cd_b = jnp.exp(jnp.full((1, 1), c, F32) * lgb)

    def chunk(i):
        sl = slice(i * c, (i + 1) * c)
        return q_ref[sl, :].astype(BF16), k_ref[sl, :] * (RET_QK ** -0.5), v_ref[sl, :].astype(BF16)

    def state_update(kc, kdec, vb, cd):
        kd = (kc * kdec).T.astype(BF16)
        sacc[...] = sacc[...] * cd + jnp.dot(kd, vb, preferred_element_type=F32)

    if has_state:
        sacc[...] = s0_ref[0]
    else:
        sacc[...] = jnp.zeros_like(sacc)
    for i in range(nc):
        qb, kc, vb = chunk(i)
        qk = _qkt(qb, kc.astype(BF16))
        o = (jnp.dot((qk * dmask_f).astype(BF16), vb, preferred_element_type=F32)
             + jnp.dot((qk * dmask_b).astype(BF16), vb, preferred_element_type=F32)
             + jnp.dot(qb, sacc[...].astype(BF16), preferred_element_type=F32) * qdec_f)
        oacc[i * c:(i + 1) * c, :] = o
        state_update(kc, kdec_f, vb, cd_f)
    if emit_state:
        st_ref[0] = sacc[...]

    if has_state:
        sacc[...] = s0_ref[1]
    else:
        sacc[...] = jnp.zeros_like(sacc)
    for i in reversed(range(nc)):
        qb, kc, vb = chunk(i)
        oacc[i * c:(i + 1) * c, :] += jnp.dot(qb, sacc[...].astype(BF16), preferred_element_type=F32) * qdec_b
        state_update(kc, kdec_b, vb, cd_b)
    if emit_state:
        st_ref[1] = sacc[...]

    o = oacc[...]
    mu = jnp.mean(o, axis=-1, keepdims=True)
    var = jnp.mean(jnp.square(o - mu), axis=-1, keepdims=True)
    o = (o - mu) * lax.rsqrt(var + LN_EPS)
    o_ref[...] = (o * _silu(g_ref[...])).astype(o_ref.dtype)


def retention(q, k, v, g, log_g, nb, l, state=None, emit_state=False):
    nc = l // RET_CHUNK
    has_state = state is not None
    in_specs = [pl.BlockSpec(memory_space=pltpu.SMEM),
                pl.BlockSpec((l, RET_QK), lambda b, h: (b, h)),
                pl.BlockSpec((l, RET_QK), lambda b, h: (b, h)),
                pl.BlockSpec((l, RET_V), lambda b, h: (b, h)),
                pl.BlockSpec((l, RET_V), lambda b, h: (b, h))]
    args = [log_g, q, k, v, g]
    st_spec = pl.BlockSpec((None, 2, None, RET_QK, RET_V), lambda b, h: (b, 0, h, 0, 0))
    if has_state:
        in_specs.append(st_spec)
        args.append(state)
    out_specs = [pl.BlockSpec((l, RET_V), lambda b, h: (b, h))]
    out_shape = [jax.ShapeDtypeStruct((nb * l, RET_HEADS * RET_V), BF16)]
    if emit_state:
        out_specs.append(st_spec)
        out_shape.append(jax.ShapeDtypeStruct((nb, 2, RET_HEADS, RET_QK, RET_V), F32))
    res = pl.pallas_call(
        functools.partial(_ret_kernel, nc, has_state, emit_state),
        grid=(nb, RET_HEADS),
        in_specs=in_specs,
        out_specs=out_specs,
        out_shape=out_shape,
        scratch_shapes=[pltpu.VMEM((l, RET_V), F32), pltpu.VMEM((RET_QK, RET_V), F32)],
        compiler_params=_cparams(("arbitrary", "arbitrary")),
        name="retention",
    )(*args)
    return res if emit_state else res[0]


def _layer_norm(z, g, b):
    mu = jnp.mean(z, axis=-1, keepdims=True)
    var = jnp.mean(jnp.square(z - mu), axis=-1, keepdims=True)
    return (z - mu) * lax.rsqrt(var + LN_EPS) * g + b


def _pack_rows(hn, hp_ref):
    tm = hn.shape[0]
    half = D // 2
    lo = lax.bitcast_convert_type(hn[:, :half].astype(BF16).astype(F32), U32)
    hi = lax.bitcast_convert_type(hn[:, half:].astype(BF16).astype(F32), U32)
    w = (lo >> 16) | (hi & HI_MASK)
    for s in range(PK_S):
        hp_ref[pl.ds(s, tm, stride=PK_S), :] = w[:, s * PK_L:(s + 1) * PK_L]


def _unpack_rows(hp_ref, tm):
    lo, hi = [], []
    for s in range(PK_S):
        w = hp_ref[pl.ds(s, tm, stride=PK_S), :]
        lo.append(lax.bitcast_convert_type(w << 16, F32).astype(BF16))
        hi.append(lax.bitcast_convert_type(w & HI_MASK, F32).astype(BF16))
    return jnp.concatenate(lo + hi, axis=1)


def _route_tile(hn, wrt_ref, rb_ref, e_ref, w_ref, rank_ref, cnt_ref):
    tm = hn.shape[0]
    neg = -jnp.inf
    logits = lax.dot_general(wrt_ref[...], hn, (((1,), (1,)), ((), ())),
                             preferred_element_type=F32, precision=lax.Precision.HIGHEST)
    s = _sigmoid(logits)
    biased = s + rb_ref[...]
    io8 = lax.broadcasted_iota(I32, (EXPERTS_PER_GROUP, tm), 0).astype(F32)
    slabs, gscore = [], []
    for g in range(N_GROUPS):
        slab = biased[g * EXPERTS_PER_GROUP:(g + 1) * EXPERTS_PER_GROUP, :]
        m1 = jnp.max(slab, axis=0, keepdims=True)
        i1 = jnp.min(jnp.where(slab == m1, io8, float(EXPERTS_PER_GROUP)), axis=0, keepdims=True)
        m2 = jnp.max(jnp.where(io8 == i1, neg, slab), axis=0, keepdims=True)
        slabs.append(slab)
        gscore.append(m1 + m2)
    masked = []
    for g in range(N_GROUPS):
        ahead = jnp.zeros_like(gscore[g])
        for g2 in range(N_GROUPS):
            if g2 == g:
                continue
            better = (gscore[g2] > gscore[g]) | ((gscore[g2] == gscore[g]) if g2 < g else False)
            ahead = ahead + jnp.where(better, 1.0, 0.0)
        masked.append(jnp.where(ahead < float(TOPK_GROUPS), slabs[g], neg))
    v = jnp.concatenate(masked, axis=0)
    io = lax.broadcasted_iota(I32, (N_EXPERTS, tm), 0).astype(F32)
    idxs, ws, hots = [], [], []
    for _ in range(TOP_K):
        m = jnp.max(v, axis=0, keepdims=True)
        idx = jnp.min(jnp.where(v == m, io, float(N_EXPERTS)), axis=0, keepdims=True)
        hot = io == idx
        ws.append(jnp.sum(jnp.where(hot, s, 0.0), axis=0, keepdims=True))
        v = jnp.where(hot, neg, v)
        idxs.append(idx)
        hots.append(hot)
    wsum = ws[0]
    for k in range(1, TOP_K):
        wsum = wsum + ws[k]
    chosen = jnp.where(hots[0], 1.0, 0.0)
    for k in range(1, TOP_K):
        chosen = chosen + jnp.where(hots[k], 1.0, 0.0)
    upper = jnp.where(lax.broadcasted_iota(I32, (tm, tm), 0) < lax.broadcasted_iota(I32, (tm, tm), 1), 1.0, 0.0)
    before = jnp.dot(chosen.astype(BF16), upper.astype(BF16), preferred_element_type=F32)
    zero_row = jnp.zeros((1, tm), F32)
    for k in range(8):
        if k < TOP_K:
            e_ref[k:k + 1, :] = idxs[k].astype(I32)
            w_ref[k:k + 1, :] = ws[k] / wsum * ROUTED_SCALE
            rank_ref[k:k + 1, :] = jnp.sum(jnp.where(hots[k], before, 0.0), axis=0, keepdims=True).astype(I32)
        else:
            e_ref[k:k + 1, :] = zero_row.astype(I32)
            w_ref[k:k + 1, :] = zero_row
            rank_ref[k:k + 1, :] = zero_row.astype(I32)
    cnt = jnp.sum(chosen, axis=1, keepdims=True)
    cnt_ref[...] = jnp.broadcast_to(cnt, (N_EXPERTS, 128)).astype(I32)


def _outproj_kernel(nk, o_ref, w_ref, x_ref, gate_ref, lng_ref, lnb_ref, sh_ref, sc_ref, wrt_ref, rb_ref,
                    xo_ref, hp_ref, e_ref, wt_ref, rank_ref, cnt_ref, acc_ref):
    kk = pl.program_id(1)
    part = jnp.dot(o_ref[...], w_ref[...], preferred_element_type=F32)

    @pl.when(kk == 0)
    def _():
        acc_ref[...] = part

    @pl.when(kk > 0)
    def _():
        acc_ref[...] += part

    @pl.when(kk == nk - 1)
    def _():
        xn = _layer_norm(ALPHA * x_ref[...] + gate_ref[...] * acc_ref[...], lng_ref[...], lnb_ref[...])
        xo_ref[...] = xn
        hn = xn * (1.0 + sc_ref[...]) + sh_ref[...]
        _pack_rows(hn, hp_ref)
        _route_tile(hn, wrt_ref, rb_ref, e_ref, wt_ref, rank_ref, cnt_ref)


def outproj_ln(o, w_out, x, mods, ln_g, ln_b, w_router_t, b_router, layer):
    k = o.shape[1]
    tk = 2048
    nk = k // tk
    tm = RT_TM if nk == 1 else RT_TM // 2
    ntile = NT // tm
    row = lambda m, kk: (m, 0)
    col = lambda m, kk: (0, m)
    return pl.pallas_call(
        functools.partial(_outproj_kernel, nk),
        grid=(ntile, nk),
        in_specs=[pl.BlockSpec((tm, tk), lambda m, kk: (m, kk)),
                  pl.BlockSpec((tk, D), lambda m, kk: (kk, 0),
                               pipeline_mode=pl.Buffered(1) if nk == 1 else None),
                  pl.BlockSpec((tm, D), row),
                  _mod_spec(layer, 2, tm),
                  pl.BlockSpec((None, None, 1, D), lambda m, kk: (layer, 0, 0, 0)),
                  pl.BlockSpec((None, None, 1, D), lambda m, kk: (layer, 0, 0, 0)),
                  _mod_spec(layer, 3, tm), _mod_spec(layer, 4, tm),
                  pl.BlockSpec((N_EXPERTS, D), lambda m, kk: (0, 0)),
                  pl.BlockSpec((N_EXPERTS, 1), lambda m, kk: (0, 0))],
        out_specs=[pl.BlockSpec((tm, D), row),
                   pl.BlockSpec((tm * PK_S, PK_L), row),
                   pl.BlockSpec((8, tm), col), pl.BlockSpec((8, tm), col), pl.BlockSpec((8, tm), col),
                   pl.BlockSpec((None, N_EXPERTS, 128), lambda m, kk: (m, 0, 0))],
        out_shape=[jax.ShapeDtypeStruct((NT, D), F32),
                   jax.ShapeDtypeStruct((NT * PK_S, PK_L), U32),
                   jax.ShapeDtypeStruct((8, NT), I32), jax.ShapeDtypeStruct((8, NT), F32),
                   jax.ShapeDtypeStruct((8, NT), I32),
                   jax.ShapeDtypeStruct((ntile, N_EXPERTS, 128), I32)],
        scratch_shapes=[pltpu.VMEM((tm, D), F32)],
        compiler_params=_cparams(("arbitrary", "arbitrary")),
        name="outproj_ln_route",
    )(o, w_out, x, mods, ln_g, ln_b, mods, mods, w_router_t, b_router.reshape(N_EXPERTS, 1))


def _slot_plan(e_t, rank_t, cnt):
    cnt = cnt[:, :, 0]
    tile_base = jnp.cumsum(cnt, axis=0) - cnt
    counts = jnp.sum(cnt, axis=0)
    padded = (counts + MOE_TB - 1) // MOE_TB * MOE_TB
    pad_end = jnp.cumsum(padded)
    pad_start = pad_end - padded
    offs = jnp.repeat(pad_start[None, :] + tile_base, NT // cnt.shape[0], axis=0)
    hot = e_t[:TOP_K, :, None] == jnp.arange(N_EXPERTS, dtype=I32)[None, None, :]
    dest = jnp.sum(jnp.where(hot, offs[None], 0), axis=-1) + rank_t[:TOP_K]
    ntile = NT // CMB_TM
    dest_tiles = dest.reshape(TOP_K, ntile, CMB_TM).transpose(1, 0, 2).reshape(ntile, 1, TOP_K * CMB_TM)
    nused = pad_end[-1] // MOE_TB
    blk = jnp.minimum(jnp.arange(MOE_NBLK, dtype=I32), nused - 1) * MOE_TB
    block_e = jnp.minimum(jnp.sum((pad_end[None, :] <= blk[:, None]).astype(I32), axis=1), N_EXPERTS - 1)
    first = jnp.concatenate([jnp.ones((1,), I32), (block_e[1:] != block_e[:-1]).astype(I32)])
    zero_start = pad_start + counts
    return (dest_tiles.astype(I32), block_e.astype(I32), first, nused.reshape(1).astype(I32),
            zero_start.astype(I32))


_DISPATCH_ROWS = TOP_K * CMB_TM


def _dispatch_kernel(zs_ref, nused_ref, dest_ref, hp_ref, xs_hbm, stage, zbuf, zsem, sem):
    i = pl.program_id(0)
    n = pl.num_programs(0)
    slot = i % 2

    @pl.when(i == 0)
    def _():
        zbuf[...] = jnp.zeros_like(zbuf)

        def zcopy(slot0, nslot):
            start = pl.multiple_of(slot0 * PK_S, PK_S)
            return pltpu.make_async_copy(zbuf.at[pl.ds(0, nslot * PK_S)],
                                         xs_hbm.at[pl.ds(start, nslot * PK_S)], zsem)

        def pad_pieces(e, wait):
            slot0 = zs_ref[e]
            npad = (MOE_TB - slot0 % MOE_TB) % MOE_TB
            piece = MOE_TB // 2
            while piece >= 1:
                @pl.when((npad & piece) != 0)
                def _(slot0=slot0, piece=piece):
                    cp = zcopy(slot0, piece)
                    cp.wait() if wait else cp.start()
                slot0 = slot0 + (npad & piece)
                piece //= 2

        def zstart(e, carry):
            pad_pieces(e, False)
            return carry

        def zwait(e, carry):
            pad_pieces(e, True)
            return carry

        def tstart(b, carry):
            zcopy(b * MOE_TB, MOE_TB).start()
            return carry

        def twait(b, carry):
            zcopy(b * MOE_TB, MOE_TB).wait()
            return carry

        lax.fori_loop(0, N_EXPERTS, zstart, 0)
        lax.fori_loop(nused_ref[0], MOE_NBLK + 1, tstart, 0)
        lax.fori_loop(0, N_EXPERTS, zwait, 0)
        lax.fori_loop(nused_ref[0], MOE_NBLK + 1, twait, 0)

    stage[slot] = hp_ref[...]

    def issue(r, carry):
        for k in range(TOP_K):
            src = pl.multiple_of(r * PK_S, PK_S)
            dst = pl.multiple_of(dest_ref[0, 0, k * CMB_TM + r] * PK_S, PK_S)
            pltpu.make_async_copy(stage.at[slot, pl.ds(src, PK_S)], xs_hbm.at[pl.ds(dst, PK_S)],
                                  sem.at[slot]).start(priority=k % 2)
        return carry

    lax.fori_loop(0, CMB_TM, issue, 0, unroll=2)

    def drain(s):
        for k in range(TOP_K):
            pltpu.make_async_copy(stage.at[s], xs_hbm.at[pl.ds(0, CMB_TM * PK_S)], sem.at[s]).wait()

    @pl.when(i > 0)
    def _():
        drain(1 - slot)

    @pl.when(i == n - 1)
    def _():
        drain(slot)


def moe_dispatch(hp, dest_tiles, zero_start, nused):
    ntile = NT // CMB_TM
    grid_spec = pltpu.PrefetchScalarGridSpec(
        num_scalar_prefetch=2,
        grid=(ntile,),
        in_specs=[pl.BlockSpec((1, 1, _DISPATCH_ROWS), lambda i, zs, nu: (i, 0, 0), memory_space=pltpu.SMEM),
                  pl.BlockSpec((CMB_TM * PK_S, PK_L), lambda i, zs, nu: (i, 0))],
        out_specs=pl.BlockSpec(memory_space=pl.ANY),
        scratch_shapes=[pltpu.VMEM((2, CMB_TM * PK_S, PK_L), U32),
                        pltpu.VMEM((MOE_TB * PK_S, PK_L), U32),
                        pltpu.SemaphoreType.DMA(()), pltpu.SemaphoreType.DMA((2,))],
    )
    return pl.pallas_call(
        _dispatch_kernel,
        grid_spec=grid_spec,
        out_shape=jax.ShapeDtypeStruct(((MOE_CAP + MOE_TB) * PK_S, PK_L), U32),
        compiler_params=_cparams(("arbitrary",)),
        name="moe_dispatch",
    )(zero_start, nused, dest_tiles, hp)


def _moe_kernel(be_ref, first_ref, nused_ref, xs_ref, wg_ref, wu_ref, wd_ref, y_ref, wgb, wub, wdb):
    i = pl.program_id(0)

    @pl.when(first_ref[i] == 1)
    def _():
        wgb[...] = wg_ref[...].astype(BF16)
        wub[...] = wu_ref[...].astype(BF16)
        wdb[...] = wd_ref[...].astype(BF16)

    @pl.when(i < nused_ref[0])
    def _():
        x = _unpack_rows(xs_ref, MOE_TB)
        a = (_silu(jnp.dot(x, wgb[...], preferred_element_type=F32))
             * jnp.dot(x, wub[...], preferred_element_type=F32))
        y = jnp.dot(a.astype(BF16), wdb[...], preferred_element_type=F32)
        for j in range(Y_S):
            y_ref[pl.ds(j, MOE_TB, stride=Y_S), :] = y[:, j * 128:(j + 1) * 128]

    @pl.when(i >= nused_ref[0])
    def _():
        y_ref[...] = jnp.zeros_like(y_ref)


def moe_experts(xs, block_e, first, nused, w_gate, w_up, w_down, layer):
    wspec = lambda shape: pl.BlockSpec((None, None) + shape, lambda i, be, fi, nu: (layer, be[i], 0, 0))
    blk = lambda i, be, fi, nu: (jnp.minimum(i, nu[0] - 1), 0)
    grid_spec = pltpu.PrefetchScalarGridSpec(
        num_scalar_prefetch=3,
        grid=(MOE_NBLK,),
        in_specs=[pl.BlockSpec((MOE_TB * PK_S, PK_L), blk),
                  wspec((D, D_EXPERT)), wspec((D, D_EXPERT)), wspec((D_EXPERT, D))],
        out_specs=pl.BlockSpec((MOE_TB * Y_S, 128), lambda i, be, fi, nu: (i, 0)),
        scratch_shapes=[pltpu.VMEM((D, D_EXPERT), BF16), pltpu.VMEM((D, D_EXPERT), BF16),
                        pltpu.VMEM((D_EXPERT, D), BF16)],
    )
    return pl.pallas_call(
        _moe_kernel,
        grid_spec=grid_spec,
        out_shape=jax.ShapeDtypeStruct((MOE_CAP * Y_S, 128), F32),
        compiler_params=_cparams(("arbitrary",)),
        name="moe_experts",
    )(block_e, first, nused, xs, w_gate, w_up, w_down)


def _combine_kernel(has_next, dest_ref, destn_ref, y_hbm, w_ref, hp_ref, x_ref, wsg_ref, wsu_ref, wsd_ref,
                    gate_ref, lng_ref, lnb_ref, *rest):
    if has_next:
        sh_ref, sc_ref, xo_ref, ho_ref, buf, sem = rest
    else:
        xo_ref, buf, sem = rest
    i = pl.program_id(0)
    n = pl.num_programs(0)
    slot = i % 2
    nrow = TOP_K * CMB_TM

    def gather(dref, s):
        def body(r, carry):
            for k in range(TOP_K):
                src = pl.multiple_of(dref[0, 0, k * CMB_TM + r] * Y_S, Y_S)
                dst = pl.multiple_of((k * CMB_TM + r) * Y_S, Y_S)
                pltpu.make_async_copy(y_hbm.at[pl.ds(src, Y_S)], buf.at[s, pl.ds(dst, Y_S)],
                                      sem.at[s]).start(priority=k % 2)
            return carry
        lax.fori_loop(0, CMB_TM, body, 0, unroll=2)

    @pl.when(i == 0)
    def _():
        gather(dest_ref, 0)

    @pl.when(i + 1 < n)
    def _():
        gather(destn_ref, 1 - slot)

    hb = _unpack_rows(hp_ref, CMB_TM)
    a = (_silu(jnp.dot(hb, wsg_ref[...], preferred_element_type=F32))
         * jnp.dot(hb, wsu_ref[...], preferred_element_type=F32))
    shared = jnp.dot(a.astype(BF16), wsd_ref[...], preferred_element_type=F32)

    pltpu.make_async_copy(y_hbm.at[pl.ds(0, nrow * Y_S)], buf.at[slot], sem.at[slot]).wait()
    w = w_ref[...]
    pieces = []
    for j in range(Y_S):
        acc = buf[slot, pl.ds(j, CMB_TM, stride=Y_S), :] * w[:, 0:1]
        for k in range(1, TOP_K):
            acc = acc + buf[slot, pl.ds(k * CMB_TM * Y_S + j, CMB_TM, stride=Y_S), :] * w[:, k:k + 1]
        pieces.append(acc)
    routed = jnp.concatenate(pieces, axis=1)
    xn = _layer_norm(ALPHA * x_ref[...] + gate_ref[...] * (routed + shared), lng_ref[...], lnb_ref[...])
    xo_ref[...] = xn
    if has_next:
        ho_ref[...] = (xn * (1.0 + sc_ref[...]) + sh_ref[...]).astype(ho_ref.dtype)


def moe_combine(y_slots, dest_tiles, w_tok, hp, x1, ws_gate, ws_up, ws_down, mods, ln_g, ln_b, layer):
    has_next = layer + 1 < DEPTH
    tm = CMB_TM
    ntile = NT // tm
    row = lambda m: (m, 0)
    const = lambda m: (0, 0)
    in_specs = [pl.BlockSpec((1, 1, TOP_K * tm), lambda m: (m, 0, 0), memory_space=pltpu.SMEM),
                pl.BlockSpec((1, 1, TOP_K * tm), lambda m: (jnp.minimum(m + 1, ntile - 1), 0, 0),
                             memory_space=pltpu.SMEM),
                pl.BlockSpec(memory_space=pl.ANY),
                pl.BlockSpec((tm, 8), row),
                pl.BlockSpec((tm * PK_S, PK_L), row), pl.BlockSpec((tm, D), row),
                pl.BlockSpec((D, D_SHARED), const), pl.BlockSpec((D, D_SHARED), const),
                pl.BlockSpec((D_SHARED, D), const),
                _mod_spec(layer, 5, tm),
                pl.BlockSpec((None, None, 1, D), lambda m: (layer, 1, 0, 0)),
                pl.BlockSpec((None, None, 1, D), lambda m: (layer, 1, 0, 0))]
    args = [dest_tiles, dest_tiles, y_slots, w_tok, hp, x1, ws_gate, ws_up, ws_down, mods, ln_g, ln_b]
    out_specs = [pl.BlockSpec((tm, D), row)]
    out_shape = [jax.ShapeDtypeStruct((NT, D), F32)]
    if has_next:
        in_specs += [_mod_spec(layer + 1, 0, tm), _mod_spec(layer + 1, 1, tm)]
        args += [mods, mods]
        out_specs.append(pl.BlockSpec((tm, D), row))
        out_shape.append(jax.ShapeDtypeStruct((NT, D), BF16))
    res = pl.pallas_call(
        functools.partial(_combine_kernel, has_next),
        grid=(ntile,),
        in_specs=in_specs,
        out_specs=out_specs,
        out_shape=out_shape,
        scratch_shapes=[pltpu.VMEM((2, TOP_K * tm * Y_S, 128), F32), pltpu.SemaphoreType.DMA((2,))],
        compiler_params=_cparams(("arbitrary",)),
        name="moe_combine",
    )(*args)
    return (res[0], res[1]) if has_next else (res[0], None)


def kernel(x_prompt, x_sample, cache_diff_k, cache_diff_v, cache_na_k, cache_na_v, state_ret, c, c_ctx, ada_w, ada_b, ln_g, ln_b, diff_w_in, diff_w_out, diff_lambda, diff_subln_g, na_w_in, na_w_out, na_rpb, ret_w_in, ret_w_out, ret_decay, moe_router, moe_router_bias, moe_w_gate, moe_w_up, moe_w_down, shared_w_gate, shared_w_up, shared_w_down):
    x = jnp.concatenate([x_prompt.reshape(NP, D), x_sample.reshape(NS, D)], axis=0)
    cond = jnp.concatenate([c_ctx[None], c, jnp.zeros((N_COND - 1 - DEC_BATCH, D), F32)], axis=0)
    mods = modulation_table(cond, ada_w, ada_b)
    ln_g4 = ln_g.reshape(DEPTH, 2, 1, D)
    ln_b4 = ln_b.reshape(DEPTH, 2, 1, D)
    cos, sin = _rope_tables()
    h = modulate(x, mods, 0)
    new_dk, new_dv, new_nk, new_nv, new_rs = [], [], [], [], []
    for i in range(DEPTH):
        j = i // N_MIXERS
        kind = i % N_MIXERS
        if kind == 0:
            w = diff_w_in[j].astype(BF16)
            wq, wk, wv = w[:, :D], w[:, D:2 * D], w[:, 2 * D:]
            qp = matmul(h, wq, 0, NP, BF16)
            kp = matmul(h, wk, 0, NP, F32)
            vp = matmul(h, wv, 0, NP, F32)
            qs = rope(matmul(h, wq, NP, NS, F32), cos, sin)
            ks = rope(matmul(h, wk, NP, NS, F32), cos, sin)
            vs = matmul(h, wv, NP, NS, BF16)
            lam_init = 0.8 - 0.6 * math.exp(-0.3 * i)
            op = diff_attention(qp, kp, vp, diff_lambda[j], diff_subln_g[j], lam_init, BATCH, SEQ, SEQ, SEQ)
            k_all = jnp.concatenate([cache_diff_k[:, j].reshape(DEC_BATCH, PAST, D).astype(BF16),
                                     ks.reshape(DEC_BATCH, DEC_SEQ, D)], axis=1).reshape(-1, D)
            v_all = jnp.concatenate([cache_diff_v[:, j].reshape(DEC_BATCH, PAST, D).astype(BF16),
                                     vs.reshape(DEC_BATCH, DEC_SEQ, D)], axis=1).reshape(-1, D)
            os_ = diff_attention(qs, k_all, v_all, diff_lambda[j], diff_subln_g[j], lam_init,
                                 DEC_BATCH, DEC_SEQ, PAST + DEC_SEQ, 256)
            new_dk.append(kp.reshape(BATCH, SEQ, DIFF_HEADS, 2 * DIFF_QK))
            new_dv.append(vp.reshape(BATCH, SEQ, DIFF_HEADS, DIFF_V))
            w_out = diff_w_out[j]
        elif kind == 1:
            w = na_w_in[j].astype(BF16)
            wq, wk, wv = w[:, :D], w[:, D:2 * D], w[:, 2 * D:]
            qp = matmul(h, wq, 0, NP, BF16)
            kp = matmul(h, wk, 0, NP, F32)
            vp = matmul(h, wv, 0, NP, F32)
            qs = matmul(h, wq, NP, NS, BF16)
            ks = matmul(h, wk, NP, NS, BF16)
            vs = matmul(h, wv, NP, NS, BF16)
            op = na_context_attention(qp, kp, vp)
            os_ = na_latent_attention(qs, ks, vs,
                                      cache_na_k[:, j].reshape(DEC_BATCH, PAST, D).astype(BF16),
                                      cache_na_v[:, j].reshape(DEC_BATCH, PAST, D).astype(BF16),
                                      _na_bias_tables(na_rpb[j]))
            new_nk.append(kp.reshape(BATCH, SEQ, NA_HEADS, NA_DH))
            new_nv.append(vp.reshape(BATCH, SEQ, NA_HEADS, NA_DH))
            w_out = na_w_out[j]
        else:
            w = ret_w_in[j].astype(BF16)
            qd = RET_HEADS * RET_QK
            vd = RET_HEADS * RET_V
            wq, wk, wv, wg = w[:, :qd], w[:, qd:2 * qd], w[:, 2 * qd:2 * qd + vd], w[:, 2 * qd + vd:]
            log_g = -jax.nn.softplus(-ret_decay[j].astype(F32))
            outs = []
            for row0, nrows, nb, l in ((0, NP, BATCH, SEQ), (NP, NS, DEC_BATCH, DEC_SEQ)):
                q_ = matmul(h, wq, row0, nrows, BF16)
                k_ = matmul(h, wk, row0, nrows, F32)
                v_ = matmul(h, wv, row0, nrows, BF16)
                g_ = matmul(h, wg, row0, nrows, F32)
                if row0 == 0:
                    o_, st = retention(q_, k_, v_, g_, log_g, nb, l, None, True)
                    new_rs.append(st)
                else:
                    o_ = retention(q_, k_, v_, g_, log_g, nb, l, state_ret[:, j], False)
                outs.append(o_)
            op, os_ = outs
            w_out = ret_w_out[j]
        o = jnp.concatenate([op, os_], axis=0)
        x1, hp, e_t, w_t, rank_t, cnt = outproj_ln(o, w_out.astype(BF16), x, mods, ln_g4, ln_b4,
                                                   moe_router[i].T, moe_router_bias[i].astype(F32), i)
        dest_tiles, block_e, first, nused, zero_start = _slot_plan(e_t, rank_t, cnt)
        xs = moe_dispatch(hp, dest_tiles, zero_start, nused)
        y_slots = moe_experts(xs, block_e, first, nused, moe_w_gate, moe_w_up, moe_w_down, i)
        x, h = moe_combine(y_slots, dest_tiles, w_t.T, hp, x1,
                           shared_w_gate[i].astype(BF16), shared_w_up[i].astype(BF16),
                           shared_w_down[i].astype(BF16), mods, ln_g4, ln_b4, i)
    return (x[:NP].reshape(BATCH, SEQ, D), x[NP:].reshape(DEC_BATCH, DEC_SEQ, D),
            jnp.stack(new_dk, axis=1), jnp.stack(new_dv, axis=1),
            jnp.stack(new_nk, axis=1), jnp.stack(new_nv, axis=1), jnp.stack(new_rs, axis=1))
```

```python
import functools
import math

import jax
import jax.numpy as jnp
import numpy as np
from jax import lax
from jax.experimental import pallas as pl
from jax.experimental.pallas import tpu as pltpu

F32 = jnp.float32
BF16 = jnp.bfloat16
U32 = jnp.uint32
I32 = jnp.int32

D = 2048
BATCH = 32
SEQ = 256
DEPTH = 4
DEC_BATCH = 8
DEC_SEQ = 1024
PAST = 512
NP = BATCH * SEQ
NS = DEC_BATCH * DEC_SEQ
NT = NP + NS
GRID_W = 64
N_MIXERS = 3
DIFF_HEADS = 8
DIFF_QK = 128
DIFF_V = 256
NA_HEADS = 16
NA_DH = 128
NA_KH = 8
NA_KH_MAX = 8
NA_KW = 16
RET_HEADS = 8
RET_QK = 256
RET_V = 512
RET_CHUNK = 128
N_EXPERTS = 64
N_GROUPS = 8
EXPERTS_PER_GROUP = 8
TOPK_GROUPS = 4
TOP_K = 6
D_EXPERT = 512
D_SHARED = 512
ROUTED_SCALE = 2.5
ROPE_BASE = 10000.0
LN_EPS = 1e-5
ALPHA = (2 * DEPTH) ** 0.25
N_COND = 16

VMEM_LIMIT = 56 * 1024 * 1024

MOE_TB = 256
MOE_NBLK = (NT * TOP_K) // MOE_TB + N_EXPERTS
MOE_CAP = MOE_NBLK * MOE_TB
RT_TM = 512
CMB_TM = 128
PK_S = 8
PK_L = D // 2 // PK_S
Y_S = D // 128
PK_P = PK_S + 1
Y_P = Y_S + 1
HI_MASK = np.uint32(0xFFFF0000)


def _cparams(sem):
    return pltpu.CompilerParams(dimension_semantics=sem, vmem_limit_bytes=VMEM_LIMIT)


def _cond_row(row0):
    return jnp.where(row0 < NP, 0, 1 + (row0 - NP) // DEC_SEQ)


def _mod_spec(layer, chunk, tm, moff=0):
    return pl.BlockSpec((None, None, None, 1, D),
                        lambda m, *_: (layer, _cond_row((m + moff) * tm), chunk, 0, 0))


def _sigmoid(x):
    return 1.0 / (1.0 + jnp.exp(-x))


def _silu(x):
    return x * _sigmoid(x)


def _mod_kernel(cond_ref, w_ref, b_ref, o_ref):
    c = _silu(cond_ref[...]).astype(BF16)
    o_ref[...] = jnp.dot(c, w_ref[...].astype(BF16), preferred_element_type=F32) + b_ref[...]


def modulation_table(cond, ada_w, ada_b):
    tn = 1024
    n6 = 6 * D
    out = pl.pallas_call(
        _mod_kernel,
        grid=(DEPTH, n6 // tn),
        in_specs=[pl.BlockSpec((N_COND, D), lambda l, n: (0, 0)),
                  pl.BlockSpec((None, D, tn), lambda l, n: (l, 0, n)),
                  pl.BlockSpec((None, 1, tn), lambda l, n: (l, 0, n))],
        out_specs=pl.BlockSpec((None, N_COND, tn), lambda l, n: (l, 0, n)),
        out_shape=jax.ShapeDtypeStruct((DEPTH, N_COND, n6), F32),
        compiler_params=_cparams(("arbitrary", "arbitrary")),
        name="modulation_table",
    )(cond, ada_w, ada_b.reshape(DEPTH, 1, n6))
    return out.reshape(DEPTH, N_COND, 6, 1, D)


def _modulate_kernel(x_ref, sh_ref, sc_ref, o_ref):
    o_ref[...] = (x_ref[...] * (1.0 + sc_ref[...]) + sh_ref[...]).astype(o_ref.dtype)


def modulate(x, mods, layer):
    tm = 512
    return pl.pallas_call(
        _modulate_kernel,
        grid=(NT // tm,),
        in_specs=[pl.BlockSpec((tm, D), lambda m: (m, 0)),
                  _mod_spec(layer, 0, tm), _mod_spec(layer, 1, tm)],
        out_specs=pl.BlockSpec((tm, D), lambda m: (m, 0)),
        out_shape=jax.ShapeDtypeStruct((NT, D), BF16),
        compiler_params=_cparams(("arbitrary",)),
        name="modulate",
    )(x, mods, mods)


def _mm_kernel(a_ref, w_ref, o_ref):
    o_ref[...] = jnp.dot(a_ref[...], w_ref[...], preferred_element_type=F32).astype(o_ref.dtype)


def matmul(a, w, row0, nrows, out_dtype):
    k = a.shape[1]
    n = w.shape[1]
    tm, tn = 1024, 512
    moff = row0 // tm
    return pl.pallas_call(
        _mm_kernel,
        grid=(nrows // tm, n // tn),
        in_specs=[pl.BlockSpec((tm, k), lambda m, j: (m + moff, 0)),
                  pl.BlockSpec((k, tn), lambda m, j: (0, j))],
        out_specs=pl.BlockSpec((tm, tn), lambda m, j: (m, j)),
        out_shape=jax.ShapeDtypeStruct((nrows, n), out_dtype),
        compiler_params=_cparams(("arbitrary", "arbitrary")),
        name="in_proj",
    )(a, w)


def _rope_tables():
    half = DIFF_QK // 2
    t = jnp.arange(DEC_SEQ)
    inv = ROPE_BASE ** (-jnp.arange(0, half, 2, dtype=F32) / half)

    def cs(pos):
        ang = pos.astype(F32)[:, None] * inv[None, :]
        return jnp.cos(ang), jnp.sin(ang)

    cr, sr = cs(t // GRID_W)
    cc, sc = cs(t % GRID_W)
    cos = jnp.concatenate([cr, cr, cc, cc], axis=-1)
    sin = jnp.concatenate([-sr, sr, -sc, sc], axis=-1)
    return cos, sin


def _rope_kernel(x_ref, cos_ref, sin_ref, o_ref):
    cos = cos_ref[...]
    sin = sin_ref[...]
    lane = lax.broadcasted_iota(jnp.int32, cos.shape, 1)
    first = (lane % (DIFF_QK // 2)) < (DIFF_QK // 4)
    for g in range(D // DIFF_QK):
        xg = x_ref[:, g * DIFF_QK:(g + 1) * DIFF_QK]
        sw = jnp.where(first, pltpu.roll(xg, DIFF_QK - DIFF_QK // 4, 1), pltpu.roll(xg, DIFF_QK // 4, 1))
        o_ref[:, g * DIFF_QK:(g + 1) * DIFF_QK] = (xg * cos + sw * sin).astype(o_ref.dtype)


def rope(x, cos, sin):
    tm = 512
    nt = DEC_SEQ // tm
    return pl.pallas_call(
        _rope_kernel,
        grid=(NS // tm,),
        in_specs=[pl.BlockSpec((tm, D), lambda m: (m, 0)),
                  pl.BlockSpec((tm, DIFF_QK), lambda m: (m % nt, 0)),
                  pl.BlockSpec((tm, DIFF_QK), lambda m: (m % nt, 0))],
        out_specs=pl.BlockSpec((tm, D), lambda m: (m, 0)),
        out_shape=jax.ShapeDtypeStruct((NS, D), BF16),
        compiler_params=_cparams(("arbitrary",)),
        name="rope",
    )(x, cos, sin)


def _qkt(q, k):
    return lax.dot_general(q, k, (((1,), (1,)), ((), ())), preferred_element_type=F32)


def _softmax(s):
    m = jnp.max(s, axis=-1, keepdims=True)
    p = jnp.exp(s - m)
    return p / jnp.sum(p, axis=-1, keepdims=True)


def _diff_kernel(lam_init, lamp_ref, g_ref, q_ref, k_ref, v_ref, o_ref):
    lp = lamp_ref[...]
    lam = (jnp.exp(jnp.sum(lp[0:1] * lp[1:2], axis=-1, keepdims=True))
           - jnp.exp(jnp.sum(lp[2:3] * lp[3:4], axis=-1, keepdims=True)) + lam_init)
    scale = DIFF_QK ** -0.5
    g = g_ref[...]
    for h in range(DIFF_HEADS):
        c0 = h * DIFF_V
        q = q_ref[:, c0:c0 + DIFF_V].astype(BF16)
        k = k_ref[:, c0:c0 + DIFF_V].astype(BF16)
        v = v_ref[:, c0:c0 + DIFF_V].astype(BF16)
        a1 = _softmax(_qkt(q[:, :DIFF_QK], k[:, :DIFF_QK]) * scale)
        a2 = _softmax(_qkt(q[:, DIFF_QK:], k[:, DIFF_QK:]) * scale)
        a = (a1 - lam * a2).astype(BF16)
        o = jnp.dot(a, v, preferred_element_type=F32)
        o = o * lax.rsqrt(jnp.mean(jnp.square(o), axis=-1, keepdims=True) + LN_EPS) * g
        o_ref[:, c0:c0 + DIFF_V] = (o * (1.0 - lam_init)).astype(o_ref.dtype)


def diff_attention(q, k, v, lam_p, subln_g, lam_init, nb, lq, lk, tq):
    nq = lq // tq
    return pl.pallas_call(
        functools.partial(_diff_kernel, lam_init),
        grid=(nb, nq),
        in_specs=[pl.BlockSpec((4, DIFF_QK), lambda b, i: (0, 0)),
                  pl.BlockSpec((1, DIFF_V), lambda b, i: (0, 0)),
                  pl.BlockSpec((tq, D), lambda b, i: (b * nq + i, 0)),
                  pl.BlockSpec((lk, D), lambda b, i: (b, 0)),
                  pl.BlockSpec((lk, D), lambda b, i: (b, 0))],
        out_specs=pl.BlockSpec((tq, D), lambda b, i: (b * nq + i, 0)),
        out_shape=jax.ShapeDtypeStruct((nb * lq, D), BF16),
        compiler_params=_cparams(("arbitrary", "arbitrary")),
        name="diff_attention",
    )(lam_p, subln_g.reshape(1, DIFF_V), q, k, v)


def _na_ctx_kernel(q_ref, k_ref, v_ref, o_ref):
    scale = NA_DH ** -0.5
    for h in range(NA_HEADS):
        c0 = h * NA_DH
        q = q_ref[:, c0:c0 + NA_DH].astype(BF16)
        k = k_ref[:, c0:c0 + NA_DH].astype(BF16)
        v = v_ref[:, c0:c0 + NA_DH].astype(BF16)
        p = _softmax(_qkt(q, k) * scale).astype(BF16)
        o_ref[:, c0:c0 + NA_DH] = jnp.dot(p, v, preferred_element_type=F32).astype(o_ref.dtype)


def na_context_attention(q, k, v):
    spec = pl.BlockSpec((SEQ, D), lambda b: (b, 0))
    return pl.pallas_call(
        _na_ctx_kernel,
        grid=(BATCH,),
        in_specs=[spec, spec, spec],
        out_specs=spec,
        out_shape=jax.ShapeDtypeStruct((NP, D), BF16),
        compiler_params=_cparams(("arbitrary",)),
        name="na_context_attention",
    )(q, k, v)


NA_ROWS = DEC_SEQ // GRID_W
NA_NLOC = NA_KH * GRID_W


def _na_window_row(r):
    return jnp.clip(r - NA_KH // 2, 0, NA_ROWS - NA_KH)


def _na_bias_tables(rpb):
    col = jnp.arange(GRID_W)
    col_start = jnp.clip(col - NA_KW // 2, 0, GRID_W - NA_KW)
    col_ok = (col[None, :] >= col_start[:, None]) & (col[None, :] < col_start[:, None] + NA_KW)
    dc_idx = jnp.clip(col[None, :] - col[:, None] + NA_KW - 1, 0, 2 * NA_KW - 2)
    hot = (dc_idx[:, :, None] == jnp.arange(2 * NA_KW - 1)[None, None, :]).astype(F32)
    full = jnp.einsum('hrd,qkd->hqrk', rpb.astype(F32), hot, precision=lax.Precision.HIGHEST)
    full = jnp.where(col_ok[None, :, None, :], full, -jnp.inf)
    full = full.reshape(NA_HEADS, GRID_W, (2 * NA_KH_MAX - 1) * GRID_W)
    tabs = [full[:, :, (NA_KH_MAX - 1 - o) * GRID_W:(NA_KH_MAX - 1 - o) * GRID_W + NA_NLOC]
            for o in range(NA_KH)]
    return jnp.stack(tabs)


def _na_lat_kernel(q_ref, k_ref, v_ref, kc_ref, vc_ref, bias_ref, o_ref):
    r = pl.program_id(1)
    start = pl.multiple_of(_na_window_row(r) * GRID_W, GRID_W)
    scale = NA_DH ** -0.5
    for h in range(NA_HEADS):
        c0 = h * NA_DH
        q = q_ref[:, c0:c0 + NA_DH]
        kw = k_ref[pl.ds(start, NA_NLOC), c0:c0 + NA_DH]
        vw = v_ref[pl.ds(start, NA_NLOC), c0:c0 + NA_DH]
        s_loc = _qkt(q, kw) * scale + bias_ref[h]
        s_ctx = _qkt(q, kc_ref[:, c0:c0 + NA_DH]) * scale
        m = jnp.maximum(jnp.max(s_loc, axis=-1, keepdims=True), jnp.max(s_ctx, axis=-1, keepdims=True))
        p_loc = jnp.exp(s_loc - m)
        p_ctx = jnp.exp(s_ctx - m)
        den = jnp.sum(p_loc, axis=-1, keepdims=True) + jnp.sum(p_ctx, axis=-1, keepdims=True)
        o = (jnp.dot((p_loc / den).astype(BF16), vw, preferred_element_type=F32)
             + jnp.dot((p_ctx / den).astype(BF16), vc_ref[:, c0:c0 + NA_DH], preferred_element_type=F32))
        o_ref[:, c0:c0 + NA_DH] = o.astype(o_ref.dtype)


def na_latent_attention(q, k, v, k_ctx, v_ctx, bias_tabs):
    return pl.pallas_call(
        _na_lat_kernel,
        grid=(DEC_BATCH, NA_ROWS),
        in_specs=[pl.BlockSpec((GRID_W, D), lambda b, r: (b * NA_ROWS + r, 0)),
                  pl.BlockSpec((DEC_SEQ, D), lambda b, r: (b, 0)),
                  pl.BlockSpec((DEC_SEQ, D), lambda b, r: (b, 0)),
                  pl.BlockSpec((None, PAST, D), lambda b, r: (b, 0, 0)),
                  pl.BlockSpec((None, PAST, D), lambda b, r: (b, 0, 0)),
                  pl.BlockSpec((None, NA_HEADS, GRID_W, NA_NLOC),
                               lambda b, r: (r - _na_window_row(r), 0, 0, 0))],
        out_specs=pl.BlockSpec((GRID_W, D), lambda b, r: (b * NA_ROWS + r, 0)),
        out_shape=jax.ShapeDtypeStruct((NS, D), BF16),
        compiler_params=_cparams(("arbitrary", "arbitrary")),
        name="na_latent_attention",
    )(q, k, v, k_ctx, v_ctx, bias_tabs)


def _ret_kernel(nc, has_state, emit_state, logg_ref, q_ref, k_ref, v_ref, g_ref, *rest):
    rest = list(rest)
    s0_ref = rest.pop(0) if has_state else None
    o_ref = rest.pop(0)
    st_ref = rest.pop(0) if emit_state else None
    oacc, sacc = rest
    h = pl.program_id(1)
    lgf = logg_ref[0, h]
    lgb = logg_ref[1, h]
    c = RET_CHUNK
    pos = lax.broadcasted_iota(jnp.int32, (c, 1), 0).astype(F32)
    diff = (lax.broadcasted_iota(jnp.int32, (c, c), 0) - lax.broadcasted_iota(jnp.int32, (c, c), 1)).astype(F32)
    dmask_f = jnp.where(diff >= 0, jnp.exp(jnp.maximum(diff, 0.0) * lgf), 0.0)
    dmask_b = jnp.where(diff <= 0, jnp.exp(jnp.maximum(-diff, 0.0) * lgb), 0.0)
    qdec_f = jnp.exp((pos + 1.0) * lgf)
    kdec_f = jnp.exp((c - 1.0 - pos) * lgf)
    qdec_b = jnp.exp((c - pos) * lgb)
    kdec_b = jnp.exp(pos * lgb)
    cd_f = jnp.exp(jnp.full((1, 1), c, F32) * lgf)
    cd_b = jnp.exp(jnp.full((1, 1), c, F32) * lgb)

    def chunk(i):
        sl = slice(i * c, (i + 1) * c)
        return q_ref[sl, :].astype(BF16), k_ref[sl, :] * (RET_QK ** -0.5), v_ref[sl, :].astype(BF16)

    def state_update(kc, kdec, vb, cd):
        kd = (kc * kdec).T.astype(BF16)
        sacc[...] = sacc[...] * cd + jnp.dot(kd, vb, preferred_element_type=F32)

    if has_state:
        sacc[...] = s0_ref[0]
    else:
        sacc[...] = jnp.zeros_like(sacc)
    for i in range(nc):
        qb, kc, vb = chunk(i)
        qk = _qkt(qb, kc.astype(BF16))
        o = (jnp.dot((qk * dmask_f).astype(BF16), vb, preferred_element_type=F32)
             + jnp.dot((qk * dmask_b).astype(BF16), vb, preferred_element_type=F32)
             + jnp.dot(qb, sacc[...].astype(BF16), preferred_element_type=F32) * qdec_f)
        oacc[i * c:(i + 1) * c, :] = o
        state_update(kc, kdec_f, vb, cd_f)
    if emit_state:
        st_ref[0] = sacc[...]

    if has_state:
        sacc[...] = s0_ref[1]
    else:
        sacc[...] = jnp.zeros_like(sacc)
    for i in reversed(range(nc)):
        qb, kc, vb = chunk(i)
        oacc[i * c:(i + 1) * c, :] += jnp.dot(qb, sacc[...].astype(BF16), preferred_element_type=F32) * qdec_b
        state_update(kc, kdec_b, vb, cd_b)
    if emit_state:
        st_ref[1] = sacc[...]

    o = oacc[...]
    mu = jnp.mean(o, axis=-1, keepdims=True)
    var = jnp.mean(jnp.square(o - mu), axis=-1, keepdims=True)
    o = (o - mu) * lax.rsqrt(var + LN_EPS)
    o_ref[...] = (o * _silu(g_ref[...])).astype(o_ref.dtype)


def retention(q, k, v, g, log_g, nb, l, state=None, emit_state=False):
    nc = l // RET_CHUNK
    has_state = state is not None
    in_specs = [pl.BlockSpec(memory_space=pltpu.SMEM),
                pl.BlockSpec((l, RET_QK), lambda b, h: (b, h)),
                pl.BlockSpec((l, RET_QK), lambda b, h: (b, h)),
                pl.BlockSpec((l, RET_V), lambda b, h: (b, h)),
                pl.BlockSpec((l, RET_V), lambda b, h: (b, h))]
    args = [log_g, q, k, v, g]
    st_spec = pl.BlockSpec((None, 2, None, RET_QK, RET_V), lambda b, h: (b, 0, h, 0, 0))
    if has_state:
        in_specs.append(st_spec)
        args.append(state)
    out_specs = [pl.BlockSpec((l, RET_V), lambda b, h: (b, h))]
    out_shape = [jax.ShapeDtypeStruct((nb * l, RET_HEADS * RET_V), BF16)]
    if emit_state:
        out_specs.append(st_spec)
        out_shape.append(jax.ShapeDtypeStruct((nb, 2, RET_HEADS, RET_QK, RET_V), F32))
    res = pl.pallas_call(
        functools.partial(_ret_kernel, nc, has_state, emit_state),
        grid=(nb, RET_HEADS),
        in_specs=in_specs,
        out_specs=out_specs,
        out_shape=out_shape,
        scratch_shapes=[pltpu.VMEM((l, RET_V), F32), pltpu.VMEM((RET_QK, RET_V), F32)],
        compiler_params=_cparams(("arbitrary", "arbitrary")),
        name="retention",
    )(*args)
    return res if emit_state else res[0]


def _layer_norm(z, g, b):
    mu = jnp.mean(z, axis=-1, keepdims=True)
    var = jnp.mean(jnp.square(z - mu), axis=-1, keepdims=True)
    return (z - mu) * lax.rsqrt(var + LN_EPS) * g + b


def _pack_rows(hn, hp_ref):
    tm = hn.shape[0]
    half = D // 2
    lo = lax.bitcast_convert_type(hn[:, :half].astype(BF16).astype(F32), U32)
    hi = lax.bitcast_convert_type(hn[:, half:].astype(BF16).astype(F32), U32)
    w = (lo >> 16) | (hi & HI_MASK)
    for s in range(PK_S):
        hp_ref[pl.ds(s, tm, stride=PK_P), :] = w[:, s * PK_L:(s + 1) * PK_L]
    hp_ref[pl.ds(PK_S, tm, stride=PK_P), :] = jnp.zeros((tm, PK_L), U32)


def _unpack_rows(hp_ref, tm):
    lo, hi = [], []
    for s in range(PK_S):
        w = hp_ref[pl.ds(s, tm, stride=PK_P), :]
        lo.append(lax.bitcast_convert_type(w << 16, F32).astype(BF16))
        hi.append(lax.bitcast_convert_type(w & HI_MASK, F32).astype(BF16))
    return jnp.concatenate(lo + hi, axis=1)


def _route_tile(hn, wrt_ref, rb_ref, e_ref, w_ref, rank_ref, cnt_ref):
    tm = hn.shape[0]
    neg = -jnp.inf
    logits = lax.dot_general(wrt_ref[...], hn, (((1,), (1,)), ((), ())),
                             preferred_element_type=F32, precision=lax.Precision.HIGHEST)
    s = _sigmoid(logits)
    biased = s + rb_ref[...]
    io8 = lax.broadcasted_iota(I32, (EXPERTS_PER_GROUP, tm), 0).astype(F32)
    slabs, gscore = [], []
    for g in range(N_GROUPS):
        slab = biased[g * EXPERTS_PER_GROUP:(g + 1) * EXPERTS_PER_GROUP, :]
        m1 = jnp.max(slab, axis=0, keepdims=True)
        i1 = jnp.min(jnp.where(slab == m1, io8, float(EXPERTS_PER_GROUP)), axis=0, keepdims=True)
        m2 = jnp.max(jnp.where(io8 == i1, neg, slab), axis=0, keepdims=True)
        slabs.append(slab)
        gscore.append(m1 + m2)
    masked = []
    for g in range(N_GROUPS):
        ahead = jnp.zeros_like(gscore[g])
        for g2 in range(N_GROUPS):
            if g2 == g:
                continue
            better = (gscore[g2] > gscore[g]) | ((gscore[g2] == gscore[g]) if g2 < g else False)
            ahead = ahead + jnp.where(better, 1.0, 0.0)
        masked.append(jnp.where(ahead < float(TOPK_GROUPS), slabs[g], neg))
    v = jnp.concatenate(masked, axis=0)
    io = lax.broadcasted_iota(I32, (N_EXPERTS, tm), 0).astype(F32)
    idxs, ws, hots = [], [], []
    for _ in range(TOP_K):
        m = jnp.max(v, axis=0, keepdims=True)
        idx = jnp.min(jnp.where(v == m, io, float(N_EXPERTS)), axis=0, keepdims=True)
        hot = io == idx
        ws.append(jnp.sum(jnp.where(hot, s, 0.0), axis=0, keepdims=True))
        v = jnp.where(hot, neg, v)
        idxs.append(idx)
        hots.append(hot)
    wsum = ws[0]
    for k in range(1, TOP_K):
        wsum = wsum + ws[k]
    chosen = jnp.where(hots[0], 1.0, 0.0)
    for k in range(1, TOP_K):
        chosen = chosen + jnp.where(hots[k], 1.0, 0.0)
    upper = jnp.where(lax.broadcasted_iota(I32, (tm, tm), 0) < lax.broadcasted_iota(I32, (tm, tm), 1), 1.0, 0.0)
    before = jnp.dot(chosen.astype(BF16), upper.astype(BF16), preferred_element_type=F32)
    zero_row = jnp.zeros((1, tm), F32)
    for k in range(8):
        if k < TOP_K:
            e_ref[k:k + 1, :] = idxs[k].astype(I32)
            w_ref[k:k + 1, :] = ws[k] / wsum * ROUTED_SCALE
            rank_ref[k:k + 1, :] = jnp.sum(jnp.where(hots[k], before, 0.0), axis=0, keepdims=True).astype(I32)
        else:
            e_ref[k:k + 1, :] = zero_row.astype(I32)
            w_ref[k:k + 1, :] = zero_row
            rank_ref[k:k + 1, :] = zero_row.astype(I32)
    cnt = jnp.sum(chosen, axis=1, keepdims=True)
    cnt_ref[...] = jnp.broadcast_to(cnt, (N_EXPERTS, 128)).astype(I32)


def _outproj_kernel(nk, o_ref, w_ref, x_ref, gate_ref, lng_ref, lnb_ref, sh_ref, sc_ref, wrt_ref, rb_ref,
                    xo_ref, hp_ref, e_ref, wt_ref, rank_ref, cnt_ref, acc_ref):
    kk = pl.program_id(1)
    part = jnp.dot(o_ref[...], w_ref[...], preferred_element_type=F32)

    @pl.when(kk == 0)
    def _():
        acc_ref[...] = part

    @pl.when(kk > 0)
    def _():
        acc_ref[...] += part

    @pl.when(kk == nk - 1)
    def _():
        xn = _layer_norm(ALPHA * x_ref[...] + gate_ref[...] * acc_ref[...], lng_ref[...], lnb_ref[...])
        xo_ref[...] = xn
        hn = xn * (1.0 + sc_ref[...]) + sh_ref[...]
        _pack_rows(hn, hp_ref)
        _route_tile(hn, wrt_ref, rb_ref, e_ref, wt_ref, rank_ref, cnt_ref)


def outproj_ln(o, w_out, x, mods, ln_g, ln_b, w_router_t, b_router, layer):
    k = o.shape[1]
    tk = 2048
    nk = k // tk
    tm = RT_TM if nk == 1 else RT_TM // 2
    ntile = NT // tm
    row = lambda m, kk: (m, 0)
    col = lambda m, kk: (0, m)
    return pl.pallas_call(
        functools.partial(_outproj_kernel, nk),
        grid=(ntile, nk),
        in_specs=[pl.BlockSpec((tm, tk), lambda m, kk: (m, kk)),
                  pl.BlockSpec((tk, D), lambda m, kk: (kk, 0),
                               pipeline_mode=pl.Buffered(1) if nk == 1 else None),
                  pl.BlockSpec((tm, D), row),
                  _mod_spec(layer, 2, tm),
                  pl.BlockSpec((None, None, 1, D), lambda m, kk: (layer, 0, 0, 0)),
                  pl.BlockSpec((None, None, 1, D), lambda m, kk: (layer, 0, 0, 0)),
                  _mod_spec(layer, 3, tm), _mod_spec(layer, 4, tm),
                  pl.BlockSpec((N_EXPERTS, D), lambda m, kk: (0, 0)),
                  pl.BlockSpec((N_EXPERTS, 1), lambda m, kk: (0, 0))],
        out_specs=[pl.BlockSpec((tm, D), row),
                   pl.BlockSpec((tm * PK_P, PK_L), row),
                   pl.BlockSpec((8, tm), col), pl.BlockSpec((8, tm), col), pl.BlockSpec((8, tm), col),
                   pl.BlockSpec((None, N_EXPERTS, 128), lambda m, kk: (m, 0, 0))],
        out_shape=[jax.ShapeDtypeStruct((NT, D), F32),
                   jax.ShapeDtypeStruct((NT * PK_P, PK_L), U32),
                   jax.ShapeDtypeStruct((8, NT), I32), jax.ShapeDtypeStruct((8, NT), F32),
                   jax.ShapeDtypeStruct((8, NT), I32),
                   jax.ShapeDtypeStruct((ntile, N_EXPERTS, 128), I32)],
        scratch_shapes=[pltpu.VMEM((tm, D), F32)],
        compiler_params=_cparams(("arbitrary", "arbitrary")),
        name="outproj_ln_route",
    )(o, w_out, x, mods, ln_g, ln_b, mods, mods, w_router_t, b_router.reshape(N_EXPERTS, 1))


def _slot_plan(e_t, rank_t, cnt):
    cnt = cnt[:, :, 0]
    tile_base = jnp.cumsum(cnt, axis=0) - cnt
    counts = jnp.sum(cnt, axis=0)
    padded = (counts + MOE_TB - 1) // MOE_TB * MOE_TB
    pad_end = jnp.cumsum(padded)
    pad_start = pad_end - padded
    offs = jnp.repeat(pad_start[None, :] + tile_base, NT // cnt.shape[0], axis=0)
    hot = e_t[:TOP_K, :, None] == jnp.arange(N_EXPERTS, dtype=I32)[None, None, :]
    dest = jnp.sum(jnp.where(hot, offs[None], 0), axis=-1) + rank_t[:TOP_K]
    ntile = NT // CMB_TM
    dest_tiles = dest.reshape(TOP_K, ntile, CMB_TM).transpose(1, 0, 2).reshape(ntile, 1, TOP_K * CMB_TM)
    nused = pad_end[-1] // MOE_TB
    blk = jnp.minimum(jnp.arange(MOE_NBLK, dtype=I32), nused - 1) * MOE_TB
    block_e = jnp.minimum(jnp.sum((pad_end[None, :] <= blk[:, None]).astype(I32), axis=1), N_EXPERTS - 1)
    first = jnp.concatenate([jnp.ones((1,), I32), (block_e[1:] != block_e[:-1]).astype(I32)])
    zero_start = pad_start + counts
    return (dest_tiles.astype(I32), block_e.astype(I32), first, nused.reshape(1).astype(I32),
            zero_start.astype(I32))


_DISPATCH_ROWS = TOP_K * CMB_TM
_CMB_ROWS = 32


def _dispatch_kernel(zs_ref, nused_ref, dest_ref, hp_ref, xs_hbm, stage, zbuf, zsem, sem):
    i = pl.program_id(0)
    n = pl.num_programs(0)
    slot = i % 2

    @pl.when(i == 0)
    def _():
        zbuf[...] = jnp.zeros_like(zbuf)

        def zcopy(slot0, nslot):
            return pltpu.make_async_copy(zbuf.at[pl.ds(0, nslot * PK_P)],
                                         xs_hbm.at[pl.ds(slot0 * PK_P, nslot * PK_P)], zsem)

        def pad_pieces(e, wait):
            slot0 = zs_ref[e]
            npad = (MOE_TB - slot0 % MOE_TB) % MOE_TB
            piece = MOE_TB // 2
            while piece >= 1:
                @pl.when((npad & piece) != 0)
                def _(slot0=slot0, piece=piece):
                    cp = zcopy(slot0, piece)
                    cp.wait() if wait else cp.start()
                slot0 = slot0 + (npad & piece)
                piece //= 2

        def zstart(e, carry):
            pad_pieces(e, False)
            return carry

        def zwait(e, carry):
            pad_pieces(e, True)
            return carry

        def tstart(b, carry):
            zcopy(b * MOE_TB, MOE_TB).start()
            return carry

        def twait(b, carry):
            zcopy(b * MOE_TB, MOE_TB).wait()
            return carry

        lax.fori_loop(0, N_EXPERTS, zstart, 0)
        lax.fori_loop(nused_ref[0], MOE_NBLK + 1, tstart, 0)
        lax.fori_loop(0, N_EXPERTS, zwait, 0)
        lax.fori_loop(nused_ref[0], MOE_NBLK + 1, twait, 0)

    stage[slot] = hp_ref[...]

    def issue(r, carry):
        for k in range(TOP_K):
            dst = dest_ref[0, 0, k * CMB_TM + r] * PK_P
            pltpu.make_async_copy(stage.at[slot, pl.ds(r * PK_P, PK_P)], xs_hbm.at[pl.ds(dst, PK_P)],
                                  sem.at[slot]).start(priority=k % 2)
        return carry

    lax.fori_loop(0, CMB_TM, issue, 0, unroll=2)

    def drain(s):
        for k in range(TOP_K):
            pltpu.make_async_copy(stage.at[s], xs_hbm.at[pl.ds(0, CMB_TM * PK_P)], sem.at[s]).wait()

    @pl.when(i > 0)
    def _():
        drain(1 - slot)

    @pl.when(i == n - 1)
    def _():
        drain(slot)


def moe_dispatch(hp, dest_tiles, zero_start, nused):
    ntile = NT // CMB_TM
    grid_spec = pltpu.PrefetchScalarGridSpec(
        num_scalar_prefetch=2,
        grid=(ntile,),
        in_specs=[pl.BlockSpec((1, 1, _DISPATCH_ROWS), lambda i, zs, nu: (i, 0, 0), memory_space=pltpu.SMEM),
                  pl.BlockSpec((CMB_TM * PK_P, PK_L), lambda i, zs, nu: (i, 0))],
        out_specs=pl.BlockSpec(memory_space=pl.ANY),
        scratch_shapes=[pltpu.VMEM((2, CMB_TM * PK_P, PK_L), U32),
                        pltpu.VMEM((MOE_TB * PK_P, PK_L), U32),
                        pltpu.SemaphoreType.DMA(()), pltpu.SemaphoreType.DMA((2,))],
    )
    return pl.pallas_call(
        _dispatch_kernel,
        grid_spec=grid_spec,
        out_shape=jax.ShapeDtypeStruct(((MOE_CAP + MOE_TB) * PK_P, PK_L), U32),
        compiler_params=_cparams(("arbitrary",)),
        name="moe_dispatch",
    )(zero_start, nused, dest_tiles, hp)


def _moe_kernel(be_ref, first_ref, nused_ref, xs_ref, wg_ref, wu_ref, wd_ref, y_ref, wgb, wub, wdb):
    i = pl.program_id(0)

    @pl.when(first_ref[i] == 1)
    def _():
        wgb[...] = wg_ref[...].astype(BF16)
        wub[...] = wu_ref[...].astype(BF16)
        wdb[...] = wd_ref[...].astype(BF16)

    @pl.when(i < nused_ref[0])
    def _():
        x = _unpack_rows(xs_ref, MOE_TB)
        a = (_silu(jnp.dot(x, wgb[...], preferred_element_type=F32))
             * jnp.dot(x, wub[...], preferred_element_type=F32))
        y = jnp.dot(a.astype(BF16), wdb[...], preferred_element_type=F32)
        for j in range(Y_S):
            y_ref[pl.ds(j, MOE_TB, stride=Y_P), :] = y[:, j * 128:(j + 1) * 128]
        y_ref[pl.ds(Y_S, MOE_TB, stride=Y_P), :] = jnp.zeros((MOE_TB, 128), F32)

    @pl.when(i >= nused_ref[0])
    def _():
        y_ref[...] = jnp.zeros_like(y_ref)


def moe_experts(xs, block_e, first, nused, w_gate, w_up, w_down, layer):
    wspec = lambda shape: pl.BlockSpec((None, None) + shape, lambda i, be, fi, nu: (layer, be[i], 0, 0))
    blk = lambda i, be, fi, nu: (jnp.minimum(i, nu[0] - 1), 0)
    grid_spec = pltpu.PrefetchScalarGridSpec(
        num_scalar_prefetch=3,
        grid=(MOE_NBLK,),
        in_specs=[pl.BlockSpec((MOE_TB * PK_P, PK_L), blk),
                  wspec((D, D_EXPERT)), wspec((D, D_EXPERT)), wspec((D_EXPERT, D))],
        out_specs=pl.BlockSpec((MOE_TB * Y_P, 128), lambda i, be, fi, nu: (i, 0)),
        scratch_shapes=[pltpu.VMEM((D, D_EXPERT), BF16), pltpu.VMEM((D, D_EXPERT), BF16),
                        pltpu.VMEM((D_EXPERT, D), BF16)],
    )
    return pl.pallas_call(
        _moe_kernel,
        grid_spec=grid_spec,
        out_shape=jax.ShapeDtypeStruct((MOE_CAP * Y_P, 128), F32),
        compiler_params=_cparams(("arbitrary",)),
        name="moe_experts",
    )(block_e, first, nused, xs, w_gate, w_up, w_down)


def _combine_kernel(has_next, dest_ref, destn_ref, y_hbm, w_ref, hp_ref, x_ref, wsg_ref, wsu_ref, wsd_ref,
                    gate_ref, lng_ref, lnb_ref, *rest):
    if has_next:
        sh_ref, sc_ref, xo_ref, ho_ref, buf, routed_ref, sem = rest
    else:
        xo_ref, buf, routed_ref, sem = rest
    i = pl.program_id(0)
    n = pl.num_programs(0)
    slot = i % 2
    nrow = TOP_K * CMB_TM

    def gather(dref, s):
        def body(r, carry):
            for k in range(TOP_K):
                src = dref[0, 0, k * CMB_TM + r] * Y_P
                dst = (k * CMB_TM + r) * Y_P
                pltpu.make_async_copy(y_hbm.at[pl.ds(src, Y_S)], buf.at[s, pl.ds(dst, Y_S)],
                                      sem.at[s]).start(priority=k % 2)
            return carry
        lax.fori_loop(0, CMB_TM, body, 0, unroll=2)

    @pl.when(i == 0)
    def _():
        gather(dest_ref, 0)

    @pl.when(i + 1 < n)
    def _():
        gather(destn_ref, 1 - slot)

    hb = _unpack_rows(hp_ref, CMB_TM)
    a = (_silu(jnp.dot(hb, wsg_ref[...], preferred_element_type=F32))
         * jnp.dot(hb, wsu_ref[...], preferred_element_type=F32))
    shared = jnp.dot(a.astype(BF16), wsd_ref[...], preferred_element_type=F32)

    pltpu.make_async_copy(y_hbm.at[pl.ds(0, nrow * Y_S)], buf.at[slot, pl.ds(0, nrow * Y_S)],
                          sem.at[slot]).wait()
    for r0 in range(0, CMB_TM, _CMB_ROWS):
        wsub = w_ref[r0:r0 + _CMB_ROWS, :]
        wb = [jnp.broadcast_to(wsub[:, k:k + 1], (_CMB_ROWS, 128)) for k in range(TOP_K)]
        for j in range(Y_S):
            acc = buf[slot, pl.ds(r0 * Y_P + j, _CMB_ROWS, stride=Y_P), :] * wb[0]
            for k in range(1, TOP_K):
                acc = acc + buf[slot, pl.ds((k * CMB_TM + r0) * Y_P + j, _CMB_ROWS, stride=Y_P), :] * wb[k]
            routed_ref[r0:r0 + _CMB_ROWS, j * 128:(j + 1) * 128] = acc
    routed = routed_ref[...]
    xn = _layer_norm(ALPHA * x_ref[...] + gate_ref[...] * (routed + shared), lng_ref[...], lnb_ref[...])
    xo_ref[...] = xn
    if has_next:
        ho_ref[...] = (xn * (1.0 + sc_ref[...]) + sh_ref[...]).astype(ho_ref.dtype)


def moe_combine(y_slots, dest_tiles, w_tok, hp, x1, ws_gate, ws_up, ws_down, mods, ln_g, ln_b, layer):
    has_next = layer + 1 < DEPTH
    tm = CMB_TM
    ntile = NT // tm
    row = lambda m: (m, 0)
    const = lambda m: (0, 0)
    in_specs = [pl.BlockSpec((1, 1, TOP_K * tm), lambda m: (m, 0, 0), memory_space=pltpu.SMEM),
                pl.BlockSpec((1, 1, TOP_K * tm), lambda m: (jnp.minimum(m + 1, ntile - 1), 0, 0),
                             memory_space=pltpu.SMEM),
                pl.BlockSpec(memory_space=pl.ANY),
                pl.BlockSpec((tm, 8), row),
                pl.BlockSpec((tm * PK_P, PK_L), row), pl.BlockSpec((tm, D), row),
                pl.BlockSpec((D, D_SHARED), const), pl.BlockSpec((D, D_SHARED), const),
                pl.BlockSpec((D_SHARED, D), const),
                _mod_spec(layer, 5, tm),
                pl.BlockSpec((None, None, 1, D), lambda m: (layer, 1, 0, 0)),
                pl.BlockSpec((None, None, 1, D), lambda m: (layer, 1, 0, 0))]
    args = [dest_tiles, dest_tiles, y_slots, w_tok, hp, x1, ws_gate, ws_up, ws_down, mods, ln_g, ln_b]
    out_specs = [pl.BlockSpec((tm, D), row)]
    out_shape = [jax.ShapeDtypeStruct((NT, D), F32)]
    if has_next:
        in_specs += [_mod_spec(layer + 1, 0, tm), _mod_spec(layer + 1, 1, tm)]
        args += [mods, mods]
        out_specs.append(pl.BlockSpec((tm, D), row))
        out_shape.append(jax.ShapeDtypeStruct((NT, D), BF16))
    res = pl.pallas_call(
        functools.partial(_combine_kernel, has_next),
        grid=(ntile,),
        in_specs=in_specs,
        out_specs=out_specs,
        out_shape=out_shape,
        scratch_shapes=[pltpu.VMEM((2, TOP_K * tm * Y_P, 128), F32), pltpu.VMEM((tm, D), F32),
                        pltpu.SemaphoreType.DMA((2,))],
        compiler_params=_cparams(("arbitrary",)),
        name="moe_combine",
    )(*args)
    return (res[0], res[1]) if has_next else (res[0], None)


def kernel(x_prompt, x_sample, cache_diff_k, cache_diff_v, cache_na_k, cache_na_v, state_ret, c, c_ctx, ada_w, ada_b, ln_g, ln_b, diff_w_in, diff_w_out, diff_lambda, diff_subln_g, na_w_in, na_w_out, na_rpb, ret_w_in, ret_w_out, ret_decay, moe_router, moe_router_bias, moe_w_gate, moe_w_up, moe_w_down, shared_w_gate, shared_w_up, shared_w_down):
    x = jnp.concatenate([x_prompt.reshape(NP, D), x_sample.reshape(NS, D)], axis=0)
    cond = jnp.concatenate([c_ctx[None], c, jnp.zeros((N_COND - 1 - DEC_BATCH, D), F32)], axis=0)
    mods = modulation_table(cond, ada_w, ada_b)
    ln_g4 = ln_g.reshape(DEPTH, 2, 1, D)
    ln_b4 = ln_b.reshape(DEPTH, 2, 1, D)
    cos, sin = _rope_tables()
    h = modulate(x, mods, 0)
    new_dk, new_dv, new_nk, new_nv, new_rs = [], [], [], [], []
    for i in range(DEPTH):
        j = i // N_MIXERS
        kind = i % N_MIXERS
        if kind == 0:
            w = diff_w_in[j].astype(BF16)
            wq, wk, wv = w[:, :D], w[:, D:2 * D], w[:, 2 * D:]
            qp = matmul(h, wq, 0, NP, BF16)
            kp = matmul(h, wk, 0, NP, F32)
            vp = matmul(h, wv, 0, NP, F32)
            qs = rope(matmul(h, wq, NP, NS, F32), cos, sin)
            ks = rope(matmul(h, wk, NP, NS, F32), cos, sin)
            vs = matmul(h, wv, NP, NS, BF16)
            lam_init = 0.8 - 0.6 * math.exp(-0.3 * i)
            op = diff_attention(qp, kp, vp, diff_lambda[j], diff_subln_g[j], lam_init, BATCH, SEQ, SEQ, SEQ)
            k_all = jnp.concatenate([cache_diff_k[:, j].reshape(DEC_BATCH, PAST, D).astype(BF16),
                                     ks.reshape(DEC_BATCH, DEC_SEQ, D)], axis=1).reshape(-1, D)
            v_all = jnp.concatenate([cache_diff_v[:, j].reshape(DEC_BATCH, PAST, D).astype(BF16),
                                     vs.reshape(DEC_BATCH, DEC_SEQ, D)], axis=1).reshape(-1, D)
            os_ = diff_attention(qs, k_all, v_all, diff_lambda[j], diff_subln_g[j], lam_init,
                                 DEC_BATCH, DEC_SEQ, PAST + DEC_SEQ, 256)
            new_dk.append(kp.reshape(BATCH, SEQ, DIFF_HEADS, 2 * DIFF_QK))
            new_dv.append(vp.reshape(BATCH, SEQ, DIFF_HEADS, DIFF_V))
            w_out = diff_w_out[j]
        elif kind == 1:
            w = na_w_in[j].astype(BF16)
            wq, wk, wv = w[:, :D], w[:, D:2 * D], w[:, 2 * D:]
            qp = matmul(h, wq, 0, NP, BF16)
            kp = matmul(h, wk, 0, NP, F32)
            vp = matmul(h, wv, 0, NP, F32)
            qs = matmul(h, wq, NP, NS, BF16)
            ks = matmul(h, wk, NP, NS, BF16)
            vs = matmul(h, wv, NP, NS, BF16)
            op = na_context_attention(qp, kp, vp)
            os_ = na_latent_attention(qs, ks, vs,
                                      cache_na_k[:, j].reshape(DEC_BATCH, PAST, D).astype(BF16),
                                      cache_na_v[:, j].reshape(DEC_BATCH, PAST, D).astype(BF16),
                                      _na_bias_tables(na_rpb[j]))
            new_nk.append(kp.reshape(BATCH, SEQ, NA_HEADS, NA_DH))
            new_nv.append(vp.reshape(BATCH, SEQ, NA_HEADS, NA_DH))
            w_out = na_w_out[j]
        else:
            w = ret_w_in[j].astype(BF16)
            qd = RET_HEADS * RET_QK
            vd = RET_HEADS * RET_V
            wq, wk, wv, wg = w[:, :qd], w[:, qd:2 * qd], w[:, 2 * qd:2 * qd + vd], w[:, 2 * qd + vd:]
            log_g = -jax.nn.softplus(-ret_decay[j].astype(F32))
            outs = []
            for row0, nrows, nb, l in ((0, NP, BATCH, SEQ), (NP, NS, DEC_BATCH, DEC_SEQ)):
                q_ = matmul(h, wq, row0, nrows, BF16)
                k_ = matmul(h, wk, row0, nrows, F32)
                v_ = matmul(h, wv, row0, nrows, BF16)
                g_ = matmul(h, wg, row0, nrows, F32)
                if row0 == 0:
                    o_, st = retention(q_, k_, v_, g_, log_g, nb, l, None, True)
                    new_rs.append(st)
                else:
                    o_ = retention(q_, k_, v_, g_, log_g, nb, l, state_ret[:, j], False)
                outs.append(o_)
            op, os_ = outs
            w_out = ret_w_out[j]
        o = jnp.concatenate([op, os_], axis=0)
        x1, hp, e_t, w_t, rank_t, cnt = outproj_ln(o, w_out.astype(BF16), x, mods, ln_g4, ln_b4,
                                                   moe_router[i].T, moe_router_bias[i].astype(F32), i)
        dest_tiles, block_e, first, nused, zero_start = _slot_plan(e_t, rank_t, cnt)
        xs = moe_dispatch(hp, dest_tiles, zero_start, nused)
        y_slots = moe_experts(xs, block_e, first, nused, moe_w_gate, moe_w_up, moe_w_down, i)
        x, h = moe_combine(y_slots, dest_tiles, w_t.T, hp, x1,
                           shared_w_gate[i].astype(BF16), shared_w_up[i].astype(BF16),
                           shared_w_down[i].astype(BF16), mods, ln_g4, ln_b4, i)
    return (x[:NP].reshape(BATCH, SEQ, D), x[NP:].reshape(DEC_BATCH, DEC_SEQ, D),
            jnp.stack(new_dk, axis=1), jnp.stack(new_dv, axis=1),
            jnp.stack(new_nk, axis=1), jnp.stack(new_nv, axis=1), jnp.stack(new_rs, axis=1))
```

```python
import functools
import math

import jax
import jax.numpy as jnp
import numpy as np
from jax import lax
from jax.experimental import pallas as pl
from jax.experimental.pallas import tpu as pltpu

F32 = jnp.float32
BF16 = jnp.bfloat16
U32 = jnp.uint32
I32 = jnp.int32

D = 2048
BATCH = 32
SEQ = 256
DEPTH = 4
DEC_BATCH = 8
DEC_SEQ = 1024
PAST = 512
NP = BATCH * SEQ
NS = DEC_BATCH * DEC_SEQ
NT = NP + NS
GRID_W = 64
N_MIXERS = 3
DIFF_HEADS = 8
DIFF_QK = 128
DIFF_V = 256
NA_HEADS = 16
NA_DH = 128
NA_KH = 8
NA_KH_MAX = 8
NA_KW = 16
RET_HEADS = 8
RET_QK = 256
RET_V = 512
RET_CHUNK = 128
N_EXPERTS = 64
N_GROUPS = 8
EXPERTS_PER_GROUP = 8
TOPK_GROUPS = 4
TOP_K = 6
D_EXPERT = 512
D_SHARED = 512
ROUTED_SCALE = 2.5
ROPE_BASE = 10000.0
LN_EPS = 1e-5
ALPHA = (2 * DEPTH) ** 0.25
N_COND = 16

VMEM_LIMIT = 56 * 1024 * 1024

MOE_TB = 256
MOE_NBLK = (NT * TOP_K) // MOE_TB + N_EXPERTS
MOE_CAP = MOE_NBLK * MOE_TB
RT_TM = 512
CMB_TM = 128
PK_S = 8
PK_L = D // 2 // PK_S
Y_S = D // 128
PK_P = PK_S + 1
Y_P = Y_S + 1
HI_MASK = np.uint32(0xFFFF0000)


def _cparams(sem):
    return pltpu.CompilerParams(dimension_semantics=sem, vmem_limit_bytes=VMEM_LIMIT)


def _cond_row(row0):
    return jnp.where(row0 < NP, 0, 1 + (row0 - NP) // DEC_SEQ)


def _mod_spec(layer, chunk, tm, moff=0):
    return pl.BlockSpec((None, None, None, 1, D),
                        lambda m, *_: (layer, _cond_row((m + moff) * tm), chunk, 0, 0))


def _sigmoid(x):
    return 1.0 / (1.0 + jnp.exp(-x))


def _silu(x):
    return x * _sigmoid(x)


def _mod_kernel(cond_ref, w_ref, b_ref, o_ref):
    c = _silu(cond_ref[...]).astype(BF16)
    o_ref[...] = jnp.dot(c, w_ref[...].astype(BF16), preferred_element_type=F32) + b_ref[...]


def modulation_table(cond, ada_w, ada_b):
    tn = 1024
    n6 = 6 * D
    out = pl.pallas_call(
        _mod_kernel,
        grid=(DEPTH, n6 // tn),
        in_specs=[pl.BlockSpec((N_COND, D), lambda l, n: (0, 0)),
                  pl.BlockSpec((None, D, tn), lambda l, n: (l, 0, n)),
                  pl.BlockSpec((None, 1, tn), lambda l, n: (l, 0, n))],
        out_specs=pl.BlockSpec((None, N_COND, tn), lambda l, n: (l, 0, n)),
        out_shape=jax.ShapeDtypeStruct((DEPTH, N_COND, n6), F32),
        compiler_params=_cparams(("arbitrary", "arbitrary")),
        name="modulation_table",
    )(cond, ada_w, ada_b.reshape(DEPTH, 1, n6))
    return out.reshape(DEPTH, N_COND, 6, 1, D)


def _modulate_kernel(x_ref, sh_ref, sc_ref, o_ref):
    o_ref[...] = (x_ref[...] * (1.0 + sc_ref[...]) + sh_ref[...]).astype(o_ref.dtype)


def modulate(x, mods, layer):
    tm = 512
    return pl.pallas_call(
        _modulate_kernel,
        grid=(NT // tm,),
        in_specs=[pl.BlockSpec((tm, D), lambda m: (m, 0)),
                  _mod_spec(layer, 0, tm), _mod_spec(layer, 1, tm)],
        out_specs=pl.BlockSpec((tm, D), lambda m: (m, 0)),
        out_shape=jax.ShapeDtypeStruct((NT, D), BF16),
        compiler_params=_cparams(("arbitrary",)),
        name="modulate",
    )(x, mods, mods)


def _mm_kernel(a_ref, w_ref, o_ref):
    o_ref[...] = jnp.dot(a_ref[...], w_ref[...], preferred_element_type=F32).astype(o_ref.dtype)


def matmul(a, w, row0, nrows, out_dtype):
    k = a.shape[1]
    n = w.shape[1]
    tm, tn = 1024, 1024
    moff = row0 // tm
    return pl.pallas_call(
        _mm_kernel,
        grid=(nrows // tm, n // tn),
        in_specs=[pl.BlockSpec((tm, k), lambda m, j: (m + moff, 0)),
                  pl.BlockSpec((k, tn), lambda m, j: (0, j))],
        out_specs=pl.BlockSpec((tm, tn), lambda m, j: (m, j)),
        out_shape=jax.ShapeDtypeStruct((nrows, n), out_dtype),
        compiler_params=_cparams(("arbitrary", "arbitrary")),
        name="in_proj",
    )(a, w)


def _rope_tables():
    half = DIFF_QK // 2
    t = jnp.arange(DEC_SEQ)
    inv = ROPE_BASE ** (-jnp.arange(0, half, 2, dtype=F32) / half)

    def cs(pos):
        ang = pos.astype(F32)[:, None] * inv[None, :]
        return jnp.cos(ang), jnp.sin(ang)

    cr, sr = cs(t // GRID_W)
    cc, sc = cs(t % GRID_W)
    cos = jnp.concatenate([cr, cr, cc, cc], axis=-1)
    sin = jnp.concatenate([-sr, sr, -sc, sc], axis=-1)
    return cos, sin


def _rope_kernel(x_ref, cos_ref, sin_ref, o_ref):
    cos = cos_ref[...]
    sin = sin_ref[...]
    lane = lax.broadcasted_iota(jnp.int32, cos.shape, 1)
    first = (lane % (DIFF_QK // 2)) < (DIFF_QK // 4)
    for g in range(D // DIFF_QK):
        xg = x_ref[:, g * DIFF_QK:(g + 1) * DIFF_QK]
        sw = jnp.where(first, pltpu.roll(xg, DIFF_QK - DIFF_QK // 4, 1), pltpu.roll(xg, DIFF_QK // 4, 1))
        o_ref[:, g * DIFF_QK:(g + 1) * DIFF_QK] = (xg * cos + sw * sin).astype(o_ref.dtype)


def rope(x, cos, sin):
    tm = 512
    nt = DEC_SEQ // tm
    return pl.pallas_call(
        _rope_kernel,
        grid=(NS // tm,),
        in_specs=[pl.BlockSpec((tm, D), lambda m: (m, 0)),
                  pl.BlockSpec((tm, DIFF_QK), lambda m: (m % nt, 0)),
                  pl.BlockSpec((tm, DIFF_QK), lambda m: (m % nt, 0))],
        out_specs=pl.BlockSpec((tm, D), lambda m: (m, 0)),
        out_shape=jax.ShapeDtypeStruct((NS, D), BF16),
        compiler_params=_cparams(("arbitrary",)),
        name="rope",
    )(x, cos, sin)


def _qkt(q, k):
    return lax.dot_general(q, k, (((1,), (1,)), ((), ())), preferred_element_type=F32)


def _softmax(s):
    m = jnp.max(s, axis=-1, keepdims=True)
    p = jnp.exp(s - m)
    return p / jnp.sum(p, axis=-1, keepdims=True)


def _diff_kernel(lam_init, lamp_ref, g_ref, q_ref, k_ref, v_ref, o_ref):
    lp = lamp_ref[...]
    lam = (jnp.exp(jnp.sum(lp[0:1] * lp[1:2], axis=-1, keepdims=True))
           - jnp.exp(jnp.sum(lp[2:3] * lp[3:4], axis=-1, keepdims=True)) + lam_init)
    scale = DIFF_QK ** -0.5
    g = g_ref[...]
    for h in range(DIFF_HEADS):
        c0 = h * DIFF_V
        q = q_ref[:, c0:c0 + DIFF_V].astype(BF16)
        k = k_ref[:, c0:c0 + DIFF_V].astype(BF16)
        v = v_ref[:, c0:c0 + DIFF_V].astype(BF16)
        a1 = _softmax(_qkt(q[:, :DIFF_QK], k[:, :DIFF_QK]) * scale)
        a2 = _softmax(_qkt(q[:, DIFF_QK:], k[:, DIFF_QK:]) * scale)
        a = (a1 - lam * a2).astype(BF16)
        o = jnp.dot(a, v, preferred_element_type=F32)
        o = o * lax.rsqrt(jnp.mean(jnp.square(o), axis=-1, keepdims=True) + LN_EPS) * g
        o_ref[:, c0:c0 + DIFF_V] = (o * (1.0 - lam_init)).astype(o_ref.dtype)


def diff_attention(q, k, v, lam_p, subln_g, lam_init, nb, lq, lk, tq):
    nq = lq // tq
    return pl.pallas_call(
        functools.partial(_diff_kernel, lam_init),
        grid=(nb, nq),
        in_specs=[pl.BlockSpec((4, DIFF_QK), lambda b, i: (0, 0)),
                  pl.BlockSpec((1, DIFF_V), lambda b, i: (0, 0)),
                  pl.BlockSpec((tq, D), lambda b, i: (b * nq + i, 0)),
                  pl.BlockSpec((lk, D), lambda b, i: (b, 0)),
                  pl.BlockSpec((lk, D), lambda b, i: (b, 0))],
        out_specs=pl.BlockSpec((tq, D), lambda b, i: (b * nq + i, 0)),
        out_shape=jax.ShapeDtypeStruct((nb * lq, D), BF16),
        compiler_params=_cparams(("arbitrary", "arbitrary")),
        name="diff_attention",
    )(lam_p, subln_g.reshape(1, DIFF_V), q, k, v)


def _na_ctx_kernel(q_ref, k_ref, v_ref, o_ref):
    scale = NA_DH ** -0.5
    for h in range(NA_HEADS):
        c0 = h * NA_DH
        q = q_ref[:, c0:c0 + NA_DH].astype(BF16)
        k = k_ref[:, c0:c0 + NA_DH].astype(BF16)
        v = v_ref[:, c0:c0 + NA_DH].astype(BF16)
        p = _softmax(_qkt(q, k) * scale).astype(BF16)
        o_ref[:, c0:c0 + NA_DH] = jnp.dot(p, v, preferred_element_type=F32).astype(o_ref.dtype)


def na_context_attention(q, k, v):
    spec = pl.BlockSpec((SEQ, D), lambda b: (b, 0))
    return pl.pallas_call(
        _na_ctx_kernel,
        grid=(BATCH,),
        in_specs=[spec, spec, spec],
        out_specs=spec,
        out_shape=jax.ShapeDtypeStruct((NP, D), BF16),
        compiler_params=_cparams(("arbitrary",)),
        name="na_context_attention",
    )(q, k, v)


NA_ROWS = DEC_SEQ // GRID_W
NA_NLOC = NA_KH * GRID_W


def _na_window_row(r):
    return jnp.clip(r - NA_KH // 2, 0, NA_ROWS - NA_KH)


def _na_bias_tables(rpb):
    col = jnp.arange(GRID_W)
    col_start = jnp.clip(col - NA_KW // 2, 0, GRID_W - NA_KW)
    col_ok = (col[None, :] >= col_start[:, None]) & (col[None, :] < col_start[:, None] + NA_KW)
    dc_idx = jnp.clip(col[None, :] - col[:, None] + NA_KW - 1, 0, 2 * NA_KW - 2)
    hot = (dc_idx[:, :, None] == jnp.arange(2 * NA_KW - 1)[None, None, :]).astype(F32)
    full = jnp.einsum('hrd,qkd->hqrk', rpb.astype(F32), hot, precision=lax.Precision.HIGHEST)
    full = jnp.where(col_ok[None, :, None, :], full, -jnp.inf)
    full = full.reshape(NA_HEADS, GRID_W, (2 * NA_KH_MAX - 1) * GRID_W)
    tabs = [full[:, :, (NA_KH_MAX - 1 - o) * GRID_W:(NA_KH_MAX - 1 - o) * GRID_W + NA_NLOC]
            for o in range(NA_KH)]
    return jnp.stack(tabs)


def _na_lat_kernel(q_ref, k_ref, v_ref, kc_ref, vc_ref, bias_ref, o_ref, pctx_ref, oloc_ref):
    scale = NA_DH ** -0.5
    s_ctx_all = _qkt(q_ref[...], kc_ref[...]) * scale
    for r in range(NA_ROWS):
        r0 = min(max(r - NA_KH // 2, 0), NA_ROWS - NA_KH)
        rows = slice(r * GRID_W, (r + 1) * GRID_W)
        win = slice(r0 * GRID_W, r0 * GRID_W + NA_NLOC)
        s_loc = _qkt(q_ref[rows, :], k_ref[win, :]) * scale + bias_ref[r - r0]
        s_ctx = s_ctx_all[rows, :]
        m = jnp.maximum(jnp.max(s_loc, axis=-1, keepdims=True), jnp.max(s_ctx, axis=-1, keepdims=True))
        p_loc = jnp.exp(s_loc - m)
        p_ctx = jnp.exp(s_ctx - m)
        den = jnp.sum(p_loc, axis=-1, keepdims=True) + jnp.sum(p_ctx, axis=-1, keepdims=True)
        oloc_ref[rows, :] = jnp.dot((p_loc / den).astype(BF16), v_ref[win, :], preferred_element_type=F32)
        pctx_ref[rows, :] = (p_ctx / den).astype(BF16)
    o = oloc_ref[...] + jnp.dot(pctx_ref[...], vc_ref[...], preferred_element_type=F32)
    o_ref[...] = o.astype(o_ref.dtype)


def na_latent_attention(q, k, v, k_ctx, v_ctx, bias_tabs):
    head = pl.BlockSpec((DEC_SEQ, NA_DH), lambda b, h: (b, h))
    ctx = pl.BlockSpec((None, PAST, NA_DH), lambda b, h: (b, 0, h))
    return pl.pallas_call(
        _na_lat_kernel,
        grid=(DEC_BATCH, NA_HEADS),
        in_specs=[head, head, head, ctx, ctx,
                  pl.BlockSpec((NA_KH, None, GRID_W, NA_NLOC), lambda b, h: (0, h, 0, 0))],
        out_specs=head,
        out_shape=jax.ShapeDtypeStruct((NS, D), BF16),
        scratch_shapes=[pltpu.VMEM((DEC_SEQ, PAST), BF16), pltpu.VMEM((DEC_SEQ, NA_DH), F32)],
        compiler_params=_cparams(("arbitrary", "arbitrary")),
        name="na_latent_attention",
    )(q, k, v, k_ctx, v_ctx, bias_tabs)


def _ret_kernel(nc, has_state, emit_state, logg_ref, q_ref, k_ref, v_ref, g_ref, *rest):
    rest = list(rest)
    s0_ref = rest.pop(0) if has_state else None
    o_ref = rest.pop(0)
    st_ref = rest.pop(0) if emit_state else None
    oacc, sacc = rest
    h = pl.program_id(1)
    lgf = logg_ref[0, h]
    lgb = logg_ref[1, h]
    c = RET_CHUNK
    pos = lax.broadcasted_iota(jnp.int32, (c, 1), 0).astype(F32)
    diff = (lax.broadcasted_iota(jnp.int32, (c, c), 0) - lax.broadcasted_iota(jnp.int32, (c, c), 1)).astype(F32)
    dmask_f = jnp.where(diff >= 0, jnp.exp(jnp.maximum(diff, 0.0) * lgf), 0.0)
    dmask_b = jnp.where(diff <= 0, jnp.exp(jnp.maximum(-diff, 0.0) * lgb), 0.0)
    qdec_f = jnp.exp((pos + 1.0) * lgf)
    kdec_f = jnp.exp((c - 1.0 - pos) * lgf)
    qdec_b = jnp.exp((c - pos) * lgb)
    kdec_b = jnp.exp(pos * lgb)
    cd_f = jnp.exp(jnp.full((1, 1), c, F32) * lgf)
    cd_b = jnp.exp(jnp.full((1, 1), c, F32) * lgb)

    def chunk(i):
        sl = slice(i * c, (i + 1) * c)
        return q_ref[sl, :].astype(BF16), k_ref[sl, :] * (RET_QK ** -0.5), v_ref[sl, :].astype(BF16)

    def state_update(kc, kdec, vb, cd):
        kd = (kc * kdec).T.astype(BF16)
        sacc[...] = sacc[...] * cd + jnp.dot(kd, vb, preferred_element_type=F32)

    if has_state:
        sacc[...] = s0_ref[0]
    else:
        sacc[...] = jnp.zeros_like(sacc)
    for i in range(nc):
        qb, kc, vb = chunk(i)
        qk = _qkt(qb, kc.astype(BF16))
        o = (jnp.dot((qk * dmask_f).astype(BF16), vb, preferred_element_type=F32)
             + jnp.dot((qk * dmask_b).astype(BF16), vb, preferred_element_type=F32)
             + jnp.dot(qb, sacc[...].astype(BF16), preferred_element_type=F32) * qdec_f)
        oacc[i * c:(i + 1) * c, :] = o
        state_update(kc, kdec_f, vb, cd_f)
    if emit_state:
        st_ref[0] = sacc[...]

    if has_state:
        sacc[...] = s0_ref[1]
    else:
        sacc[...] = jnp.zeros_like(sacc)
    for i in reversed(range(nc)):
        qb, kc, vb = chunk(i)
        oacc[i * c:(i + 1) * c, :] += jnp.dot(qb, sacc[...].astype(BF16), preferred_element_type=F32) * qdec_b
        state_update(kc, kdec_b, vb, cd_b)
    if emit_state:
        st_ref[1] = sacc[...]

    o = oacc[...]
    mu = jnp.mean(o, axis=-1, keepdims=True)
    var = jnp.mean(jnp.square(o - mu), axis=-1, keepdims=True)
    o = (o - mu) * lax.rsqrt(var + LN_EPS)
    o_ref[...] = (o * _silu(g_ref[...])).astype(o_ref.dtype)


def retention(q, k, v, g, log_g, nb, l, state=None, emit_state=False):
    nc = l // RET_CHUNK
    has_state = state is not None
    in_specs = [pl.BlockSpec(memory_space=pltpu.SMEM),
                pl.BlockSpec((l, RET_QK), lambda b, h: (b, h)),
                pl.BlockSpec((l, RET_QK), lambda b, h: (b, h)),
                pl.BlockSpec((l, RET_V), lambda b, h: (b, h)),
                pl.BlockSpec((l, RET_V), lambda b, h: (b, h))]
    args = [log_g, q, k, v, g]
    st_spec = pl.BlockSpec((None, 2, None, RET_QK, RET_V), lambda b, h: (b, 0, h, 0, 0))
    if has_state:
        in_specs.append(st_spec)
        args.append(state)
    out_specs = [pl.BlockSpec((l, RET_V), lambda b, h: (b, h))]
    out_shape = [jax.ShapeDtypeStruct((nb * l, RET_HEADS * RET_V), BF16)]
    if emit_state:
        out_specs.append(st_spec)
        out_shape.append(jax.ShapeDtypeStruct((nb, 2, RET_HEADS, RET_QK, RET_V), F32))
    res = pl.pallas_call(
        functools.partial(_ret_kernel, nc, has_state, emit_state),
        grid=(nb, RET_HEADS),
        in_specs=in_specs,
        out_specs=out_specs,
        out_shape=out_shape,
        scratch_shapes=[pltpu.VMEM((l, RET_V), F32), pltpu.VMEM((RET_QK, RET_V), F32)],
        compiler_params=_cparams(("arbitrary", "arbitrary")),
        name="retention",
    )(*args)
    return res if emit_state else res[0]


def _layer_norm(z, g, b):
    mu = jnp.mean(z, axis=-1, keepdims=True)
    var = jnp.mean(jnp.square(z - mu), axis=-1, keepdims=True)
    return (z - mu) * lax.rsqrt(var + LN_EPS) * g + b


def _pack_rows(hn, hp_ref):
    tm = hn.shape[0]
    half = D // 2
    lo = lax.bitcast_convert_type(hn[:, :half].astype(BF16).astype(F32), U32)
    hi = lax.bitcast_convert_type(hn[:, half:].astype(BF16).astype(F32), U32)
    w = (lo >> 16) | (hi & HI_MASK)
    for s in range(PK_S):
        hp_ref[pl.ds(s, tm, stride=PK_P), :] = w[:, s * PK_L:(s + 1) * PK_L]
    hp_ref[pl.ds(PK_S, tm, stride=PK_P), :] = jnp.zeros((tm, PK_L), U32)


def _unpack_rows(hp_ref, tm):
    lo, hi = [], []
    for s in range(PK_S):
        w = hp_ref[pl.ds(s, tm, stride=PK_P), :]
        lo.append(lax.bitcast_convert_type(w << 16, F32).astype(BF16))
        hi.append(lax.bitcast_convert_type(w & HI_MASK, F32).astype(BF16))
    return jnp.concatenate(lo + hi, axis=1)


def _route_tile(hn, wrt_ref, rb_ref, e_ref, w_ref, rank_ref, cnt_ref):
    tm = hn.shape[0]
    neg = -jnp.inf
    logits = lax.dot_general(wrt_ref[...], hn, (((1,), (1,)), ((), ())),
                             preferred_element_type=F32, precision=lax.Precision.HIGHEST)
    s = _sigmoid(logits)
    biased = s + rb_ref[...]
    io8 = lax.broadcasted_iota(I32, (EXPERTS_PER_GROUP, tm), 0).astype(F32)
    slabs, gscore = [], []
    for g in range(N_GROUPS):
        slab = biased[g * EXPERTS_PER_GROUP:(g + 1) * EXPERTS_PER_GROUP, :]
        m1 = jnp.max(slab, axis=0, keepdims=True)
        i1 = jnp.min(jnp.where(slab == m1, io8, float(EXPERTS_PER_GROUP)), axis=0, keepdims=True)
        m2 = jnp.max(jnp.where(io8 == i1, neg, slab), axis=0, keepdims=True)
        slabs.append(slab)
        gscore.append(m1 + m2)
    masked = []
    for g in range(N_GROUPS):
        ahead = jnp.zeros_like(gscore[g])
        for g2 in range(N_GROUPS):
            if g2 == g:
                continue
            better = (gscore[g2] > gscore[g]) | ((gscore[g2] == gscore[g]) if g2 < g else False)
            ahead = ahead + jnp.where(better, 1.0, 0.0)
        masked.append(jnp.where(ahead < float(TOPK_GROUPS), slabs[g], neg))
    v = jnp.concatenate(masked, axis=0)
    io = lax.broadcasted_iota(I32, (N_EXPERTS, tm), 0).astype(F32)
    idxs, ws, hots = [], [], []
    for _ in range(TOP_K):
        m = jnp.max(v, axis=0, keepdims=True)
        idx = jnp.min(jnp.where(v == m, io, float(N_EXPERTS)), axis=0, keepdims=True)
        hot = io == idx
        ws.append(jnp.sum(jnp.where(hot, s, 0.0), axis=0, keepdims=True))
        v = jnp.where(hot, neg, v)
        idxs.append(idx)
        hots.append(hot)
    wsum = ws[0]
    for k in range(1, TOP_K):
        wsum = wsum + ws[k]
    chosen = jnp.where(hots[0], 1.0, 0.0)
    for k in range(1, TOP_K):
        chosen = chosen + jnp.where(hots[k], 1.0, 0.0)
    upper = jnp.where(lax.broadcasted_iota(I32, (tm, tm), 0) < lax.broadcasted_iota(I32, (tm, tm), 1), 1.0, 0.0)
    before = jnp.dot(chosen.astype(BF16), upper.astype(BF16), preferred_element_type=F32)
    zero_row = jnp.zeros((1, tm), F32)
    for k in range(8):
        if k < TOP_K:
            e_ref[k:k + 1, :] = idxs[k].astype(I32)
            w_ref[k:k + 1, :] = ws[k] / wsum * ROUTED_SCALE
            rank_ref[k:k + 1, :] = jnp.sum(jnp.where(hots[k], before, 0.0), axis=0, keepdims=True).astype(I32)
        else:
            e_ref[k:k + 1, :] = zero_row.astype(I32)
            w_ref[k:k + 1, :] = zero_row
            rank_ref[k:k + 1, :] = zero_row.astype(I32)
    cnt = jnp.sum(chosen, axis=1, keepdims=True)
    cnt_ref[...] = jnp.broadcast_to(cnt, (N_EXPERTS, 128)).astype(I32)


def _outproj_kernel(nk, o_ref, w_ref, x_ref, gate_ref, lng_ref, lnb_ref, sh_ref, sc_ref, wrt_ref, rb_ref,
                    xo_ref, hp_ref, e_ref, wt_ref, rank_ref, cnt_ref, acc_ref):
    kk = pl.program_id(1)
    part = jnp.dot(o_ref[...], w_ref[...], preferred_element_type=F32)

    @pl.when(kk == 0)
    def _():
        acc_ref[...] = part

    @pl.when(kk > 0)
    def _():
        acc_ref[...] += part

    @pl.when(kk == nk - 1)
    def _():
        xn = _layer_norm(ALPHA * x_ref[...] + gate_ref[...] * acc_ref[...], lng_ref[...], lnb_ref[...])
        xo_ref[...] = xn
        hn = xn * (1.0 + sc_ref[...]) + sh_ref[...]
        _pack_rows(hn, hp_ref)
        _route_tile(hn, wrt_ref, rb_ref, e_ref, wt_ref, rank_ref, cnt_ref)


def outproj_ln(o, w_out, x, mods, ln_g, ln_b, w_router_t, b_router, layer):
    k = o.shape[1]
    tk = 2048 if k == 2048 else 1024
    nk = k // tk
    tm = RT_TM
    ntile = NT // tm
    row = lambda m, kk: (m, 0)
    col = lambda m, kk: (0, m)
    return pl.pallas_call(
        functools.partial(_outproj_kernel, nk),
        grid=(ntile, nk),
        in_specs=[pl.BlockSpec((tm, tk), lambda m, kk: (m, kk)),
                  pl.BlockSpec((tk, D), lambda m, kk: (kk, 0),
                               pipeline_mode=pl.Buffered(1) if nk == 1 else None),
                  pl.BlockSpec((tm, D), row),
                  _mod_spec(layer, 2, tm),
                  pl.BlockSpec((None, None, 1, D), lambda m, kk: (layer, 0, 0, 0)),
                  pl.BlockSpec((None, None, 1, D), lambda m, kk: (layer, 0, 0, 0)),
                  _mod_spec(layer, 3, tm), _mod_spec(layer, 4, tm),
                  pl.BlockSpec((N_EXPERTS, D), lambda m, kk: (0, 0)),
                  pl.BlockSpec((N_EXPERTS, 1), lambda m, kk: (0, 0))],
        out_specs=[pl.BlockSpec((tm, D), row),
                   pl.BlockSpec((tm * PK_P, PK_L), row),
                   pl.BlockSpec((8, tm), col), pl.BlockSpec((8, tm), col), pl.BlockSpec((8, tm), col),
                   pl.BlockSpec((None, N_EXPERTS, 128), lambda m, kk: (m, 0, 0))],
        out_shape=[jax.ShapeDtypeStruct((NT, D), F32),
                   jax.ShapeDtypeStruct((NT * PK_P, PK_L), U32),
                   jax.ShapeDtypeStruct((8, NT), I32), jax.ShapeDtypeStruct((8, NT), F32),
                   jax.ShapeDtypeStruct((8, NT), I32),
                   jax.ShapeDtypeStruct((ntile, N_EXPERTS, 128), I32)],
        scratch_shapes=[pltpu.VMEM((tm, D), F32)],
        compiler_params=_cparams(("arbitrary", "arbitrary")),
        name="outproj_ln_route",
    )(o, w_out, x, mods, ln_g, ln_b, mods, mods, w_router_t, b_router.reshape(N_EXPERTS, 1))


def _slot_plan(e_t, rank_t, cnt):
    cnt = cnt[:, :, 0]
    tile_base = jnp.cumsum(cnt, axis=0) - cnt
    counts = jnp.sum(cnt, axis=0)
    padded = (counts + MOE_TB - 1) // MOE_TB * MOE_TB
    pad_end = jnp.cumsum(padded)
    pad_start = pad_end - padded
    offs = jnp.repeat(pad_start[None, :] + tile_base, NT // cnt.shape[0], axis=0)
    hot = e_t[:TOP_K, :, None] == jnp.arange(N_EXPERTS, dtype=I32)[None, None, :]
    dest = jnp.sum(jnp.where(hot, offs[None], 0), axis=-1) + rank_t[:TOP_K]
    ntile = NT // CMB_TM
    dest_tiles = dest.reshape(TOP_K, ntile, CMB_TM).transpose(1, 0, 2).reshape(ntile, 1, TOP_K * CMB_TM)
    nused = pad_end[-1] // MOE_TB
    blk = jnp.minimum(jnp.arange(MOE_NBLK, dtype=I32), nused - 1) * MOE_TB
    block_e = jnp.minimum(jnp.sum((pad_end[None, :] <= blk[:, None]).astype(I32), axis=1), N_EXPERTS - 1)
    first = jnp.concatenate([jnp.ones((1,), I32), (block_e[1:] != block_e[:-1]).astype(I32)])
    zero_start = pad_start + counts
    ids = jnp.arange(N_EXPERTS, dtype=I32)
    used = counts > 0
    later = jnp.where(used[None, :] & (ids[None, :] > ids[:, None]), ids[None, :], N_EXPERTS)
    nxt = jnp.min(later, axis=1)
    nxt = jnp.where(nxt == N_EXPERTS, ids, nxt)
    par = (jnp.cumsum(used.astype(I32)) - 1) % 2
    owner = block_e[:, None] == ids[None, :]
    nxt_b = jnp.sum(jnp.where(owner, nxt[None, :], 0), axis=1)
    par_b = jnp.sum(jnp.where(owner, par[None, :], 0), axis=1)
    return (dest_tiles.astype(I32), block_e.astype(I32), first, nused.reshape(1).astype(I32),
            zero_start.astype(I32), nxt_b.astype(I32), par_b.astype(I32))


_DISPATCH_ROWS = TOP_K * CMB_TM
_CMB_ROWS = 32


def _dispatch_kernel(zs_ref, nused_ref, dest_ref, hp_ref, xs_hbm, stage, zbuf, zsem, sem):
    i = pl.program_id(0)
    n = pl.num_programs(0)
    slot = i % 2

    @pl.when(i == 0)
    def _():
        zbuf[...] = jnp.zeros_like(zbuf)

        def zcopy(slot0, nslot):
            return pltpu.make_async_copy(zbuf.at[pl.ds(0, nslot * PK_P)],
                                         xs_hbm.at[pl.ds(slot0 * PK_P, nslot * PK_P)], zsem)

        def pad_pieces(e, wait):
            slot0 = zs_ref[e]
            npad = (MOE_TB - slot0 % MOE_TB) % MOE_TB
            piece = MOE_TB // 2
            while piece >= 1:
                @pl.when((npad & piece) != 0)
                def _(slot0=slot0, piece=piece):
                    cp = zcopy(slot0, piece)
                    cp.wait() if wait else cp.start()
                slot0 = slot0 + (npad & piece)
                piece //= 2

        def zstart(e, carry):
            pad_pieces(e, False)
            return carry

        def zwait(e, carry):
            pad_pieces(e, True)
            return carry

        def tstart(b, carry):
            zcopy(b * MOE_TB, MOE_TB).start()
            return carry

        def twait(b, carry):
            zcopy(b * MOE_TB, MOE_TB).wait()
            return carry

        lax.fori_loop(0, N_EXPERTS, zstart, 0)
        lax.fori_loop(nused_ref[0], MOE_NBLK + 1, tstart, 0)
        lax.fori_loop(0, N_EXPERTS, zwait, 0)
        lax.fori_loop(nused_ref[0], MOE_NBLK + 1, twait, 0)

    stage[slot] = hp_ref[...]

    def issue(r, carry):
        for k in range(TOP_K):
            dst = dest_ref[0, 0, k * CMB_TM + r] * PK_P
            pltpu.make_async_copy(stage.at[slot, pl.ds(r * PK_P, PK_P)], xs_hbm.at[pl.ds(dst, PK_P)],
                                  sem.at[slot]).start(priority=k % 2)
        return carry

    lax.fori_loop(0, CMB_TM, issue, 0, unroll=2)

    def drain(s):
        for k in range(TOP_K):
            pltpu.make_async_copy(stage.at[s], xs_hbm.at[pl.ds(0, CMB_TM * PK_P)], sem.at[s]).wait()

    @pl.when(i > 0)
    def _():
        drain(1 - slot)

    @pl.when(i == n - 1)
    def _():
        drain(slot)


def moe_dispatch(hp, dest_tiles, zero_start, nused):
    ntile = NT // CMB_TM
    grid_spec = pltpu.PrefetchScalarGridSpec(
        num_scalar_prefetch=2,
        grid=(ntile,),
        in_specs=[pl.BlockSpec((1, 1, _DISPATCH_ROWS), lambda i, zs, nu: (i, 0, 0), memory_space=pltpu.SMEM),
                  pl.BlockSpec((CMB_TM * PK_P, PK_L), lambda i, zs, nu: (i, 0))],
        out_specs=pl.BlockSpec(memory_space=pl.ANY),
        scratch_shapes=[pltpu.VMEM((2, CMB_TM * PK_P, PK_L), U32),
                        pltpu.VMEM((MOE_TB * PK_P, PK_L), U32),
                        pltpu.SemaphoreType.DMA(()), pltpu.SemaphoreType.DMA((2,))],
    )
    return pl.pallas_call(
        _dispatch_kernel,
        grid_spec=grid_spec,
        out_shape=jax.ShapeDtypeStruct(((MOE_CAP + MOE_TB) * PK_P, PK_L), U32),
        compiler_params=_cparams(("arbitrary",)),
        name="moe_dispatch",
    )(zero_start, nused, dest_tiles, hp)


def _moe_kernel(layer, be_ref, first_ref, nused_ref, nxt_ref, par_ref, xs_ref, wg_hbm, wu_hbm, wd_hbm, y_ref,
                wgs, wus, wds, wgb, wub, wdb, sem):
    i = pl.program_id(0)

    def weight_copies(e, s):
        return (pltpu.make_async_copy(wg_hbm.at[layer, e], wgs.at[s], sem.at[s]),
                pltpu.make_async_copy(wu_hbm.at[layer, e], wus.at[s], sem.at[s]),
                pltpu.make_async_copy(wd_hbm.at[layer, e], wds.at[s], sem.at[s]))

    @pl.when(i == 0)
    def _():
        for cp in weight_copies(be_ref[0], par_ref[0]):
            cp.start()

    @pl.when(first_ref[i] == 1)
    def _():
        e, s = be_ref[i], par_ref[i]
        for cp in weight_copies(e, s):
            cp.wait()

        @pl.when(nxt_ref[i] != e)
        def _():
            for cp in weight_copies(nxt_ref[i], 1 - s):
                cp.start()

        wgb[...] = wgs[s].astype(BF16)
        wub[...] = wus[s].astype(BF16)
        wdb[...] = wds[s].astype(BF16)

    @pl.when(i < nused_ref[0])
    def _():
        x = _unpack_rows(xs_ref, MOE_TB)
        a = (_silu(jnp.dot(x, wgb[...], preferred_element_type=F32))
             * jnp.dot(x, wub[...], preferred_element_type=F32))
        y = jnp.dot(a.astype(BF16), wdb[...], preferred_element_type=F32)
        for j in range(Y_S):
            y_ref[pl.ds(j, MOE_TB, stride=Y_P), :] = y[:, j * 128:(j + 1) * 128]
        y_ref[pl.ds(Y_S, MOE_TB, stride=Y_P), :] = jnp.zeros((MOE_TB, 128), F32)

    @pl.when(i >= nused_ref[0])
    def _():
        y_ref[...] = jnp.zeros_like(y_ref)


def moe_experts(xs, block_e, first, nused, nxt_e, par, w_gate, w_up, w_down, layer):
    blk = lambda i, be, fi, nu, nx, pa: (jnp.minimum(i, nu[0] - 1), 0)
    grid_spec = pltpu.PrefetchScalarGridSpec(
        num_scalar_prefetch=5,
        grid=(MOE_NBLK,),
        in_specs=[pl.BlockSpec((MOE_TB * PK_P, PK_L), blk),
                  pl.BlockSpec(memory_space=pl.ANY), pl.BlockSpec(memory_space=pl.ANY),
                  pl.BlockSpec(memory_space=pl.ANY)],
        out_specs=pl.BlockSpec((MOE_TB * Y_P, 128), lambda i, be, fi, nu, nx, pa: (i, 0)),
        scratch_shapes=[pltpu.VMEM((2, D, D_EXPERT), F32), pltpu.VMEM((2, D, D_EXPERT), F32),
                        pltpu.VMEM((2, D_EXPERT, D), F32),
                        pltpu.VMEM((D, D_EXPERT), BF16), pltpu.VMEM((D, D_EXPERT), BF16),
                        pltpu.VMEM((D_EXPERT, D), BF16),
                        pltpu.SemaphoreType.DMA((2,))],
    )
    return pl.pallas_call(
        functools.partial(_moe_kernel, layer),
        grid_spec=grid_spec,
        out_shape=jax.ShapeDtypeStruct((MOE_CAP * Y_P, 128), F32),
        compiler_params=_cparams(("arbitrary",)),
        name="moe_experts",
    )(block_e, first, nused, nxt_e, par, xs, w_gate, w_up, w_down)


def _combine_kernel(has_next, dest_ref, destn_ref, y_hbm, w_ref, hp_ref, x_ref, wsg_ref, wsu_ref, wsd_ref,
                    gate_ref, lng_ref, lnb_ref, *rest):
    if has_next:
        sh_ref, sc_ref, xo_ref, ho_ref, buf, routed_ref, sem = rest
    else:
        xo_ref, buf, routed_ref, sem = rest
    i = pl.program_id(0)
    n = pl.num_programs(0)
    slot = i % 2
    nrow = TOP_K * CMB_TM

    def gather(dref, s):
        def body(r, carry):
            for k in range(TOP_K):
                src = dref[0, 0, k * CMB_TM + r] * Y_P
                dst = (k * CMB_TM + r) * Y_P
                pltpu.make_async_copy(y_hbm.at[pl.ds(src, Y_S)], buf.at[s, pl.ds(dst, Y_S)],
                                      sem.at[s]).start(priority=k % 2)
            return carry
        lax.fori_loop(0, CMB_TM, body, 0, unroll=2)

    @pl.when(i == 0)
    def _():
        gather(dest_ref, 0)

    @pl.when(i + 1 < n)
    def _():
        gather(destn_ref, 1 - slot)

    hb = _unpack_rows(hp_ref, CMB_TM)
    a = (_silu(jnp.dot(hb, wsg_ref[...], preferred_element_type=F32))
         * jnp.dot(hb, wsu_ref[...], preferred_element_type=F32))
    shared = jnp.dot(a.astype(BF16), wsd_ref[...], preferred_element_type=F32)

    pltpu.make_async_copy(y_hbm.at[pl.ds(0, nrow * Y_S)], buf.at[slot, pl.ds(0, nrow * Y_S)],
                          sem.at[slot]).wait()
    for r0 in range(0, CMB_TM, _CMB_ROWS):
        wsub = w_ref[r0:r0 + _CMB_ROWS, :]
        wb = [jnp.broadcast_to(wsub[:, k:k + 1], (_CMB_ROWS, 128)) for k in range(TOP_K)]
        for j in range(Y_S):
            acc = buf[slot, pl.ds(r0 * Y_P + j, _CMB_ROWS, stride=Y_P), :] * wb[0]
            for k in range(1, TOP_K):
                acc = acc + buf[slot, pl.ds((k * CMB_TM + r0) * Y_P + j, _CMB_ROWS, stride=Y_P), :] * wb[k]
            routed_ref[r0:r0 + _CMB_ROWS, j * 128:(j + 1) * 128] = acc
    routed = routed_ref[...]
    xn = _layer_norm(ALPHA * x_ref[...] + gate_ref[...] * (routed + shared), lng_ref[...], lnb_ref[...])
    xo_ref[...] = xn
    if has_next:
        ho_ref[...] = (xn * (1.0 + sc_ref[...]) + sh_ref[...]).astype(ho_ref.dtype)


def moe_combine(y_slots, dest_tiles, w_tok, hp, x1, ws_gate, ws_up, ws_down, mods, ln_g, ln_b, layer):
    has_next = layer + 1 < DEPTH
    tm = CMB_TM
    ntile = NT // tm
    row = lambda m: (m, 0)
    const = lambda m: (0, 0)
    in_specs = [pl.BlockSpec((1, 1, TOP_K * tm), lambda m: (m, 0, 0), memory_space=pltpu.SMEM),
                pl.BlockSpec((1, 1, TOP_K * tm), lambda m: (jnp.minimum(m + 1, ntile - 1), 0, 0),
                             memory_space=pltpu.SMEM),
                pl.BlockSpec(memory_space=pl.ANY),
                pl.BlockSpec((tm, 8), row),
                pl.BlockSpec((tm * PK_P, PK_L), row), pl.BlockSpec((tm, D), row),
                pl.BlockSpec((D, D_SHARED), const), pl.BlockSpec((D, D_SHARED), const),
                pl.BlockSpec((D_SHARED, D), const),
                _mod_spec(layer, 5, tm),
                pl.BlockSpec((None, None, 1, D), lambda m: (layer, 1, 0, 0)),
                pl.BlockSpec((None, None, 1, D), lambda m: (layer, 1, 0, 0))]
    args = [dest_tiles, dest_tiles, y_slots, w_tok, hp, x1, ws_gate, ws_up, ws_down, mods, ln_g, ln_b]
    out_specs = [pl.BlockSpec((tm, D), row)]
    out_shape = [jax.ShapeDtypeStruct((NT, D), F32)]
    if has_next:
        in_specs += [_mod_spec(layer + 1, 0, tm), _mod_spec(layer + 1, 1, tm)]
        args += [mods, mods]
        out_specs.append(pl.BlockSpec((tm, D), row))
        out_shape.append(jax.ShapeDtypeStruct((NT, D), BF16))
    res = pl.pallas_call(
        functools.partial(_combine_kernel, has_next),
        grid=(ntile,),
        in_specs=in_specs,
        out_specs=out_specs,
        out_shape=out_shape,
        scratch_shapes=[pltpu.VMEM((2, TOP_K * tm * Y_P, 128), F32), pltpu.VMEM((tm, D), F32),
                        pltpu.SemaphoreType.DMA((2,))],
        compiler_params=_cparams(("arbitrary",)),
        name="moe_combine",
    )(*args)
    return (res[0], res[1]) if has_next else (res[0], None)


def kernel(x_prompt, x_sample, cache_diff_k, cache_diff_v, cache_na_k, cache_na_v, state_ret, c, c_ctx, ada_w, ada_b, ln_g, ln_b, diff_w_in, diff_w_out, diff_lambda, diff_subln_g, na_w_in, na_w_out, na_rpb, ret_w_in, ret_w_out, ret_decay, moe_router, moe_router_bias, moe_w_gate, moe_w_up, moe_w_down, shared_w_gate, shared_w_up, shared_w_down):
    x = jnp.concatenate([x_prompt.reshape(NP, D), x_sample.reshape(NS, D)], axis=0)
    cond = jnp.concatenate([c_ctx[None], c, jnp.zeros((N_COND - 1 - DEC_BATCH, D), F32)], axis=0)
    mods = modulation_table(cond, ada_w, ada_b)
    ln_g4 = ln_g.reshape(DEPTH, 2, 1, D)
    ln_b4 = ln_b.reshape(DEPTH, 2, 1, D)
    cos, sin = _rope_tables()
    h = modulate(x, mods, 0)
    new_dk, new_dv, new_nk, new_nv, new_rs = [], [], [], [], []
    for i in range(DEPTH):
        j = i // N_MIXERS
        kind = i % N_MIXERS
        if kind == 0:
            w = diff_w_in[j].astype(BF16)
            wq, wk, wv = w[:, :D], w[:, D:2 * D], w[:, 2 * D:]
            qp = matmul(h, wq, 0, NP, BF16)
            kp = matmul(h, wk, 0, NP, F32)
            vp = matmul(h, wv, 0, NP, F32)
            qs = rope(matmul(h, wq, NP, NS, F32), cos, sin)
            ks = rope(matmul(h, wk, NP, NS, F32), cos, sin)
            vs = matmul(h, wv, NP, NS, BF16)
            lam_init = 0.8 - 0.6 * math.exp(-0.3 * i)
            op = diff_attention(qp, kp, vp, diff_lambda[j], diff_subln_g[j], lam_init, BATCH, SEQ, SEQ, SEQ)
            k_all = jnp.concatenate([cache_diff_k[:, j].reshape(DEC_BATCH, PAST, D).astype(BF16),
                                     ks.reshape(DEC_BATCH, DEC_SEQ, D)], axis=1).reshape(-1, D)
            v_all = jnp.concatenate([cache_diff_v[:, j].reshape(DEC_BATCH, PAST, D).astype(BF16),
                                     vs.reshape(DEC_BATCH, DEC_SEQ, D)], axis=1).reshape(-1, D)
            os_ = diff_attention(qs, k_all, v_all, diff_lambda[j], diff_subln_g[j], lam_init,
                                 DEC_BATCH, DEC_SEQ, PAST + DEC_SEQ, 256)
            new_dk.append(kp.reshape(BATCH, SEQ, DIFF_HEADS, 2 * DIFF_QK))
            new_dv.append(vp.reshape(BATCH, SEQ, DIFF_HEADS, DIFF_V))
            w_out = diff_w_out[j]
        elif kind == 1:
            w = na_w_in[j].astype(BF16)
            wq, wk, wv = w[:, :D], w[:, D:2 * D], w[:, 2 * D:]
            qp = matmul(h, wq, 0, NP, BF16)
            kp = matmul(h, wk, 0, NP, F32)
            vp = matmul(h, wv, 0, NP, F32)
            qs = matmul(h, wq, NP, NS, BF16)
            ks = matmul(h, wk, NP, NS, BF16)
            vs = matmul(h, wv, NP, NS, BF16)
            op = na_context_attention(qp, kp, vp)
            os_ = na_latent_attention(qs, ks, vs,
                                      cache_na_k[:, j].reshape(DEC_BATCH, PAST, D).astype(BF16),
                                      cache_na_v[:, j].reshape(DEC_BATCH, PAST, D).astype(BF16),
                                      _na_bias_tables(na_rpb[j]))
            new_nk.append(kp.reshape(BATCH, SEQ, NA_HEADS, NA_DH))
            new_nv.append(vp.reshape(BATCH, SEQ, NA_HEADS, NA_DH))
            w_out = na_w_out[j]
        else:
            w = ret_w_in[j].astype(BF16)
            qd = RET_HEADS * RET_QK
            vd = RET_HEADS * RET_V
            wq, wk, wv, wg = w[:, :qd], w[:, qd:2 * qd], w[:, 2 * qd:2 * qd + vd], w[:, 2 * qd + vd:]
            log_g = -jax.nn.softplus(-ret_decay[j].astype(F32))
            outs = []
            for row0, nrows, nb, l in ((0, NP, BATCH, SEQ), (NP, NS, DEC_BATCH, DEC_SEQ)):
                q_ = matmul(h, wq, row0, nrows, BF16)
                k_ = matmul(h, wk, row0, nrows, F32)
                v_ = matmul(h, wv, row0, nrows, BF16)
                g_ = matmul(h, wg, row0, nrows, F32)
                if row0 == 0:
                    o_, st = retention(q_, k_, v_, g_, log_g, nb, l, None, True)
                    new_rs.append(st)
                else:
                    o_ = retention(q_, k_, v_, g_, log_g, nb, l, state_ret[:, j], False)
                outs.append(o_)
            op, os_ = outs
            w_out = ret_w_out[j]
        o = jnp.concatenate([op, os_], axis=0)
        x1, hp, e_t, w_t, rank_t, cnt = outproj_ln(o, w_out.astype(BF16), x, mods, ln_g4, ln_b4,
                                                   moe_router[i].T, moe_router_bias[i].astype(F32), i)
        dest_tiles, block_e, first, nused, zero_start, nxt_e, par = _slot_plan(e_t, rank_t, cnt)
        xs = moe_dispatch(hp, dest_tiles, zero_start, nused)
        y_slots = moe_experts(xs, block_e, first, nused, nxt_e, par, moe_w_gate, moe_w_up, moe_w_down, i)
        x, h = moe_combine(y_slots, dest_tiles, w_t.T, hp, x1,
                           shared_w_gate[i].astype(BF16), shared_w_up[i].astype(BF16),
                           shared_w_down[i].astype(BF16), mods, ln_g4, ln_b4, i)
    return (x[:NP].reshape(BATCH, SEQ, D), x[NP:].reshape(DEC_BATCH, DEC_SEQ, D),
            jnp.stack(new_dk, axis=1), jnp.stack(new_dv, axis=1),
            jnp.stack(new_nk, axis=1), jnp.stack(new_nv, axis=1), jnp.stack(new_rs, axis=1))
```

```python
import functools
import math

import jax
import jax.numpy as jnp
import numpy as np
from jax import lax
from jax.experimental import pallas as pl
from jax.experimental.pallas import tpu as pltpu

F32 = jnp.float32
BF16 = jnp.bfloat16
U32 = jnp.uint32
I32 = jnp.int32

D = 2048
BATCH = 32
SEQ = 256
DEPTH = 4
DEC_BATCH = 8
DEC_SEQ = 1024
PAST = 512
NP = BATCH * SEQ
NS = DEC_BATCH * DEC_SEQ
NT = NP + NS
GRID_W = 64
N_MIXERS = 3
DIFF_HEADS = 8
DIFF_QK = 128
DIFF_V = 256
NA_HEADS = 16
NA_DH = 128
NA_KH = 8
NA_KH_MAX = 8
NA_KW = 16
RET_HEADS = 8
RET_QK = 256
RET_V = 512
RET_CHUNK = 128
N_EXPERTS = 64
N_GROUPS = 8
EXPERTS_PER_GROUP = 8
TOPK_GROUPS = 4
TOP_K = 6
D_EXPERT = 512
D_SHARED = 512
ROUTED_SCALE = 2.5
ROPE_BASE = 10000.0
LN_EPS = 1e-5
ALPHA = (2 * DEPTH) ** 0.25
N_COND = 16

VMEM_LIMIT = 56 * 1024 * 1024

MOE_TB = 256
MOE_NBLK = (NT * TOP_K) // MOE_TB + N_EXPERTS
MOE_CAP = MOE_NBLK * MOE_TB
RT_TM = 512
CMB_TM = 256
PK_S = 8
PK_L = D // 2 // PK_S
Y_S = D // 128
PK_P = PK_S + 1
Y_P = Y_S + 1
HI_MASK = np.uint32(0xFFFF0000)


def _cparams(sem):
    return pltpu.CompilerParams(dimension_semantics=sem, vmem_limit_bytes=VMEM_LIMIT)


def _cond_row(row0):
    return jnp.where(row0 < NP, 0, 1 + (row0 - NP) // DEC_SEQ)


def _mod_spec(layer, chunk, tm, moff=0):
    return pl.BlockSpec((None, None, None, 1, D),
                        lambda m, *_: (layer, _cond_row((m + moff) * tm), chunk, 0, 0))


def _sigmoid(x):
    return 1.0 / (1.0 + jnp.exp(-x))


def _silu(x):
    return x * _sigmoid(x)


def _mod_kernel(cond_ref, w_ref, b_ref, o_ref):
    c = _silu(cond_ref[...]).astype(BF16)
    o_ref[...] = jnp.dot(c, w_ref[...].astype(BF16), preferred_element_type=F32) + b_ref[...]


def modulation_table(cond, ada_w, ada_b):
    tn = 1024
    n6 = 6 * D
    out = pl.pallas_call(
        _mod_kernel,
        grid=(DEPTH, n6 // tn),
        in_specs=[pl.BlockSpec((N_COND, D), lambda l, n: (0, 0)),
                  pl.BlockSpec((None, D, tn), lambda l, n: (l, 0, n)),
                  pl.BlockSpec((None, 1, tn), lambda l, n: (l, 0, n))],
        out_specs=pl.BlockSpec((None, N_COND, tn), lambda l, n: (l, 0, n)),
        out_shape=jax.ShapeDtypeStruct((DEPTH, N_COND, n6), F32),
        compiler_params=_cparams(("arbitrary", "arbitrary")),
        name="modulation_table",
    )(cond, ada_w, ada_b.reshape(DEPTH, 1, n6))
    return out.reshape(DEPTH, N_COND, 6, 1, D)


def _modulate_kernel(x_ref, sh_ref, sc_ref, o_ref):
    o_ref[...] = (x_ref[...] * (1.0 + sc_ref[...]) + sh_ref[...]).astype(o_ref.dtype)


def modulate(x, mods, layer):
    tm = 512
    return pl.pallas_call(
        _modulate_kernel,
        grid=(NT // tm,),
        in_specs=[pl.BlockSpec((tm, D), lambda m: (m, 0)),
                  _mod_spec(layer, 0, tm), _mod_spec(layer, 1, tm)],
        out_specs=pl.BlockSpec((tm, D), lambda m: (m, 0)),
        out_shape=jax.ShapeDtypeStruct((NT, D), BF16),
        compiler_params=_cparams(("arbitrary",)),
        name="modulate",
    )(x, mods, mods)


def _mm_kernel(a_ref, w_ref, o_ref, wb_ref):
    @pl.when(pl.program_id(1) == 0)
    def _():
        wb_ref[...] = w_ref[...].astype(BF16)

    o_ref[...] = jnp.dot(a_ref[...], wb_ref[...], preferred_element_type=F32).astype(o_ref.dtype)


def matmul(a, w, layer, col0, ncols, row0, nrows, out_dtype):
    k = a.shape[1]
    tm, tn = 1024, 1024
    moff = row0 // tm
    joff = col0 // tn
    return pl.pallas_call(
        _mm_kernel,
        grid=(ncols // tn, nrows // tm),
        in_specs=[pl.BlockSpec((tm, k), lambda j, m: (m + moff, 0)),
                  pl.BlockSpec((None, k, tn), lambda j, m: (layer, 0, j + joff))],
        out_specs=pl.BlockSpec((tm, tn), lambda j, m: (m, j)),
        out_shape=jax.ShapeDtypeStruct((nrows, ncols), out_dtype),
        scratch_shapes=[pltpu.VMEM((k, tn), BF16)],
        compiler_params=_cparams(("arbitrary", "arbitrary")),
        name="in_proj",
    )(a, w)


def _rope_tables():
    half = DIFF_QK // 2
    t = jnp.arange(DEC_SEQ)
    inv = ROPE_BASE ** (-jnp.arange(0, half, 2, dtype=F32) / half)

    def cs(pos):
        ang = pos.astype(F32)[:, None] * inv[None, :]
        return jnp.cos(ang), jnp.sin(ang)

    cr, sr = cs(t // GRID_W)
    cc, sc = cs(t % GRID_W)
    cos = jnp.concatenate([cr, cr, cc, cc], axis=-1)
    sin = jnp.concatenate([-sr, sr, -sc, sc], axis=-1)
    return cos, sin


def _rope_kernel(x_ref, cos_ref, sin_ref, o_ref):
    cos = cos_ref[...]
    sin = sin_ref[...]
    lane = lax.broadcasted_iota(jnp.int32, cos.shape, 1)
    first = (lane % (DIFF_QK // 2)) < (DIFF_QK // 4)
    for g in range(D // DIFF_QK):
        xg = x_ref[:, g * DIFF_QK:(g + 1) * DIFF_QK]
        sw = jnp.where(first, pltpu.roll(xg, DIFF_QK - DIFF_QK // 4, 1), pltpu.roll(xg, DIFF_QK // 4, 1))
        o_ref[:, g * DIFF_QK:(g + 1) * DIFF_QK] = (xg * cos + sw * sin).astype(o_ref.dtype)


def rope(x, cos, sin):
    tm = 512
    nt = DEC_SEQ // tm
    return pl.pallas_call(
        _rope_kernel,
        grid=(NS // tm,),
        in_specs=[pl.BlockSpec((tm, D), lambda m: (m, 0)),
                  pl.BlockSpec((tm, DIFF_QK), lambda m: (m % nt, 0)),
                  pl.BlockSpec((tm, DIFF_QK), lambda m: (m % nt, 0))],
        out_specs=pl.BlockSpec((tm, D), lambda m: (m, 0)),
        out_shape=jax.ShapeDtypeStruct((NS, D), BF16),
        compiler_params=_cparams(("arbitrary",)),
        name="rope",
    )(x, cos, sin)


def _qkt(q, k):
    return lax.dot_general(q, k, (((1,), (1,)), ((), ())), preferred_element_type=F32)


def _softmax(s):
    m = jnp.max(s, axis=-1, keepdims=True)
    p = jnp.exp(s - m)
    return p / jnp.sum(p, axis=-1, keepdims=True)


def _diff_kernel(lam_init, lamp_ref, g_ref, q_ref, k_ref, v_ref, o_ref):
    lp = lamp_ref[...]
    lam = (jnp.exp(jnp.sum(lp[0:1] * lp[1:2], axis=-1, keepdims=True))
           - jnp.exp(jnp.sum(lp[2:3] * lp[3:4], axis=-1, keepdims=True)) + lam_init)
    scale = DIFF_QK ** -0.5
    g = g_ref[...]
    for h in range(DIFF_HEADS):
        c0 = h * DIFF_V
        q = q_ref[:, c0:c0 + DIFF_V].astype(BF16)
        k = k_ref[:, c0:c0 + DIFF_V].astype(BF16)
        v = v_ref[:, c0:c0 + DIFF_V].astype(BF16)
        a1 = _softmax(_qkt(q[:, :DIFF_QK], k[:, :DIFF_QK]) * scale)
        a2 = _softmax(_qkt(q[:, DIFF_QK:], k[:, DIFF_QK:]) * scale)
        a = (a1 - lam * a2).astype(BF16)
        o = jnp.dot(a, v, preferred_element_type=F32)
        o = o * lax.rsqrt(jnp.mean(jnp.square(o), axis=-1, keepdims=True) + LN_EPS) * g
        o_ref[:, c0:c0 + DIFF_V] = (o * (1.0 - lam_init)).astype(o_ref.dtype)


def diff_attention(q, k, v, lam_p, subln_g, lam_init, nb, lq, lk, tq):
    nq = lq // tq
    return pl.pallas_call(
        functools.partial(_diff_kernel, lam_init),
        grid=(nb, nq),
        in_specs=[pl.BlockSpec((4, DIFF_QK), lambda b, i: (0, 0)),
                  pl.BlockSpec((1, DIFF_V), lambda b, i: (0, 0)),
                  pl.BlockSpec((tq, D), lambda b, i: (b * nq + i, 0)),
                  pl.BlockSpec((lk, D), lambda b, i: (b, 0)),
                  pl.BlockSpec((lk, D), lambda b, i: (b, 0))],
        out_specs=pl.BlockSpec((tq, D), lambda b, i: (b * nq + i, 0)),
        out_shape=jax.ShapeDtypeStruct((nb * lq, D), BF16),
        compiler_params=_cparams(("arbitrary", "arbitrary")),
        name="diff_attention",
    )(lam_p, subln_g.reshape(1, DIFF_V), q, k, v)


def _na_ctx_kernel(q_ref, k_ref, v_ref, o_ref):
    scale = NA_DH ** -0.5
    for h in range(NA_HEADS):
        c0 = h * NA_DH
        q = q_ref[:, c0:c0 + NA_DH].astype(BF16)
        k = k_ref[:, c0:c0 + NA_DH].astype(BF16)
        v = v_ref[:, c0:c0 + NA_DH].astype(BF16)
        p = _softmax(_qkt(q, k) * scale).astype(BF16)
        o_ref[:, c0:c0 + NA_DH] = jnp.dot(p, v, preferred_element_type=F32).astype(o_ref.dtype)


def na_context_attention(q, k, v):
    spec = pl.BlockSpec((SEQ, D), lambda b: (b, 0))
    return pl.pallas_call(
        _na_ctx_kernel,
        grid=(BATCH,),
        in_specs=[spec, spec, spec],
        out_specs=spec,
        out_shape=jax.ShapeDtypeStruct((NP, D), BF16),
        compiler_params=_cparams(("arbitrary",)),
        name="na_context_attention",
    )(q, k, v)


NA_ROWS = DEC_SEQ // GRID_W
NA_NLOC = NA_KH * GRID_W


def _na_window_row(r):
    return jnp.clip(r - NA_KH // 2, 0, NA_ROWS - NA_KH)


def _na_bias_tables(rpb):
    col = jnp.arange(GRID_W)
    col_start = jnp.clip(col - NA_KW // 2, 0, GRID_W - NA_KW)
    col_ok = (col[None, :] >= col_start[:, None]) & (col[None, :] < col_start[:, None] + NA_KW)
    dc_idx = jnp.clip(col[None, :] - col[:, None] + NA_KW - 1, 0, 2 * NA_KW - 2)
    hot = (dc_idx[:, :, None] == jnp.arange(2 * NA_KW - 1)[None, None, :]).astype(F32)
    full = jnp.einsum('hrd,qkd->hqrk', rpb.astype(F32), hot, precision=lax.Precision.HIGHEST)
    full = jnp.where(col_ok[None, :, None, :], full, -jnp.inf)
    full = full.reshape(NA_HEADS, GRID_W, (2 * NA_KH_MAX - 1) * GRID_W)
    tabs = [full[:, :, (NA_KH_MAX - 1 - o) * GRID_W:(NA_KH_MAX - 1 - o) * GRID_W + NA_NLOC]
            for o in range(NA_KH)]
    return jnp.stack(tabs)


def _na_lat_kernel(q_ref, k_ref, v_ref, kc_ref, vc_ref, bias_ref, o_ref, pctx_ref, oloc_ref):
    scale = NA_DH ** -0.5
    s_ctx_all = _qkt(q_ref[...], kc_ref[...]) * scale
    for r in range(NA_ROWS):
        r0 = min(max(r - NA_KH // 2, 0), NA_ROWS - NA_KH)
        rows = slice(r * GRID_W, (r + 1) * GRID_W)
        win = slice(r0 * GRID_W, r0 * GRID_W + NA_NLOC)
        s_loc = _qkt(q_ref[rows, :], k_ref[win, :]) * scale + bias_ref[r - r0]
        s_ctx = s_ctx_all[rows, :]
        m = jnp.maximum(jnp.max(s_loc, axis=-1, keepdims=True), jnp.max(s_ctx, axis=-1, keepdims=True))
        p_loc = jnp.exp(s_loc - m)
        p_ctx = jnp.exp(s_ctx - m)
        den = jnp.sum(p_loc, axis=-1, keepdims=True) + jnp.sum(p_ctx, axis=-1, keepdims=True)
        oloc_ref[rows, :] = jnp.dot((p_loc / den).astype(BF16), v_ref[win, :], preferred_element_type=F32)
        pctx_ref[rows, :] = (p_ctx / den).astype(BF16)
    o = oloc_ref[...] + jnp.dot(pctx_ref[...], vc_ref[...], preferred_element_type=F32)
    o_ref[...] = o.astype(o_ref.dtype)


def na_latent_attention(q, k, v, k_ctx, v_ctx, bias_tabs):
    head = pl.BlockSpec((DEC_SEQ, NA_DH), lambda b, h: (b, h))
    ctx = pl.BlockSpec((None, PAST, NA_DH), lambda b, h: (b, 0, h))
    return pl.pallas_call(
        _na_lat_kernel,
        grid=(DEC_BATCH, NA_HEADS),
        in_specs=[head, head, head, ctx, ctx,
                  pl.BlockSpec((NA_KH, None, GRID_W, NA_NLOC), lambda b, h: (0, h, 0, 0))],
        out_specs=head,
        out_shape=jax.ShapeDtypeStruct((NS, D), BF16),
        scratch_shapes=[pltpu.VMEM((DEC_SEQ, PAST), BF16), pltpu.VMEM((DEC_SEQ, NA_DH), F32)],
        compiler_params=_cparams(("arbitrary", "arbitrary")),
        name="na_latent_attention",
    )(q, k, v, k_ctx, v_ctx, bias_tabs)


def _ret_kernel(nc, has_state, emit_state, logg_ref, q_ref, k_ref, v_ref, g_ref, *rest):
    rest = list(rest)
    s0_ref = rest.pop(0) if has_state else None
    o_ref = rest.pop(0)
    st_ref = rest.pop(0) if emit_state else None
    oacc, sacc = rest
    h = pl.program_id(1)
    lgf = logg_ref[0, h]
    lgb = logg_ref[1, h]
    c = RET_CHUNK
    pos = lax.broadcasted_iota(jnp.int32, (c, 1), 0).astype(F32)
    diff = (lax.broadcasted_iota(jnp.int32, (c, c), 0) - lax.broadcasted_iota(jnp.int32, (c, c), 1)).astype(F32)
    dmask_f = jnp.where(diff >= 0, jnp.exp(jnp.maximum(diff, 0.0) * lgf), 0.0)
    dmask_b = jnp.where(diff <= 0, jnp.exp(jnp.maximum(-diff, 0.0) * lgb), 0.0)
    qdec_f = jnp.exp((pos + 1.0) * lgf)
    kdec_f = jnp.exp((c - 1.0 - pos) * lgf)
    qdec_b = jnp.exp((c - pos) * lgb)
    kdec_b = jnp.exp(pos * lgb)
    cd_f = jnp.exp(jnp.full((1, 1), c, F32) * lgf)
    cd_b = jnp.exp(jnp.full((1, 1), c, F32) * lgb)

    def chunk(i):
        sl = slice(i * c, (i + 1) * c)
        return q_ref[sl, :].astype(BF16), k_ref[sl, :] * (RET_QK ** -0.5), v_ref[sl, :].astype(BF16)

    def state_update(kc, kdec, vb, cd):
        kd = (kc * kdec).T.astype(BF16)
        sacc[...] = sacc[...] * cd + jnp.dot(kd, vb, preferred_element_type=F32)

    if has_state:
        sacc[...] = s0_ref[0]
    else:
        sacc[...] = jnp.zeros_like(sacc)
    for i in range(nc):
        qb, kc, vb = chunk(i)
        qk = _qkt(qb, kc.astype(BF16))
        o = (jnp.dot((qk * dmask_f).astype(BF16), vb, preferred_element_type=F32)
             + jnp.dot((qk * dmask_b).astype(BF16), vb, preferred_element_type=F32)
             + jnp.dot(qb, sacc[...].astype(BF16), preferred_element_type=F32) * qdec_f)
        oacc[i * c:(i + 1) * c, :] = o
        state_update(kc, kdec_f, vb, cd_f)
    if emit_state:
        st_ref[0] = sacc[...]

    if has_state:
        sacc[...] = s0_ref[1]
    else:
        sacc[...] = jnp.zeros_like(sacc)
    for i in reversed(range(nc)):
        qb, kc, vb = chunk(i)
        oacc[i * c:(i + 1) * c, :] += jnp.dot(qb, sacc[...].astype(BF16), preferred_element_type=F32) * qdec_b
        state_update(kc, kdec_b, vb, cd_b)
    if emit_state:
        st_ref[1] = sacc[...]

    o = oacc[...]
    mu = jnp.mean(o, axis=-1, keepdims=True)
    var = jnp.mean(jnp.square(o - mu), axis=-1, keepdims=True)
    o = (o - mu) * lax.rsqrt(var + LN_EPS)
    o_ref[...] = (o * _silu(g_ref[...])).astype(o_ref.dtype)


def retention(q, k, v, g, log_g, nb, l, state=None, emit_state=False):
    nc = l // RET_CHUNK
    has_state = state is not None
    in_specs = [pl.BlockSpec(memory_space=pltpu.SMEM),
                pl.BlockSpec((l, RET_QK), lambda b, h: (b, h)),
                pl.BlockSpec((l, RET_QK), lambda b, h: (b, h)),
                pl.BlockSpec((l, RET_V), lambda b, h: (b, h)),
                pl.BlockSpec((l, RET_V), lambda b, h: (b, h))]
    args = [log_g, q, k, v, g]
    st_spec = pl.BlockSpec((None, 2, None, RET_QK, RET_V), lambda b, h: (b, 0, h, 0, 0))
    if has_state:
        in_specs.append(st_spec)
        args.append(state)
    out_specs = [pl.BlockSpec((l, RET_V), lambda b, h: (b, h))]
    out_shape = [jax.ShapeDtypeStruct((nb * l, RET_HEADS * RET_V), BF16)]
    if emit_state:
        out_specs.append(st_spec)
        out_shape.append(jax.ShapeDtypeStruct((nb, 2, RET_HEADS, RET_QK, RET_V), F32))
    res = pl.pallas_call(
        functools.partial(_ret_kernel, nc, has_state, emit_state),
        grid=(nb, RET_HEADS),
        in_specs=in_specs,
        out_specs=out_specs,
        out_shape=out_shape,
        scratch_shapes=[pltpu.VMEM((l, RET_V), F32), pltpu.VMEM((RET_QK, RET_V), F32)],
        compiler_params=_cparams(("arbitrary", "arbitrary")),
        name="retention",
    )(*args)
    return res if emit_state else res[0]


def _layer_norm(z, g, b):
    mu = jnp.mean(z, axis=-1, keepdims=True)
    var = jnp.mean(jnp.square(z - mu), axis=-1, keepdims=True)
    return (z - mu) * lax.rsqrt(var + LN_EPS) * g + b


def _pack_rows(hn, hp_ref):
    tm = hn.shape[0]
    half = D // 2
    lo = lax.bitcast_convert_type(hn[:, :half].astype(BF16).astype(F32), U32)
    hi = lax.bitcast_convert_type(hn[:, half:].astype(BF16).astype(F32), U32)
    w = (lo >> 16) | (hi & HI_MASK)
    for s in range(PK_S):
        hp_ref[pl.ds(s, tm, stride=PK_P), :] = w[:, s * PK_L:(s + 1) * PK_L]
    hp_ref[pl.ds(PK_S, tm, stride=PK_P), :] = jnp.zeros((tm, PK_L), U32)


def _unpack_rows(hp_ref, tm):
    lo, hi = [], []
    for s in range(PK_S):
        w = hp_ref[pl.ds(s, tm, stride=PK_P), :]
        lo.append(lax.bitcast_convert_type(w << 16, F32).astype(BF16))
        hi.append(lax.bitcast_convert_type(w & HI_MASK, F32).astype(BF16))
    return jnp.concatenate(lo + hi, axis=1)


def _route_tile(hn, wrt_ref, rb_ref, e_ref, w_ref, rank_ref, cnt_ref):
    tm = hn.shape[0]
    neg = -jnp.inf
    logits = lax.dot_general(wrt_ref[...], hn, (((1,), (1,)), ((), ())),
                             preferred_element_type=F32, precision=lax.Precision.HIGHEST)
    s = _sigmoid(logits)
    biased = s + rb_ref[...]
    io8 = lax.broadcasted_iota(I32, (EXPERTS_PER_GROUP, tm), 0).astype(F32)
    slabs, gscore = [], []
    for g in range(N_GROUPS):
        slab = biased[g * EXPERTS_PER_GROUP:(g + 1) * EXPERTS_PER_GROUP, :]
        m1 = jnp.max(slab, axis=0, keepdims=True)
        i1 = jnp.min(jnp.where(slab == m1, io8, float(EXPERTS_PER_GROUP)), axis=0, keepdims=True)
        m2 = jnp.max(jnp.where(io8 == i1, neg, slab), axis=0, keepdims=True)
        slabs.append(slab)
        gscore.append(m1 + m2)
    masked = []
    for g in range(N_GROUPS):
        ahead = jnp.zeros_like(gscore[g])
        for g2 in range(N_GROUPS):
            if g2 == g:
                continue
            better = (gscore[g2] > gscore[g]) | ((gscore[g2] == gscore[g]) if g2 < g else False)
            ahead = ahead + jnp.where(better, 1.0, 0.0)
        masked.append(jnp.where(ahead < float(TOPK_GROUPS), slabs[g], neg))
    v = jnp.concatenate(masked, axis=0)
    io = lax.broadcasted_iota(I32, (N_EXPERTS, tm), 0).astype(F32)
    idxs, ws, hots = [], [], []
    for _ in range(TOP_K):
        m = jnp.max(v, axis=0, keepdims=True)
        idx = jnp.min(jnp.where(v == m, io, float(N_EXPERTS)), axis=0, keepdims=True)
        hot = io == idx
        ws.append(jnp.sum(jnp.where(hot, s, 0.0), axis=0, keepdims=True))
        v = jnp.where(hot, neg, v)
        idxs.append(idx)
        hots.append(hot)
    wsum = ws[0]
    for k in range(1, TOP_K):
        wsum = wsum + ws[k]
    chosen = jnp.where(hots[0], 1.0, 0.0)
    for k in range(1, TOP_K):
        chosen = chosen + jnp.where(hots[k], 1.0, 0.0)
    upper = jnp.where(lax.broadcasted_iota(I32, (tm, tm), 0) < lax.broadcasted_iota(I32, (tm, tm), 1), 1.0, 0.0)
    before = jnp.dot(chosen.astype(BF16), upper.astype(BF16), preferred_element_type=F32)
    zero_row = jnp.zeros((1, tm), F32)
    for k in range(8):
        if k < TOP_K:
            e_ref[k:k + 1, :] = idxs[k].astype(I32)
            w_ref[k:k + 1, :] = ws[k] / wsum * ROUTED_SCALE
            rank_ref[k:k + 1, :] = jnp.sum(jnp.where(hots[k], before, 0.0), axis=0, keepdims=True).astype(I32)
        else:
            e_ref[k:k + 1, :] = zero_row.astype(I32)
            w_ref[k:k + 1, :] = zero_row
            rank_ref[k:k + 1, :] = zero_row.astype(I32)
    cnt = jnp.sum(chosen, axis=1, keepdims=True)
    cnt_ref[...] = jnp.broadcast_to(cnt, (N_EXPERTS, 128)).astype(I32)


def _outproj_kernel(nk, o_ref, w_ref, x_ref, gate_ref, lng_ref, lnb_ref, sh_ref, sc_ref, wrt_ref, rb_ref,
                    xo_ref, hp_ref, e_ref, wt_ref, rank_ref, cnt_ref, acc_ref):
    kk = pl.program_id(1)
    part = jnp.dot(o_ref[...], w_ref[...], preferred_element_type=F32)

    @pl.when(kk == 0)
    def _():
        acc_ref[...] = part

    @pl.when(kk > 0)
    def _():
        acc_ref[...] += part

    @pl.when(kk == nk - 1)
    def _():
        xn = _layer_norm(ALPHA * x_ref[...] + gate_ref[...] * acc_ref[...], lng_ref[...], lnb_ref[...])
        xo_ref[...] = xn
        hn = xn * (1.0 + sc_ref[...]) + sh_ref[...]
        _pack_rows(hn, hp_ref)
        _route_tile(hn, wrt_ref, rb_ref, e_ref, wt_ref, rank_ref, cnt_ref)


def outproj_ln(o, w_out, x, mods, ln_g, ln_b, w_router_t, b_router, layer):
    k = o.shape[1]
    tk = 2048 if k == 2048 else 1024
    nk = k // tk
    tm = RT_TM
    ntile = NT // tm
    row = lambda m, kk: (m, 0)
    col = lambda m, kk: (0, m)
    return pl.pallas_call(
        functools.partial(_outproj_kernel, nk),
        grid=(ntile, nk),
        in_specs=[pl.BlockSpec((tm, tk), lambda m, kk: (m, kk)),
                  pl.BlockSpec((tk, D), lambda m, kk: (kk, 0),
                               pipeline_mode=pl.Buffered(1) if nk == 1 else None),
                  pl.BlockSpec((tm, D), row),
                  _mod_spec(layer, 2, tm),
                  pl.BlockSpec((None, None, 1, D), lambda m, kk: (layer, 0, 0, 0)),
                  pl.BlockSpec((None, None, 1, D), lambda m, kk: (layer, 0, 0, 0)),
                  _mod_spec(layer, 3, tm), _mod_spec(layer, 4, tm),
                  pl.BlockSpec((N_EXPERTS, D), lambda m, kk: (0, 0)),
                  pl.BlockSpec((N_EXPERTS, 1), lambda m, kk: (0, 0))],
        out_specs=[pl.BlockSpec((tm, D), row),
                   pl.BlockSpec((tm * PK_P, PK_L), row),
                   pl.BlockSpec((8, tm), col), pl.BlockSpec((8, tm), col), pl.BlockSpec((8, tm), col),
                   pl.BlockSpec((None, N_EXPERTS, 128), lambda m, kk: (m, 0, 0))],
        out_shape=[jax.ShapeDtypeStruct((NT, D), F32),
                   jax.ShapeDtypeStruct((NT * PK_P, PK_L), U32),
                   jax.ShapeDtypeStruct((8, NT), I32), jax.ShapeDtypeStruct((8, NT), F32),
                   jax.ShapeDtypeStruct((8, NT), I32),
                   jax.ShapeDtypeStruct((ntile, N_EXPERTS, 128), I32)],
        scratch_shapes=[pltpu.VMEM((tm, D), F32)],
        compiler_params=_cparams(("arbitrary", "arbitrary")),
        name="outproj_ln_route",
    )(o, w_out, x, mods, ln_g, ln_b, mods, mods, w_router_t, b_router.reshape(N_EXPERTS, 1))


def _outproj_split_kernel(np_tiles, op_ref, os_ref, w_ref, x_ref, gate_ref, lng_ref, lnb_ref, sh_ref, sc_ref,
                         wrt_ref, rb_ref, xo_ref, hp_ref, e_ref, wt_ref, rank_ref, cnt_ref):
    m = pl.program_id(0)
    o_blk = jnp.where(m < np_tiles, op_ref[...], os_ref[...])
    y = jnp.dot(o_blk, w_ref[...], preferred_element_type=F32)
    xn = _layer_norm(ALPHA * x_ref[...] + gate_ref[...] * y, lng_ref[...], lnb_ref[...])
    xo_ref[...] = xn
    hn = xn * (1.0 + sc_ref[...]) + sh_ref[...]
    _pack_rows(hn, hp_ref)
    _route_tile(hn, wrt_ref, rb_ref, e_ref, wt_ref, rank_ref, cnt_ref)


def outproj_ln_split(op, os_, w_out, x, mods, ln_g, ln_b, w_router_t, b_router, layer):
    tm = RT_TM
    ntile = NT // tm
    np_tiles = NP // tm
    ns_tiles = NS // tm
    row = lambda m: (m, 0)
    col = lambda m: (0, m)
    mod = lambda chunk: _mod_spec(layer, chunk, tm)
    const2 = lambda m: (0, 0)
    return pl.pallas_call(
        functools.partial(_outproj_split_kernel, np_tiles),
        grid=(ntile,),
        in_specs=[pl.BlockSpec((tm, D), lambda m: (jnp.minimum(m, np_tiles - 1), 0)),
                  pl.BlockSpec((tm, D), lambda m: (jnp.clip(m - np_tiles, 0, ns_tiles - 1), 0)),
                  pl.BlockSpec((D, D), const2, pipeline_mode=pl.Buffered(1)),
                  pl.BlockSpec((tm, D), row),
                  mod(2),
                  pl.BlockSpec((None, None, 1, D), lambda m: (layer, 0, 0, 0)),
                  pl.BlockSpec((None, None, 1, D), lambda m: (layer, 0, 0, 0)),
                  mod(3), mod(4),
                  pl.BlockSpec((N_EXPERTS, D), const2),
                  pl.BlockSpec((N_EXPERTS, 1), const2)],
        out_specs=[pl.BlockSpec((tm, D), row),
                   pl.BlockSpec((tm * PK_P, PK_L), row),
                   pl.BlockSpec((8, tm), col), pl.BlockSpec((8, tm), col), pl.BlockSpec((8, tm), col),
                   pl.BlockSpec((None, N_EXPERTS, 128), lambda m: (m, 0, 0))],
        out_shape=[jax.ShapeDtypeStruct((NT, D), F32),
                   jax.ShapeDtypeStruct((NT * PK_P, PK_L), U32),
                   jax.ShapeDtypeStruct((8, NT), I32), jax.ShapeDtypeStruct((8, NT), F32),
                   jax.ShapeDtypeStruct((8, NT), I32),
                   jax.ShapeDtypeStruct((ntile, N_EXPERTS, 128), I32)],
        compiler_params=_cparams(("arbitrary",)),
        name="outproj_ln_route_split",
    )(op, os_, w_out, x, mods, ln_g, ln_b, mods, mods, w_router_t, b_router.reshape(N_EXPERTS, 1))


def _slot_plan(e_t, rank_t, cnt):
    cnt = cnt[:, :, 0]
    tile_base = jnp.cumsum(cnt, axis=0) - cnt
    counts = jnp.sum(cnt, axis=0)
    padded = (counts + MOE_TB - 1) // MOE_TB * MOE_TB
    pad_end = jnp.cumsum(padded)
    pad_start = pad_end - padded
    offs = jnp.repeat(pad_start[None, :] + tile_base, NT // cnt.shape[0], axis=0)
    hot = e_t[:TOP_K, :, None] == jnp.arange(N_EXPERTS, dtype=I32)[None, None, :]
    dest = jnp.sum(jnp.where(hot, offs[None], 0), axis=-1) + rank_t[:TOP_K]
    ntile = NT // CMB_TM
    dest_tiles = dest.reshape(TOP_K, ntile, CMB_TM).transpose(1, 0, 2).reshape(ntile, 1, TOP_K * CMB_TM)
    nused = pad_end[-1] // MOE_TB
    blk = jnp.minimum(jnp.arange(MOE_NBLK, dtype=I32), nused - 1) * MOE_TB
    block_e = jnp.minimum(jnp.sum((pad_end[None, :] <= blk[:, None]).astype(I32), axis=1), N_EXPERTS - 1)
    first = jnp.concatenate([jnp.ones((1,), I32), (block_e[1:] != block_e[:-1]).astype(I32)])
    zero_start = pad_start + counts
    ids = jnp.arange(N_EXPERTS, dtype=I32)
    used = counts > 0
    later = jnp.where(used[None, :] & (ids[None, :] > ids[:, None]), ids[None, :], N_EXPERTS)
    nxt = jnp.min(later, axis=1)
    nxt = jnp.where(nxt == N_EXPERTS, ids, nxt)
    par = (jnp.cumsum(used.astype(I32)) - 1) % 2
    owner = block_e[:, None] == ids[None, :]
    nxt_b = jnp.sum(jnp.where(owner, nxt[None, :], 0), axis=1)
    par_b = jnp.sum(jnp.where(owner, par[None, :], 0), axis=1)
    return (dest_tiles.astype(I32), block_e.astype(I32), first, nused.reshape(1).astype(I32),
            zero_start.astype(I32), nxt_b.astype(I32), par_b.astype(I32))


_DISPATCH_ROWS = TOP_K * CMB_TM
_CMB_ROWS = 32


def _dispatch_kernel(zs_ref, nused_ref, dest_ref, hp_ref, xs_hbm, stage, zbuf, zsem, sem):
    i = pl.program_id(0)
    n = pl.num_programs(0)
    slot = i % 2

    @pl.when(i == 0)
    def _():
        zbuf[...] = jnp.zeros_like(zbuf)

        def zcopy(slot0, nslot):
            return pltpu.make_async_copy(zbuf.at[pl.ds(0, nslot * PK_P)],
                                         xs_hbm.at[pl.ds(slot0 * PK_P, nslot * PK_P)], zsem)

        def pad_pieces(e, wait):
            slot0 = zs_ref[e]
            npad = (MOE_TB - slot0 % MOE_TB) % MOE_TB
            piece = MOE_TB // 2
            while piece >= 1:
                @pl.when((npad & piece) != 0)
                def _(slot0=slot0, piece=piece):
                    cp = zcopy(slot0, piece)
                    cp.wait() if wait else cp.start()
                slot0 = slot0 + (npad & piece)
                piece //= 2

        def zstart(e, carry):
            pad_pieces(e, False)
            return carry

        def zwait(e, carry):
            pad_pieces(e, True)
            return carry

        def tstart(b, carry):
            zcopy(b * MOE_TB, MOE_TB).start()
            return carry

        def twait(b, carry):
            zcopy(b * MOE_TB, MOE_TB).wait()
            return carry

        lax.fori_loop(0, N_EXPERTS, zstart, 0)
        lax.fori_loop(nused_ref[0], MOE_NBLK + 1, tstart, 0)
        lax.fori_loop(0, N_EXPERTS, zwait, 0)
        lax.fori_loop(nused_ref[0], MOE_NBLK + 1, twait, 0)

    stage[slot] = hp_ref[...]

    def issue(r, carry):
        for k in range(TOP_K):
            dst = dest_ref[0, 0, k * CMB_TM + r] * PK_P
            pltpu.make_async_copy(stage.at[slot, pl.ds(r * PK_P, PK_P)], xs_hbm.at[pl.ds(dst, PK_P)],
                                  sem.at[slot]).start(priority=k % 2)
        return carry

    lax.fori_loop(0, CMB_TM, issue, 0, unroll=2)

    def drain(s):
        for k in range(TOP_K):
            pltpu.make_async_copy(stage.at[s], xs_hbm.at[pl.ds(0, CMB_TM * PK_P)], sem.at[s]).wait()

    @pl.when(i > 0)
    def _():
        drain(1 - slot)

    @pl.when(i == n - 1)
    def _():
        drain(slot)


def moe_dispatch(hp, dest_tiles, zero_start, nused):
    ntile = NT // CMB_TM
    grid_spec = pltpu.PrefetchScalarGridSpec(
        num_scalar_prefetch=2,
        grid=(ntile,),
        in_specs=[pl.BlockSpec((1, 1, _DISPATCH_ROWS), lambda i, zs, nu: (i, 0, 0), memory_space=pltpu.SMEM),
                  pl.BlockSpec((CMB_TM * PK_P, PK_L), lambda i, zs, nu: (i, 0))],
        out_specs=pl.BlockSpec(memory_space=pl.ANY),
        scratch_shapes=[pltpu.VMEM((2, CMB_TM * PK_P, PK_L), U32),
                        pltpu.VMEM((MOE_TB * PK_P, PK_L), U32),
                        pltpu.SemaphoreType.DMA(()), pltpu.SemaphoreType.DMA((2,))],
    )
    return pl.pallas_call(
        _dispatch_kernel,
        grid_spec=grid_spec,
        out_shape=jax.ShapeDtypeStruct(((MOE_CAP + MOE_TB) * PK_P, PK_L), U32),
        compiler_params=_cparams(("arbitrary",)),
        name="moe_dispatch",
    )(zero_start, nused, dest_tiles, hp)


def _moe_kernel(layer, be_ref, first_ref, nused_ref, nxt_ref, par_ref, xs_ref, wg_hbm, wu_hbm, wd_hbm, y_ref,
                wgs, wus, wds, wgb, wub, wdb, sem):
    i = pl.program_id(0)

    def weight_copies(e, s):
        return (pltpu.make_async_copy(wg_hbm.at[layer, e], wgs.at[s], sem.at[s]),
                pltpu.make_async_copy(wu_hbm.at[layer, e], wus.at[s], sem.at[s]),
                pltpu.make_async_copy(wd_hbm.at[layer, e], wds.at[s], sem.at[s]))

    @pl.when(i == 0)
    def _():
        for cp in weight_copies(be_ref[0], par_ref[0]):
            cp.start()

    @pl.when(first_ref[i] == 1)
    def _():
        e, s = be_ref[i], par_ref[i]
        for cp in weight_copies(e, s):
            cp.wait()

        @pl.when(nxt_ref[i] != e)
        def _():
            for cp in weight_copies(nxt_ref[i], 1 - s):
                cp.start()

        wgb[...] = wgs[s].astype(BF16)
        wub[...] = wus[s].astype(BF16)
        wdb[...] = wds[s].astype(BF16)

    @pl.when(i < nused_ref[0])
    def _():
        x = _unpack_rows(xs_ref, MOE_TB)
        a = (_silu(jnp.dot(x, wgb[...], preferred_element_type=F32))
             * jnp.dot(x, wub[...], preferred_element_type=F32))
        y = jnp.dot(a.astype(BF16), wdb[...], preferred_element_type=F32)
        for j in range(Y_S):
            y_ref[pl.ds(j, MOE_TB, stride=Y_P), :] = y[:, j * 128:(j + 1) * 128]
        y_ref[pl.ds(Y_S, MOE_TB, stride=Y_P), :] = jnp.zeros((MOE_TB, 128), F32)

    @pl.when(i >= nused_ref[0])
    def _():
        y_ref[...] = jnp.zeros_like(y_ref)


def moe_experts(xs, block_e, first, nused, nxt_e, par, w_gate, w_up, w_down, layer):
    blk = lambda i, be, fi, nu, nx, pa: (jnp.minimum(i, nu[0] - 1), 0)
    grid_spec = pltpu.PrefetchScalarGridSpec(
        num_scalar_prefetch=5,
        grid=(MOE_NBLK,),
        in_specs=[pl.BlockSpec((MOE_TB * PK_P, PK_L), blk),
                  pl.BlockSpec(memory_space=pl.ANY), pl.BlockSpec(memory_space=pl.ANY),
                  pl.BlockSpec(memory_space=pl.ANY)],
        out_specs=pl.BlockSpec((MOE_TB * Y_P, 128), lambda i, be, fi, nu, nx, pa: (i, 0)),
        scratch_shapes=[pltpu.VMEM((2, D, D_EXPERT), F32), pltpu.VMEM((2, D, D_EXPERT), F32),
                        pltpu.VMEM((2, D_EXPERT, D), F32),
                        pltpu.VMEM((D, D_EXPERT), BF16), pltpu.VMEM((D, D_EXPERT), BF16),
                        pltpu.VMEM((D_EXPERT, D), BF16),
                        pltpu.SemaphoreType.DMA((2,))],
    )
    return pl.pallas_call(
        functools.partial(_moe_kernel, layer),
        grid_spec=grid_spec,
        out_shape=jax.ShapeDtypeStruct((MOE_CAP * Y_P, 128), F32),
        compiler_params=_cparams(("arbitrary",)),
        name="moe_experts",
    )(block_e, first, nused, nxt_e, par, xs, w_gate, w_up, w_down)


def _combine_kernel(has_next, dest_ref, destn_ref, y_hbm, w_ref, hp_ref, x_ref, wsg_ref, wsu_ref, wsd_ref,
                    gate_ref, lng_ref, lnb_ref, *rest):
    if has_next:
        sh_ref, sc_ref, xo_ref, ho_ref, buf, routed_ref, sem = rest
    else:
        xo_ref, buf, routed_ref, sem = rest
    i = pl.program_id(0)
    n = pl.num_programs(0)
    slot = i % 2
    nrow = TOP_K * CMB_TM

    def gather(dref, s):
        def body(r, carry):
            for k in range(TOP_K):
                src = dref[0, 0, k * CMB_TM + r] * Y_P
                dst = (k * CMB_TM + r) * Y_P
                pltpu.make_async_copy(y_hbm.at[pl.ds(src, Y_S)], buf.at[s, pl.ds(dst, Y_S)],
                                      sem.at[s]).start(priority=k % 2)
            return carry
        lax.fori_loop(0, CMB_TM, body, 0, unroll=2)

    @pl.when(i == 0)
    def _():
        gather(dest_ref, 0)

    @pl.when(i + 1 < n)
    def _():
        gather(destn_ref, 1 - slot)

    hb = _unpack_rows(hp_ref, CMB_TM)
    a = (_silu(jnp.dot(hb, wsg_ref[...], preferred_element_type=F32))
         * jnp.dot(hb, wsu_ref[...], preferred_element_type=F32))
    shared = jnp.dot(a.astype(BF16), wsd_ref[...], preferred_element_type=F32)

    pltpu.make_async_copy(y_hbm.at[pl.ds(0, nrow * Y_S)], buf.at[slot, pl.ds(0, nrow * Y_S)],
                          sem.at[slot]).wait()
    for r0 in range(0, CMB_TM, _CMB_ROWS):
        wsub = w_ref[r0:r0 + _CMB_ROWS, :]
        wb = [jnp.broadcast_to(wsub[:, k:k + 1], (_CMB_ROWS, 128)) for k in range(TOP_K)]
        for j in range(Y_S):
            acc = buf[slot, pl.ds(r0 * Y_P + j, _CMB_ROWS, stride=Y_P), :] * wb[0]
            for k in range(1, TOP_K):
                acc = acc + buf[slot, pl.ds((k * CMB_TM + r0) * Y_P + j, _CMB_ROWS, stride=Y_P), :] * wb[k]
            routed_ref[r0:r0 + _CMB_ROWS, j * 128:(j + 1) * 128] = acc
    routed = routed_ref[...]
    xn = _layer_norm(ALPHA * x_ref[...] + gate_ref[...] * (routed + shared), lng_ref[...], lnb_ref[...])
    xo_ref[...] = xn
    if has_next:
        ho_ref[...] = (xn * (1.0 + sc_ref[...]) + sh_ref[...]).astype(ho_ref.dtype)


def moe_combine(y_slots, dest_tiles, w_tok, hp, x1, ws_gate, ws_up, ws_down, mods, ln_g, ln_b, layer):
    has_next = layer + 1 < DEPTH
    tm = CMB_TM
    ntile = NT // tm
    row = lambda m: (m, 0)
    const = lambda m: (0, 0)
    in_specs = [pl.BlockSpec((1, 1, TOP_K * tm), lambda m: (m, 0, 0), memory_space=pltpu.SMEM),
                pl.BlockSpec((1, 1, TOP_K * tm), lambda m: (jnp.minimum(m + 1, ntile - 1), 0, 0),
                             memory_space=pltpu.SMEM),
                pl.BlockSpec(memory_space=pl.ANY),
                pl.BlockSpec((tm, 8), row),
                pl.BlockSpec((tm * PK_P, PK_L), row), pl.BlockSpec((tm, D), row),
                pl.BlockSpec((D, D_SHARED), const, pipeline_mode=pl.Buffered(1)),
                pl.BlockSpec((D, D_SHARED), const, pipeline_mode=pl.Buffered(1)),
                pl.BlockSpec((D_SHARED, D), const, pipeline_mode=pl.Buffered(1)),
                _mod_spec(layer, 5, tm),
                pl.BlockSpec((None, None, 1, D), lambda m: (layer, 1, 0, 0)),
                pl.BlockSpec((None, None, 1, D), lambda m: (layer, 1, 0, 0))]
    args = [dest_tiles, dest_tiles, y_slots, w_tok, hp, x1, ws_gate, ws_up, ws_down, mods, ln_g, ln_b]
    out_specs = [pl.BlockSpec((tm, D), row)]
    out_shape = [jax.ShapeDtypeStruct((NT, D), F32)]
    if has_next:
        in_specs += [_mod_spec(layer + 1, 0, tm), _mod_spec(layer + 1, 1, tm)]
        args += [mods, mods]
        out_specs.append(pl.BlockSpec((tm, D), row))
        out_shape.append(jax.ShapeDtypeStruct((NT, D), BF16))
    res = pl.pallas_call(
        functools.partial(_combine_kernel, has_next),
        grid=(ntile,),
        in_specs=in_specs,
        out_specs=out_specs,
        out_shape=out_shape,
        scratch_shapes=[pltpu.VMEM((2, TOP_K * tm * Y_P, 128), F32), pltpu.VMEM((tm, D), F32),
                        pltpu.SemaphoreType.DMA((2,))],
        compiler_params=_cparams(("arbitrary",)),
        name="moe_combine",
    )(*args)
    return (res[0], res[1]) if has_next else (res[0], None)


def kernel(x_prompt, x_sample, cache_diff_k, cache_diff_v, cache_na_k, cache_na_v, state_ret, c, c_ctx, ada_w, ada_b, ln_g, ln_b, diff_w_in, diff_w_out, diff_lambda, diff_subln_g, na_w_in, na_w_out, na_rpb, ret_w_in, ret_w_out, ret_decay, moe_router, moe_router_bias, moe_w_gate, moe_w_up, moe_w_down, shared_w_gate, shared_w_up, shared_w_down):
    x = jnp.concatenate([x_prompt.reshape(NP, D), x_sample.reshape(NS, D)], axis=0)
    cond = jnp.concatenate([c_ctx[None], c, jnp.zeros((N_COND - 1 - DEC_BATCH, D), F32)], axis=0)
    mods = modulation_table(cond, ada_w, ada_b)
    ln_g4 = ln_g.reshape(DEPTH, 2, 1, D)
    ln_b4 = ln_b.reshape(DEPTH, 2, 1, D)
    cos, sin = _rope_tables()
    h = modulate(x, mods, 0)
    new_dk, new_dv, new_nk, new_nv, new_rs = [], [], [], [], []
    for i in range(DEPTH):
        j = i // N_MIXERS
        kind = i % N_MIXERS
        if kind == 0:
            qp = matmul(h, diff_w_in, j, 0, D, 0, NP, BF16)
            kp = matmul(h, diff_w_in, j, D, D, 0, NP, F32)
            vp = matmul(h, diff_w_in, j, 2 * D, D, 0, NP, F32)
            qs = rope(matmul(h, diff_w_in, j, 0, D, NP, NS, F32), cos, sin)
            ks = rope(matmul(h, diff_w_in, j, D, D, NP, NS, F32), cos, sin)
            vs = matmul(h, diff_w_in, j, 2 * D, D, NP, NS, BF16)
            lam_init = 0.8 - 0.6 * math.exp(-0.3 * i)
            op = diff_attention(qp, kp, vp, diff_lambda[j], diff_subln_g[j], lam_init, BATCH, SEQ, SEQ, SEQ)
            k_all = jnp.concatenate([cache_diff_k[:, j].reshape(DEC_BATCH, PAST, D).astype(BF16),
                                     ks.reshape(DEC_BATCH, DEC_SEQ, D)], axis=1).reshape(-1, D)
            v_all = jnp.concatenate([cache_diff_v[:, j].reshape(DEC_BATCH, PAST, D).astype(BF16),
                                     vs.reshape(DEC_BATCH, DEC_SEQ, D)], axis=1).reshape(-1, D)
            os_ = diff_attention(qs, k_all, v_all, diff_lambda[j], diff_subln_g[j], lam_init,
                                 DEC_BATCH, DEC_SEQ, PAST + DEC_SEQ, 256)
            new_dk.append(kp.reshape(BATCH, SEQ, DIFF_HEADS, 2 * DIFF_QK))
            new_dv.append(vp.reshape(BATCH, SEQ, DIFF_HEADS, DIFF_V))
            w_out = diff_w_out[j]
        elif kind == 1:
            qp = matmul(h, na_w_in, j, 0, D, 0, NP, BF16)
            kp = matmul(h, na_w_in, j, D, D, 0, NP, F32)
            vp = matmul(h, na_w_in, j, 2 * D, D, 0, NP, F32)
            qs = matmul(h, na_w_in, j, 0, D, NP, NS, BF16)
            ks = matmul(h, na_w_in, j, D, D, NP, NS, BF16)
            vs = matmul(h, na_w_in, j, 2 * D, D, NP, NS, BF16)
            op = na_context_attention(qp, kp, vp)
            os_ = na_latent_attention(qs, ks, vs,
                                      cache_na_k[:, j].reshape(DEC_BATCH, PAST, D).astype(BF16),
                                      cache_na_v[:, j].reshape(DEC_BATCH, PAST, D).astype(BF16),
                                      _na_bias_tables(na_rpb[j]))
            new_nk.append(kp.reshape(BATCH, SEQ, NA_HEADS, NA_DH))
            new_nv.append(vp.reshape(BATCH, SEQ, NA_HEADS, NA_DH))
            w_out = na_w_out[j]
        else:
            qd = RET_HEADS * RET_QK
            vd = RET_HEADS * RET_V
            log_g = -jax.nn.softplus(-ret_decay[j].astype(F32))
            outs = []
            for row0, nrows, nb, l in ((0, NP, BATCH, SEQ), (NP, NS, DEC_BATCH, DEC_SEQ)):
                q_ = matmul(h, ret_w_in, j, 0, qd, row0, nrows, BF16)
                k_ = matmul(h, ret_w_in, j, qd, qd, row0, nrows, F32)
                v_ = matmul(h, ret_w_in, j, 2 * qd, vd, row0, nrows, BF16)
                g_ = matmul(h, ret_w_in, j, 2 * qd + vd, vd, row0, nrows, F32)
                if row0 == 0:
                    o_, st = retention(q_, k_, v_, g_, log_g, nb, l, None, True)
                    new_rs.append(st)
                else:
                    o_ = retention(q_, k_, v_, g_, log_g, nb, l, state_ret[:, j], False)
                outs.append(o_)
            op, os_ = outs
            w_out = ret_w_out[j]
        if w_out.shape[0] == D:
            x1, hp, e_t, w_t, rank_t, cnt = outproj_ln_split(
                op, os_, w_out.astype(BF16), x, mods, ln_g4, ln_b4,
                moe_router[i].T, moe_router_bias[i].astype(F32), i)
        else:
            x1, hp, e_t, w_t, rank_t, cnt = outproj_ln(
                jnp.concatenate([op, os_], axis=0), w_out.astype(BF16), x, mods, ln_g4, ln_b4,
                moe_router[i].T, moe_router_bias[i].astype(F32), i)
        dest_tiles, block_e, first, nused, zero_start, nxt_e, par = _slot_plan(e_t, rank_t, cnt)
        xs = moe_dispatch(hp, dest_tiles, zero_start, nused)
        y_slots = moe_experts(xs, block_e, first, nused, nxt_e, par, moe_w_gate, moe_w_up, moe_w_down, i)
        x, h = moe_combine(y_slots, dest_tiles, w_t.T, hp, x1,
                           shared_w_gate[i].astype(BF16), shared_w_up[i].astype(BF16),
                           shared_w_down[i].astype(BF16), mods, ln_g4, ln_b4, i)
    return (x[:NP].reshape(BATCH, SEQ, D), x[NP:].reshape(DEC_BATCH, DEC_SEQ, D),
            jnp.stack(new_dk, axis=1), jnp.stack(new_dv, axis=1),
            jnp.stack(new_nk, axis=1), jnp.stack(new_nv, axis=1), jnp.stack(new_rs, axis=1))
```

```python
import functools
import math

import jax
import jax.numpy as jnp
import numpy as np
from jax import lax
from jax.experimental import pallas as pl
from jax.experimental.pallas import tpu as pltpu

F32 = jnp.float32
BF16 = jnp.bfloat16
U32 = jnp.uint32
I32 = jnp.int32

D = 2048
BATCH = 32
SEQ = 256
DEPTH = 4
DEC_BATCH = 8
DEC_SEQ = 1024
PAST = 512
NP = BATCH * SEQ
NS = DEC_BATCH * DEC_SEQ
NT = NP + NS
GRID_W = 64
N_MIXERS = 3
DIFF_HEADS = 8
DIFF_QK = 128
DIFF_V = 256
NA_HEADS = 16
NA_DH = 128
NA_KH = 8
NA_KH_MAX = 8
NA_KW = 16
RET_HEADS = 8
RET_QK = 256
RET_V = 512
RET_CHUNK = 128
N_EXPERTS = 64
N_GROUPS = 8
EXPERTS_PER_GROUP = 8
TOPK_GROUPS = 4
TOP_K = 6
D_EXPERT = 512
D_SHARED = 512
ROUTED_SCALE = 2.5
ROPE_BASE = 10000.0
LN_EPS = 1e-5
ALPHA = (2 * DEPTH) ** 0.25
N_COND = 16

VMEM_LIMIT = 56 * 1024 * 1024

MOE_TB = 256
MOE_NBLK = (NT * TOP_K) // MOE_TB + N_EXPERTS
MOE_CAP = MOE_NBLK * MOE_TB
RT_TM = 512
CMB_TM = 256
PK_S = 8
PK_L = D // 2 // PK_S
Y_S = D // 128
PK_P = PK_S + 1
Y_P = Y_S + 1
HI_MASK = np.uint32(0xFFFF0000)


def _cparams(sem):
    return pltpu.CompilerParams(dimension_semantics=sem, vmem_limit_bytes=VMEM_LIMIT)


def _cond_row(row0):
    return jnp.where(row0 < NP, 0, 1 + (row0 - NP) // DEC_SEQ)


def _mod_spec(layer, chunk, tm, moff=0):
    return pl.BlockSpec((None, None, None, 1, D),
                        lambda m, *_: (layer, _cond_row((m + moff) * tm), chunk, 0, 0))


def _sigmoid(x):
    return 1.0 / (1.0 + jnp.exp(-x))


def _silu(x):
    return x * _sigmoid(x)


def _mod_kernel(cond_ref, w_ref, b_ref, o_ref):
    c = _silu(cond_ref[...]).astype(BF16)
    o_ref[...] = jnp.dot(c, w_ref[...].astype(BF16), preferred_element_type=F32) + b_ref[...]


def modulation_table(cond, ada_w, ada_b):
    tn = 1024
    n6 = 6 * D
    out = pl.pallas_call(
        _mod_kernel,
        grid=(DEPTH, n6 // tn),
        in_specs=[pl.BlockSpec((N_COND, D), lambda l, n: (0, 0)),
                  pl.BlockSpec((None, D, tn), lambda l, n: (l, 0, n)),
                  pl.BlockSpec((None, 1, tn), lambda l, n: (l, 0, n))],
        out_specs=pl.BlockSpec((None, N_COND, tn), lambda l, n: (l, 0, n)),
        out_shape=jax.ShapeDtypeStruct((DEPTH, N_COND, n6), F32),
        compiler_params=_cparams(("arbitrary", "arbitrary")),
        name="modulation_table",
    )(cond, ada_w, ada_b.reshape(DEPTH, 1, n6))
    return out.reshape(DEPTH, N_COND, 6, 1, D)


def _modulate_kernel(x_ref, sh_ref, sc_ref, o_ref):
    o_ref[...] = (x_ref[...] * (1.0 + sc_ref[...]) + sh_ref[...]).astype(o_ref.dtype)


def modulate(x, mods, layer):
    tm = 512
    return pl.pallas_call(
        _modulate_kernel,
        grid=(NT // tm,),
        in_specs=[pl.BlockSpec((tm, D), lambda m: (m, 0)),
                  _mod_spec(layer, 0, tm), _mod_spec(layer, 1, tm)],
        out_specs=pl.BlockSpec((tm, D), lambda m: (m, 0)),
        out_shape=jax.ShapeDtypeStruct((NT, D), BF16),
        compiler_params=_cparams(("arbitrary",)),
        name="modulate",
    )(x, mods, mods)


def _mm_kernel(a_ref, w_ref, o_ref, wb_ref):
    @pl.when(pl.program_id(1) == 0)
    def _():
        wb_ref[...] = w_ref[...].astype(BF16)

    o_ref[...] = jnp.dot(a_ref[...], wb_ref[...], preferred_element_type=F32).astype(o_ref.dtype)


def matmul(a, w, layer, col0, ncols, row0, nrows, out_dtype):
    k = a.shape[1]
    tm, tn = 1024, 1024
    moff = row0 // tm
    joff = col0 // tn
    return pl.pallas_call(
        _mm_kernel,
        grid=(ncols // tn, nrows // tm),
        in_specs=[pl.BlockSpec((tm, k), lambda j, m: (m + moff, 0)),
                  pl.BlockSpec((None, k, tn), lambda j, m: (layer, 0, j + joff))],
        out_specs=pl.BlockSpec((tm, tn), lambda j, m: (m, j)),
        out_shape=jax.ShapeDtypeStruct((nrows, ncols), out_dtype),
        scratch_shapes=[pltpu.VMEM((k, tn), BF16)],
        compiler_params=_cparams(("arbitrary", "arbitrary")),
        name="in_proj",
    )(a, w)


def _rope_tables():
    half = DIFF_QK // 2
    t = jnp.arange(DEC_SEQ)
    inv = ROPE_BASE ** (-jnp.arange(0, half, 2, dtype=F32) / half)

    def cs(pos):
        ang = pos.astype(F32)[:, None] * inv[None, :]
        return jnp.cos(ang), jnp.sin(ang)

    cr, sr = cs(t // GRID_W)
    cc, sc = cs(t % GRID_W)
    cos = jnp.concatenate([cr, cr, cc, cc], axis=-1)
    sin = jnp.concatenate([-sr, sr, -sc, sc], axis=-1)
    return cos, sin


def _rope_kernel(x_ref, cos_ref, sin_ref, o_ref):
    cos = cos_ref[...]
    sin = sin_ref[...]
    lane = lax.broadcasted_iota(jnp.int32, cos.shape, 1)
    first = (lane % (DIFF_QK // 2)) < (DIFF_QK // 4)
    for g in range(D // DIFF_QK):
        xg = x_ref[:, g * DIFF_QK:(g + 1) * DIFF_QK]
        sw = jnp.where(first, pltpu.roll(xg, DIFF_QK - DIFF_QK // 4, 1), pltpu.roll(xg, DIFF_QK // 4, 1))
        o_ref[:, g * DIFF_QK:(g + 1) * DIFF_QK] = (xg * cos + sw * sin).astype(o_ref.dtype)


def rope(x, cos, sin):
    tm = 512
    nt = DEC_SEQ // tm
    return pl.pallas_call(
        _rope_kernel,
        grid=(NS // tm,),
        in_specs=[pl.BlockSpec((tm, D), lambda m: (m, 0)),
                  pl.BlockSpec((tm, DIFF_QK), lambda m: (m % nt, 0)),
                  pl.BlockSpec((tm, DIFF_QK), lambda m: (m % nt, 0))],
        out_specs=pl.BlockSpec((tm, D), lambda m: (m, 0)),
        out_shape=jax.ShapeDtypeStruct((NS, D), BF16),
        compiler_params=_cparams(("arbitrary",)),
        name="rope",
    )(x, cos, sin)


def _qkt(q, k):
    return lax.dot_general(q, k, (((1,), (1,)), ((), ())), preferred_element_type=F32)


def _softmax(s):
    m = jnp.max(s, axis=-1, keepdims=True)
    p = jnp.exp(s - m)
    return p / jnp.sum(p, axis=-1, keepdims=True)


def _softmax_parts(parts):
    m = jnp.max(parts[0], axis=-1, keepdims=True)
    for s in parts[1:]:
        m = jnp.maximum(m, jnp.max(s, axis=-1, keepdims=True))
    ps = [jnp.exp(s - m) for s in parts]
    den = jnp.sum(ps[0], axis=-1, keepdims=True)
    for p in ps[1:]:
        den = den + jnp.sum(p, axis=-1, keepdims=True)
    return [p / den for p in ps]


def _diff_kernel(lam_init, nsrc, lamp_ref, g_ref, q_ref, *refs):
    kv_refs, o_ref = refs[:2 * nsrc], refs[2 * nsrc]
    lp = lamp_ref[...]
    lam = (jnp.exp(jnp.sum(lp[0:1] * lp[1:2], axis=-1, keepdims=True))
           - jnp.exp(jnp.sum(lp[2:3] * lp[3:4], axis=-1, keepdims=True)) + lam_init)
    scale = DIFF_QK ** -0.5
    g = g_ref[...]
    for h in range(DIFF_HEADS):
        c0 = h * DIFF_V
        q = q_ref[:, c0:c0 + DIFF_V].astype(BF16)
        ks = [kv_refs[2 * i][:, c0:c0 + DIFF_V].astype(BF16) for i in range(nsrc)]
        a1 = _softmax_parts([_qkt(q[:, :DIFF_QK], k[:, :DIFF_QK]) * scale for k in ks])
        a2 = _softmax_parts([_qkt(q[:, DIFF_QK:], k[:, DIFF_QK:]) * scale for k in ks])
        o = None
        for i in range(nsrc):
            a = (a1[i] - lam * a2[i]).astype(BF16)
            v = kv_refs[2 * i + 1][:, c0:c0 + DIFF_V].astype(BF16)
            part = jnp.dot(a, v, preferred_element_type=F32)
            o = part if o is None else o + part
        o = o * lax.rsqrt(jnp.mean(jnp.square(o), axis=-1, keepdims=True) + LN_EPS) * g
        o_ref[:, c0:c0 + DIFF_V] = (o * (1.0 - lam_init)).astype(o_ref.dtype)


def diff_attention(q, kv_srcs, lam_p, subln_g, lam_init, nb, lq, tq):
    nq = lq // tq
    in_specs = [pl.BlockSpec((4, DIFF_QK), lambda b, i: (0, 0)),
                pl.BlockSpec((1, DIFF_V), lambda b, i: (0, 0)),
                pl.BlockSpec((tq, D), lambda b, i: (b * nq + i, 0))]
    args = [lam_p, subln_g.reshape(1, DIFF_V), q]
    for k, v, spec in kv_srcs:
        in_specs += [spec, spec]
        args += [k, v]
    return pl.pallas_call(
        functools.partial(_diff_kernel, lam_init, len(kv_srcs)),
        grid=(nb, nq),
        in_specs=in_specs,
        out_specs=pl.BlockSpec((tq, D), lambda b, i: (b * nq + i, 0)),
        out_shape=jax.ShapeDtypeStruct((nb * lq, D), BF16),
        compiler_params=_cparams(("arbitrary", "arbitrary")),
        name="diff_attention",
    )(*args)


def _na_ctx_kernel(q_ref, k_ref, v_ref, o_ref):
    scale = NA_DH ** -0.5
    for h in range(NA_HEADS):
        c0 = h * NA_DH
        q = q_ref[:, c0:c0 + NA_DH].astype(BF16)
        k = k_ref[:, c0:c0 + NA_DH].astype(BF16)
        v = v_ref[:, c0:c0 + NA_DH].astype(BF16)
        p = _softmax(_qkt(q, k) * scale).astype(BF16)
        o_ref[:, c0:c0 + NA_DH] = jnp.dot(p, v, preferred_element_type=F32).astype(o_ref.dtype)


def na_context_attention(q, k, v):
    spec = pl.BlockSpec((SEQ, D), lambda b: (b, 0))
    return pl.pallas_call(
        _na_ctx_kernel,
        grid=(BATCH,),
        in_specs=[spec, spec, spec],
        out_specs=spec,
        out_shape=jax.ShapeDtypeStruct((NP, D), BF16),
        compiler_params=_cparams(("arbitrary",)),
        name="na_context_attention",
    )(q, k, v)


NA_ROWS = DEC_SEQ // GRID_W
NA_NLOC = NA_KH * GRID_W


def _na_window_row(r):
    return jnp.clip(r - NA_KH // 2, 0, NA_ROWS - NA_KH)


def _na_bias_tables(rpb):
    col = jnp.arange(GRID_W)
    col_start = jnp.clip(col - NA_KW // 2, 0, GRID_W - NA_KW)
    col_ok = (col[None, :] >= col_start[:, None]) & (col[None, :] < col_start[:, None] + NA_KW)
    dc_idx = jnp.clip(col[None, :] - col[:, None] + NA_KW - 1, 0, 2 * NA_KW - 2)
    hot = (dc_idx[:, :, None] == jnp.arange(2 * NA_KW - 1)[None, None, :]).astype(F32)
    full = jnp.einsum('hrd,qkd->hqrk', rpb.astype(F32), hot, precision=lax.Precision.HIGHEST)
    full = jnp.where(col_ok[None, :, None, :], full, -jnp.inf)
    full = full.reshape(NA_HEADS, GRID_W, (2 * NA_KH_MAX - 1) * GRID_W)
    tabs = [full[:, :, (NA_KH_MAX - 1 - o) * GRID_W:(NA_KH_MAX - 1 - o) * GRID_W + NA_NLOC]
            for o in range(NA_KH)]
    return jnp.stack(tabs)


def _na_lat_kernel(q_ref, k_ref, v_ref, kc_ref, vc_ref, bias_ref, o_ref, pctx_ref, oloc_ref):
    scale = NA_DH ** -0.5
    s_ctx_all = _qkt(q_ref[...], kc_ref[...]) * scale
    for r in range(NA_ROWS):
        r0 = min(max(r - NA_KH // 2, 0), NA_ROWS - NA_KH)
        rows = slice(r * GRID_W, (r + 1) * GRID_W)
        win = slice(r0 * GRID_W, r0 * GRID_W + NA_NLOC)
        s_loc = _qkt(q_ref[rows, :], k_ref[win, :]) * scale + bias_ref[r - r0]
        s_ctx = s_ctx_all[rows, :]
        m = jnp.maximum(jnp.max(s_loc, axis=-1, keepdims=True), jnp.max(s_ctx, axis=-1, keepdims=True))
        p_loc = jnp.exp(s_loc - m)
        p_ctx = jnp.exp(s_ctx - m)
        den = jnp.sum(p_loc, axis=-1, keepdims=True) + jnp.sum(p_ctx, axis=-1, keepdims=True)
        oloc_ref[rows, :] = jnp.dot((p_loc / den).astype(BF16), v_ref[win, :], preferred_element_type=F32)
        pctx_ref[rows, :] = (p_ctx / den).astype(BF16)
    o = oloc_ref[...] + jnp.dot(pctx_ref[...], vc_ref[...], preferred_element_type=F32)
    o_ref[...] = o.astype(o_ref.dtype)


def na_latent_attention(q, k, v, k_ctx, v_ctx, bias_tabs):
    head = pl.BlockSpec((DEC_SEQ, NA_DH), lambda b, h: (b, h))
    ctx = pl.BlockSpec((None, PAST, NA_DH), lambda b, h: (b, 0, h))
    return pl.pallas_call(
        _na_lat_kernel,
        grid=(DEC_BATCH, NA_HEADS),
        in_specs=[head, head, head, ctx, ctx,
                  pl.BlockSpec((NA_KH, None, GRID_W, NA_NLOC), lambda b, h: (0, h, 0, 0))],
        out_specs=head,
        out_shape=jax.ShapeDtypeStruct((NS, D), BF16),
        scratch_shapes=[pltpu.VMEM((DEC_SEQ, PAST), BF16), pltpu.VMEM((DEC_SEQ, NA_DH), F32)],
        compiler_params=_cparams(("arbitrary", "arbitrary")),
        name="na_latent_attention",
    )(q, k, v, k_ctx, v_ctx, bias_tabs)


def _ret_kernel(nc, has_state, emit_state, logg_ref, q_ref, k_ref, v_ref, g_ref, *rest):
    rest = list(rest)
    s0_ref = rest.pop(0) if has_state else None
    o_ref = rest.pop(0)
    st_ref = rest.pop(0) if emit_state else None
    oacc, sacc = rest
    h = pl.program_id(1)
    lgf = logg_ref[0, h]
    lgb = logg_ref[1, h]
    c = RET_CHUNK
    pos = lax.broadcasted_iota(jnp.int32, (c, 1), 0).astype(F32)
    diff = (lax.broadcasted_iota(jnp.int32, (c, c), 0) - lax.broadcasted_iota(jnp.int32, (c, c), 1)).astype(F32)
    dmask_f = jnp.where(diff >= 0, jnp.exp(jnp.maximum(diff, 0.0) * lgf), 0.0)
    dmask_b = jnp.where(diff <= 0, jnp.exp(jnp.maximum(-diff, 0.0) * lgb), 0.0)
    qdec_f = jnp.exp((pos + 1.0) * lgf)
    kdec_f = jnp.exp((c - 1.0 - pos) * lgf)
    qdec_b = jnp.exp((c - pos) * lgb)
    kdec_b = jnp.exp(pos * lgb)
    cd_f = jnp.exp(jnp.full((1, 1), c, F32) * lgf)
    cd_b = jnp.exp(jnp.full((1, 1), c, F32) * lgb)

    def chunk(i):
        sl = slice(i * c, (i + 1) * c)
        return q_ref[sl, :].astype(BF16), k_ref[sl, :] * (RET_QK ** -0.5), v_ref[sl, :].astype(BF16)

    def state_update(kc, kdec, vb, cd):
        kd = (kc * kdec).T.astype(BF16)
        sacc[...] = sacc[...] * cd + jnp.dot(kd, vb, preferred_element_type=F32)

    if has_state:
        sacc[...] = s0_ref[0]
    else:
        sacc[...] = jnp.zeros_like(sacc)
    for i in range(nc):
        qb, kc, vb = chunk(i)
        qk = _qkt(qb, kc.astype(BF16))
        o = (jnp.dot((qk * dmask_f).astype(BF16), vb, preferred_element_type=F32)
             + jnp.dot((qk * dmask_b).astype(BF16), vb, preferred_element_type=F32)
             + jnp.dot(qb, sacc[...].astype(BF16), preferred_element_type=F32) * qdec_f)
        oacc[i * c:(i + 1) * c, :] = o
        state_update(kc, kdec_f, vb, cd_f)
    if emit_state:
        st_ref[0] = sacc[...]

    if has_state:
        sacc[...] = s0_ref[1]
    else:
        sacc[...] = jnp.zeros_like(sacc)
    for i in reversed(range(nc)):
        qb, kc, vb = chunk(i)
        oacc[i * c:(i + 1) * c, :] += jnp.dot(qb, sacc[...].astype(BF16), preferred_element_type=F32) * qdec_b
        state_update(kc, kdec_b, vb, cd_b)
    if emit_state:
        st_ref[1] = sacc[...]

    o = oacc[...]
    mu = jnp.mean(o, axis=-1, keepdims=True)
    var = jnp.mean(jnp.square(o - mu), axis=-1, keepdims=True)
    o = (o - mu) * lax.rsqrt(var + LN_EPS)
    o_ref[...] = (o * _silu(g_ref[...])).astype(o_ref.dtype)


def retention(q, k, v, g, log_g, nb, l, state=None, emit_state=False):
    nc = l // RET_CHUNK
    has_state = state is not None
    in_specs = [pl.BlockSpec(memory_space=pltpu.SMEM),
                pl.BlockSpec((l, RET_QK), lambda b, h: (b, h)),
                pl.BlockSpec((l, RET_QK), lambda b, h: (b, h)),
                pl.BlockSpec((l, RET_V), lambda b, h: (b, h)),
                pl.BlockSpec((l, RET_V), lambda b, h: (b, h))]
    args = [log_g, q, k, v, g]
    st_spec = pl.BlockSpec((None, 2, None, RET_QK, RET_V), lambda b, h: (b, 0, h, 0, 0))
    if has_state:
        in_specs.append(st_spec)
        args.append(state)
    out_specs = [pl.BlockSpec((l, RET_V), lambda b, h: (b, h))]
    out_shape = [jax.ShapeDtypeStruct((nb * l, RET_HEADS * RET_V), BF16)]
    if emit_state:
        out_specs.append(st_spec)
        out_shape.append(jax.ShapeDtypeStruct((nb, 2, RET_HEADS, RET_QK, RET_V), F32))
    res = pl.pallas_call(
        functools.partial(_ret_kernel, nc, has_state, emit_state),
        grid=(nb, RET_HEADS),
        in_specs=in_specs,
        out_specs=out_specs,
        out_shape=out_shape,
        scratch_shapes=[pltpu.VMEM((l, RET_V), F32), pltpu.VMEM((RET_QK, RET_V), F32)],
        compiler_params=_cparams(("arbitrary", "arbitrary")),
        name="retention",
    )(*args)
    return res if emit_state else res[0]


def _layer_norm(z, g, b):
    mu = jnp.mean(z, axis=-1, keepdims=True)
    var = jnp.mean(jnp.square(z - mu), axis=-1, keepdims=True)
    return (z - mu) * lax.rsqrt(var + LN_EPS) * g + b


def _pack_rows(hn, hp_ref):
    tm = hn.shape[0]
    half = D // 2
    lo = lax.bitcast_convert_type(hn[:, :half].astype(BF16).astype(F32), U32)
    hi = lax.bitcast_convert_type(hn[:, half:].astype(BF16).astype(F32), U32)
    w = (lo >> 16) | (hi & HI_MASK)
    for s in range(PK_S):
        hp_ref[pl.ds(s, tm, stride=PK_P), :] = w[:, s * PK_L:(s + 1) * PK_L]
    hp_ref[pl.ds(PK_S, tm, stride=PK_P), :] = jnp.zeros((tm, PK_L), U32)


def _unpack_rows(hp_ref, tm):
    lo, hi = [], []
    for s in range(PK_S):
        w = hp_ref[pl.ds(s, tm, stride=PK_P), :]
        lo.append(lax.bitcast_convert_type(w << 16, F32).astype(BF16))
        hi.append(lax.bitcast_convert_type(w & HI_MASK, F32).astype(BF16))
    return jnp.concatenate(lo + hi, axis=1)


def _route_tile(hn, wrt_ref, rb_ref, e_ref, w_ref, rank_ref, cnt_ref):
    tm = hn.shape[0]
    neg = -jnp.inf
    hb = hn.astype(BF16)
    hl = (hn - hb.astype(F32)).astype(BF16)
    logits = _qkt(wrt_ref[0], hb) + _qkt(wrt_ref[0], hl) + _qkt(wrt_ref[1], hb)
    s = _sigmoid(logits)
    biased = s + rb_ref[...]
    io8 = lax.broadcasted_iota(I32, (EXPERTS_PER_GROUP, tm), 0).astype(F32)
    slabs, gscore = [], []
    for g in range(N_GROUPS):
        slab = biased[g * EXPERTS_PER_GROUP:(g + 1) * EXPERTS_PER_GROUP, :]
        m1 = jnp.max(slab, axis=0, keepdims=True)
        i1 = jnp.min(jnp.where(slab == m1, io8, float(EXPERTS_PER_GROUP)), axis=0, keepdims=True)
        m2 = jnp.max(jnp.where(io8 == i1, neg, slab), axis=0, keepdims=True)
        slabs.append(slab)
        gscore.append(m1 + m2)
    masked = []
    for g in range(N_GROUPS):
        ahead = jnp.zeros_like(gscore[g])
        for g2 in range(N_GROUPS):
            if g2 == g:
                continue
            better = (gscore[g2] > gscore[g]) | ((gscore[g2] == gscore[g]) if g2 < g else False)
            ahead = ahead + jnp.where(better, 1.0, 0.0)
        masked.append(jnp.where(ahead < float(TOPK_GROUPS), slabs[g], neg))
    v = jnp.concatenate(masked, axis=0)
    io = lax.broadcasted_iota(I32, (N_EXPERTS, tm), 0).astype(F32)
    idxs, ws, hots = [], [], []
    for _ in range(TOP_K):
        m = jnp.max(v, axis=0, keepdims=True)
        idx = jnp.min(jnp.where(v == m, io, float(N_EXPERTS)), axis=0, keepdims=True)
        hot = io == idx
        ws.append(jnp.sum(jnp.where(hot, s, 0.0), axis=0, keepdims=True))
        v = jnp.where(hot, neg, v)
        idxs.append(idx)
        hots.append(hot)
    wsum = ws[0]
    for k in range(1, TOP_K):
        wsum = wsum + ws[k]
    chosen = jnp.where(hots[0], 1.0, 0.0)
    for k in range(1, TOP_K):
        chosen = chosen + jnp.where(hots[k], 1.0, 0.0)
    upper = jnp.where(lax.broadcasted_iota(I32, (tm, tm), 0) < lax.broadcasted_iota(I32, (tm, tm), 1), 1.0, 0.0)
    before = jnp.dot(chosen.astype(BF16), upper.astype(BF16), preferred_element_type=F32)
    zero_row = jnp.zeros((1, tm), F32)
    for k in range(8):
        if k < TOP_K:
            e_ref[k:k + 1, :] = idxs[k].astype(I32)
            w_ref[k:k + 1, :] = ws[k] / wsum * ROUTED_SCALE
            rank_ref[k:k + 1, :] = jnp.sum(jnp.where(hots[k], before, 0.0), axis=0, keepdims=True).astype(I32)
        else:
            e_ref[k:k + 1, :] = zero_row.astype(I32)
            w_ref[k:k + 1, :] = zero_row
            rank_ref[k:k + 1, :] = zero_row.astype(I32)
    cnt = jnp.sum(chosen, axis=1, keepdims=True)
    cnt_ref[...] = jnp.broadcast_to(cnt, (N_EXPERTS, 128)).astype(I32)


def _outproj_kernel(nk, o_ref, w_ref, x_ref, gate_ref, lng_ref, lnb_ref, sh_ref, sc_ref, wrt_ref, rb_ref,
                    xo_ref, hp_ref, e_ref, wt_ref, rank_ref, cnt_ref, acc_ref):
    kk = pl.program_id(1)
    part = jnp.dot(o_ref[...], w_ref[...], preferred_element_type=F32)

    @pl.when(kk == 0)
    def _():
        acc_ref[...] = part

    @pl.when(kk > 0)
    def _():
        acc_ref[...] += part

    @pl.when(kk == nk - 1)
    def _():
        xn = _layer_norm(ALPHA * x_ref[...] + gate_ref[...] * acc_ref[...], lng_ref[...], lnb_ref[...])
        xo_ref[...] = xn
        hn = xn * (1.0 + sc_ref[...]) + sh_ref[...]
        _pack_rows(hn, hp_ref)
        _route_tile(hn, wrt_ref, rb_ref, e_ref, wt_ref, rank_ref, cnt_ref)


def outproj_ln(o, w_out, x, mods, ln_g, ln_b, w_router_t, b_router, layer):
    k = o.shape[1]
    tk = 2048 if k == 2048 else 1024
    nk = k // tk
    tm = RT_TM
    ntile = NT // tm
    row = lambda m, kk: (m, 0)
    col = lambda m, kk: (0, m)
    return pl.pallas_call(
        functools.partial(_outproj_kernel, nk),
        grid=(ntile, nk),
        in_specs=[pl.BlockSpec((tm, tk), lambda m, kk: (m, kk)),
                  pl.BlockSpec((tk, D), lambda m, kk: (kk, 0),
                               pipeline_mode=pl.Buffered(1) if nk == 1 else None),
                  pl.BlockSpec((tm, D), row),
                  _mod_spec(layer, 2, tm),
                  pl.BlockSpec((None, None, 1, D), lambda m, kk: (layer, 0, 0, 0)),
                  pl.BlockSpec((None, None, 1, D), lambda m, kk: (layer, 0, 0, 0)),
                  _mod_spec(layer, 3, tm), _mod_spec(layer, 4, tm),
                  pl.BlockSpec((2, N_EXPERTS, D), lambda m, kk: (0, 0, 0)),
                  pl.BlockSpec((N_EXPERTS, 1), lambda m, kk: (0, 0))],
        out_specs=[pl.BlockSpec((tm, D), row),
                   pl.BlockSpec((tm * PK_P, PK_L), row),
                   pl.BlockSpec((8, tm), col), pl.BlockSpec((8, tm), col), pl.BlockSpec((8, tm), col),
                   pl.BlockSpec((None, N_EXPERTS, 128), lambda m, kk: (m, 0, 0))],
        out_shape=[jax.ShapeDtypeStruct((NT, D), F32),
                   jax.ShapeDtypeStruct((NT * PK_P, PK_L), U32),
                   jax.ShapeDtypeStruct((8, NT), I32), jax.ShapeDtypeStruct((8, NT), F32),
                   jax.ShapeDtypeStruct((8, NT), I32),
                   jax.ShapeDtypeStruct((ntile, N_EXPERTS, 128), I32)],
        scratch_shapes=[pltpu.VMEM((tm, D), F32)],
        compiler_params=_cparams(("arbitrary", "arbitrary")),
        name="outproj_ln_route",
    )(o, w_out, x, mods, ln_g, ln_b, mods, mods, w_router_t, b_router.reshape(N_EXPERTS, 1))


def _outproj_split_kernel(np_tiles, op_ref, os_ref, w_ref, x_ref, gate_ref, lng_ref, lnb_ref, sh_ref, sc_ref,
                         wrt_ref, rb_ref, xo_ref, hp_ref, e_ref, wt_ref, rank_ref, cnt_ref):
    m = pl.program_id(0)
    o_blk = jnp.where(m < np_tiles, op_ref[...], os_ref[...])
    y = jnp.dot(o_blk, w_ref[...], preferred_element_type=F32)
    xn = _layer_norm(ALPHA * x_ref[...] + gate_ref[...] * y, lng_ref[...], lnb_ref[...])
    xo_ref[...] = xn
    hn = xn * (1.0 + sc_ref[...]) + sh_ref[...]
    _pack_rows(hn, hp_ref)
    _route_tile(hn, wrt_ref, rb_ref, e_ref, wt_ref, rank_ref, cnt_ref)


def outproj_ln_split(op, os_, w_out, x, mods, ln_g, ln_b, w_router_t, b_router, layer):
    tm = RT_TM
    ntile = NT // tm
    np_tiles = NP // tm
    ns_tiles = NS // tm
    row = lambda m: (m, 0)
    col = lambda m: (0, m)
    mod = lambda chunk: _mod_spec(layer, chunk, tm)
    const2 = lambda m: (0, 0)
    return pl.pallas_call(
        functools.partial(_outproj_split_kernel, np_tiles),
        grid=(ntile,),
        in_specs=[pl.BlockSpec((tm, D), lambda m: (jnp.minimum(m, np_tiles - 1), 0)),
                  pl.BlockSpec((tm, D), lambda m: (jnp.clip(m - np_tiles, 0, ns_tiles - 1), 0)),
                  pl.BlockSpec((D, D), const2, pipeline_mode=pl.Buffered(1)),
                  pl.BlockSpec((tm, D), row),
                  mod(2),
                  pl.BlockSpec((None, None, 1, D), lambda m: (layer, 0, 0, 0)),
                  pl.BlockSpec((None, None, 1, D), lambda m: (layer, 0, 0, 0)),
                  mod(3), mod(4),
                  pl.BlockSpec((2, N_EXPERTS, D), lambda m: (0, 0, 0)),
                  pl.BlockSpec((N_EXPERTS, 1), const2)],
        out_specs=[pl.BlockSpec((tm, D), row),
                   pl.BlockSpec((tm * PK_P, PK_L), row),
                   pl.BlockSpec((8, tm), col), pl.BlockSpec((8, tm), col), pl.BlockSpec((8, tm), col),
                   pl.BlockSpec((None, N_EXPERTS, 128), lambda m: (m, 0, 0))],
        out_shape=[jax.ShapeDtypeStruct((NT, D), F32),
                   jax.ShapeDtypeStruct((NT * PK_P, PK_L), U32),
                   jax.ShapeDtypeStruct((8, NT), I32), jax.ShapeDtypeStruct((8, NT), F32),
                   jax.ShapeDtypeStruct((8, NT), I32),
                   jax.ShapeDtypeStruct((ntile, N_EXPERTS, 128), I32)],
        compiler_params=_cparams(("arbitrary",)),
        name="outproj_ln_route_split",
    )(op, os_, w_out, x, mods, ln_g, ln_b, mods, mods, w_router_t, b_router.reshape(N_EXPERTS, 1))


def _slot_plan(e_t, rank_t, cnt):
    cnt = cnt[:, :, 0]
    tile_base = jnp.cumsum(cnt, axis=0) - cnt
    counts = jnp.sum(cnt, axis=0)
    padded = (counts + MOE_TB - 1) // MOE_TB * MOE_TB
    pad_end = jnp.cumsum(padded)
    pad_start = pad_end - padded
    offs = jnp.repeat(pad_start[None, :] + tile_base, NT // cnt.shape[0], axis=0)
    hot = e_t[:TOP_K, :, None] == jnp.arange(N_EXPERTS, dtype=I32)[None, None, :]
    dest = jnp.sum(jnp.where(hot, offs[None], 0), axis=-1) + rank_t[:TOP_K]
    ntile = NT // CMB_TM
    dest_tiles = dest.reshape(TOP_K, ntile, CMB_TM).transpose(1, 0, 2).reshape(ntile, 1, TOP_K * CMB_TM)
    nused = pad_end[-1] // MOE_TB
    blk = jnp.minimum(jnp.arange(MOE_NBLK, dtype=I32), nused - 1) * MOE_TB
    block_e = jnp.minimum(jnp.sum((pad_end[None, :] <= blk[:, None]).astype(I32), axis=1), N_EXPERTS - 1)
    first = jnp.concatenate([jnp.ones((1,), I32), (block_e[1:] != block_e[:-1]).astype(I32)])
    zero_start = pad_start + counts
    ids = jnp.arange(N_EXPERTS, dtype=I32)
    used = counts > 0
    later = jnp.where(used[None, :] & (ids[None, :] > ids[:, None]), ids[None, :], N_EXPERTS)
    nxt = jnp.min(later, axis=1)
    nxt = jnp.where(nxt == N_EXPERTS, ids, nxt)
    par = (jnp.cumsum(used.astype(I32)) - 1) % 2
    owner = block_e[:, None] == ids[None, :]
    nxt_b = jnp.sum(jnp.where(owner, nxt[None, :], 0), axis=1)
    par_b = jnp.sum(jnp.where(owner, par[None, :], 0), axis=1)
    return (dest_tiles.astype(I32), block_e.astype(I32), first, nused.reshape(1).astype(I32),
            zero_start.astype(I32), nxt_b.astype(I32), par_b.astype(I32))


_DISPATCH_ROWS = TOP_K * CMB_TM
_CMB_ROWS = 32


def _dispatch_kernel(zs_ref, nused_ref, dest_ref, hp_ref, xs_hbm, stage, zbuf, zsem, sem):
    i = pl.program_id(0)
    n = pl.num_programs(0)
    slot = i % 2

    @pl.when(i == 0)
    def _():
        zbuf[...] = jnp.zeros_like(zbuf)

        def zcopy(slot0, nslot):
            return pltpu.make_async_copy(zbuf.at[pl.ds(0, nslot * PK_P)],
                                         xs_hbm.at[pl.ds(slot0 * PK_P, nslot * PK_P)], zsem)

        def pad_pieces(e, wait):
            slot0 = zs_ref[e]
            npad = (MOE_TB - slot0 % MOE_TB) % MOE_TB
            piece = MOE_TB // 2
            while piece >= 1:
                @pl.when((npad & piece) != 0)
                def _(slot0=slot0, piece=piece):
                    cp = zcopy(slot0, piece)
                    cp.wait() if wait else cp.start()
                slot0 = slot0 + (npad & piece)
                piece //= 2

        def zstart(e, carry):
            pad_pieces(e, False)
            return carry

        def zwait(e, carry):
            pad_pieces(e, True)
            return carry

        def tstart(b, carry):
            zcopy(b * MOE_TB, MOE_TB).start()
            return carry

        def twait(b, carry):
            zcopy(b * MOE_TB, MOE_TB).wait()
            return carry

        lax.fori_loop(0, N_EXPERTS, zstart, 0)
        lax.fori_loop(nused_ref[0], MOE_NBLK + 1, tstart, 0)
        lax.fori_loop(0, N_EXPERTS, zwait, 0)
        lax.fori_loop(nused_ref[0], MOE_NBLK + 1, twait, 0)

    stage[slot] = hp_ref[...]

    def issue(r, carry):
        for k in range(TOP_K):
            dst = dest_ref[0, 0, k * CMB_TM + r] * PK_P
            pltpu.make_async_copy(stage.at[slot, pl.ds(r * PK_P, PK_P)], xs_hbm.at[pl.ds(dst, PK_P)],
                                  sem.at[slot]).start(priority=k % 2)
        return carry

    lax.fori_loop(0, CMB_TM, issue, 0, unroll=2)

    def drain(s):
        for k in range(TOP_K):
            pltpu.make_async_copy(stage.at[s], xs_hbm.at[pl.ds(0, CMB_TM * PK_P)], sem.at[s]).wait()

    @pl.when(i > 0)
    def _():
        drain(1 - slot)

    @pl.when(i == n - 1)
    def _():
        drain(slot)


def moe_dispatch(hp, dest_tiles, zero_start, nused):
    ntile = NT // CMB_TM
    grid_spec = pltpu.PrefetchScalarGridSpec(
        num_scalar_prefetch=2,
        grid=(ntile,),
        in_specs=[pl.BlockSpec((1, 1, _DISPATCH_ROWS), lambda i, zs, nu: (i, 0, 0), memory_space=pltpu.SMEM),
                  pl.BlockSpec((CMB_TM * PK_P, PK_L), lambda i, zs, nu: (i, 0))],
        out_specs=pl.BlockSpec(memory_space=pl.ANY),
        scratch_shapes=[pltpu.VMEM((2, CMB_TM * PK_P, PK_L), U32),
                        pltpu.VMEM((MOE_TB * PK_P, PK_L), U32),
                        pltpu.SemaphoreType.DMA(()), pltpu.SemaphoreType.DMA((2,))],
    )
    return pl.pallas_call(
        _dispatch_kernel,
        grid_spec=grid_spec,
        out_shape=jax.ShapeDtypeStruct(((MOE_CAP + MOE_TB) * PK_P, PK_L), U32),
        compiler_params=_cparams(("arbitrary",)),
        name="moe_dispatch",
    )(zero_start, nused, dest_tiles, hp)


def _moe_kernel(layer, be_ref, first_ref, nused_ref, nxt_ref, par_ref, xs_ref, wg_hbm, wu_hbm, wd_hbm, y_ref,
                wgs, wus, wds, wgb, wub, wdb, sem):
    i = pl.program_id(0)

    def weight_copies(e, s):
        return (pltpu.make_async_copy(wg_hbm.at[layer, e], wgs.at[s], sem.at[s]),
                pltpu.make_async_copy(wu_hbm.at[layer, e], wus.at[s], sem.at[s]),
                pltpu.make_async_copy(wd_hbm.at[layer, e], wds.at[s], sem.at[s]))

    @pl.when(i == 0)
    def _():
        for cp in weight_copies(be_ref[0], par_ref[0]):
            cp.start()

    @pl.when(first_ref[i] == 1)
    def _():
        e, s = be_ref[i], par_ref[i]
        for cp in weight_copies(e, s):
            cp.wait()

        @pl.when(nxt_ref[i] != e)
        def _():
            for cp in weight_copies(nxt_ref[i], 1 - s):
                cp.start()

        wgb[...] = wgs[s].astype(BF16)
        wub[...] = wus[s].astype(BF16)
        wdb[...] = wds[s].astype(BF16)

    @pl.when(i < nused_ref[0])
    def _():
        x = _unpack_rows(xs_ref, MOE_TB)
        a = (_silu(jnp.dot(x, wgb[...], preferred_element_type=F32))
             * jnp.dot(x, wub[...], preferred_element_type=F32))
        y = jnp.dot(a.astype(BF16), wdb[...], preferred_element_type=F32)
        for j in range(Y_S):
            y_ref[pl.ds(j, MOE_TB, stride=Y_P), :] = y[:, j * 128:(j + 1) * 128]
        y_ref[pl.ds(Y_S, MOE_TB, stride=Y_P), :] = jnp.zeros((MOE_TB, 128), F32)

    @pl.when(i >= nused_ref[0])
    def _():
        y_ref[...] = jnp.zeros_like(y_ref)


def moe_experts(xs, block_e, first, nused, nxt_e, par, w_gate, w_up, w_down, layer):
    blk = lambda i, be, fi, nu, nx, pa: (jnp.minimum(i, nu[0] - 1), 0)
    grid_spec = pltpu.PrefetchScalarGridSpec(
        num_scalar_prefetch=5,
        grid=(MOE_NBLK,),
        in_specs=[pl.BlockSpec((MOE_TB * PK_P, PK_L), blk),
                  pl.BlockSpec(memory_space=pl.ANY), pl.BlockSpec(memory_space=pl.ANY),
                  pl.BlockSpec(memory_space=pl.ANY)],
        out_specs=pl.BlockSpec((MOE_TB * Y_P, 128), lambda i, be, fi, nu, nx, pa: (i, 0)),
        scratch_shapes=[pltpu.VMEM((2, D, D_EXPERT), F32), pltpu.VMEM((2, D, D_EXPERT), F32),
                        pltpu.VMEM((2, D_EXPERT, D), F32),
                        pltpu.VMEM((D, D_EXPERT), BF16), pltpu.VMEM((D, D_EXPERT), BF16),
                        pltpu.VMEM((D_EXPERT, D), BF16),
                        pltpu.SemaphoreType.DMA((2,))],
    )
    return pl.pallas_call(
        functools.partial(_moe_kernel, layer),
        grid_spec=grid_spec,
        out_shape=jax.ShapeDtypeStruct((MOE_CAP * Y_P, 128), F32),
        compiler_params=_cparams(("arbitrary",)),
        name="moe_experts",
    )(block_e, first, nused, nxt_e, par, xs, w_gate, w_up, w_down)


def _combine_kernel(has_next, dest_ref, destn_ref, y_hbm, w_ref, hp_ref, x_ref, wsg_ref, wsu_ref, wsd_ref,
                    gate_ref, lng_ref, lnb_ref, *rest):
    if has_next:
        sh_ref, sc_ref, xo_ref, ho_ref, buf, routed_ref, sem = rest
    else:
        xo_ref, buf, routed_ref, sem = rest
    i = pl.program_id(0)
    n = pl.num_programs(0)
    slot = i % 2
    nrow = TOP_K * CMB_TM

    def gather(dref, s):
        def body(r, carry):
            for k in range(TOP_K):
                src = dref[0, 0, k * CMB_TM + r] * Y_P
                dst = (k * CMB_TM + r) * Y_P
                pltpu.make_async_copy(y_hbm.at[pl.ds(src, Y_S)], buf.at[s, pl.ds(dst, Y_S)],
                                      sem.at[s]).start(priority=k % 2)
            return carry
        lax.fori_loop(0, CMB_TM, body, 0, unroll=2)

    @pl.when(i == 0)
    def _():
        gather(dest_ref, 0)

    @pl.when(i + 1 < n)
    def _():
        gather(destn_ref, 1 - slot)

    hb = _unpack_rows(hp_ref, CMB_TM)
    a = (_silu(jnp.dot(hb, wsg_ref[...], preferred_element_type=F32))
         * jnp.dot(hb, wsu_ref[...], preferred_element_type=F32))
    shared = jnp.dot(a.astype(BF16), wsd_ref[...], preferred_element_type=F32)

    pltpu.make_async_copy(y_hbm.at[pl.ds(0, nrow * Y_S)], buf.at[slot, pl.ds(0, nrow * Y_S)],
                          sem.at[slot]).wait()
    for r0 in range(0, CMB_TM, _CMB_ROWS):
        wsub = w_ref[r0:r0 + _CMB_ROWS, :]
        wb = [jnp.broadcast_to(wsub[:, k:k + 1], (_CMB_ROWS, 128)) for k in range(TOP_K)]
        for j in range(Y_S):
            acc = buf[slot, pl.ds(r0 * Y_P + j, _CMB_ROWS, stride=Y_P), :] * wb[0]
            for k in range(1, TOP_K):
                acc = acc + buf[slot, pl.ds((k * CMB_TM + r0) * Y_P + j, _CMB_ROWS, stride=Y_P), :] * wb[k]
            routed_ref[r0:r0 + _CMB_ROWS, j * 128:(j + 1) * 128] = acc
    routed = routed_ref[...]
    xn = _layer_norm(ALPHA * x_ref[...] + gate_ref[...] * (routed + shared), lng_ref[...], lnb_ref[...])
    xo_ref[...] = xn
    if has_next:
        ho_ref[...] = (xn * (1.0 + sc_ref[...]) + sh_ref[...]).astype(ho_ref.dtype)


def moe_combine(y_slots, dest_tiles, w_tok, hp, x1, ws_gate, ws_up, ws_down, mods, ln_g, ln_b, layer):
    has_next = layer + 1 < DEPTH
    tm = CMB_TM
    ntile = NT // tm
    row = lambda m: (m, 0)
    const = lambda m: (0, 0)
    in_specs = [pl.BlockSpec((1, 1, TOP_K * tm), lambda m: (m, 0, 0), memory_space=pltpu.SMEM),
                pl.BlockSpec((1, 1, TOP_K * tm), lambda m: (jnp.minimum(m + 1, ntile - 1), 0, 0),
                             memory_space=pltpu.SMEM),
                pl.BlockSpec(memory_space=pl.ANY),
                pl.BlockSpec((tm, 8), row),
                pl.BlockSpec((tm * PK_P, PK_L), row), pl.BlockSpec((tm, D), row),
                pl.BlockSpec((D, D_SHARED), const, pipeline_mode=pl.Buffered(1)),
                pl.BlockSpec((D, D_SHARED), const, pipeline_mode=pl.Buffered(1)),
                pl.BlockSpec((D_SHARED, D), const, pipeline_mode=pl.Buffered(1)),
                _mod_spec(layer, 5, tm),
                pl.BlockSpec((None, None, 1, D), lambda m: (layer, 1, 0, 0)),
                pl.BlockSpec((None, None, 1, D), lambda m: (layer, 1, 0, 0))]
    args = [dest_tiles, dest_tiles, y_slots, w_tok, hp, x1, ws_gate, ws_up, ws_down, mods, ln_g, ln_b]
    out_specs = [pl.BlockSpec((tm, D), row)]
    out_shape = [jax.ShapeDtypeStruct((NT, D), F32)]
    if has_next:
        in_specs += [_mod_spec(layer + 1, 0, tm), _mod_spec(layer + 1, 1, tm)]
        args += [mods, mods]
        out_specs.append(pl.BlockSpec((tm, D), row))
        out_shape.append(jax.ShapeDtypeStruct((NT, D), BF16))
    res = pl.pallas_call(
        functools.partial(_combine_kernel, has_next),
        grid=(ntile,),
        in_specs=in_specs,
        out_specs=out_specs,
        out_shape=out_shape,
        scratch_shapes=[pltpu.VMEM((2, TOP_K * tm * Y_P, 128), F32), pltpu.VMEM((tm, D), F32),
                        pltpu.SemaphoreType.DMA((2,))],
        compiler_params=_cparams(("arbitrary",)),
        name="moe_combine",
    )(*args)
    return (res[0], res[1]) if has_next else (res[0], None)


def _split_bf16(w):
    hi = w.astype(BF16)
    lo = (w - hi.astype(F32)).astype(BF16)
    return jnp.stack([hi, lo])


def kernel(x_prompt, x_sample, cache_diff_k, cache_diff_v, cache_na_k, cache_na_v, state_ret, c, c_ctx, ada_w, ada_b, ln_g, ln_b, diff_w_in, diff_w_out, diff_lambda, diff_subln_g, na_w_in, na_w_out, na_rpb, ret_w_in, ret_w_out, ret_decay, moe_router, moe_router_bias, moe_w_gate, moe_w_up, moe_w_down, shared_w_gate, shared_w_up, shared_w_down):
    x = jnp.concatenate([x_prompt.reshape(NP, D), x_sample.reshape(NS, D)], axis=0)
    cond = jnp.concatenate([c_ctx[None], c, jnp.zeros((N_COND - 1 - DEC_BATCH, D), F32)], axis=0)
    mods = modulation_table(cond, ada_w, ada_b)
    ln_g4 = ln_g.reshape(DEPTH, 2, 1, D)
    ln_b4 = ln_b.reshape(DEPTH, 2, 1, D)
    cos, sin = _rope_tables()
    h = modulate(x, mods, 0)
    new_dk, new_dv, new_nk, new_nv, new_rs = [], [], [], [], []
    for i in range(DEPTH):
        j = i // N_MIXERS
        kind = i % N_MIXERS
        if kind == 0:
            qp = matmul(h, diff_w_in, j, 0, D, 0, NP, BF16)
            kp = matmul(h, diff_w_in, j, D, D, 0, NP, F32)
            vp = matmul(h, diff_w_in, j, 2 * D, D, 0, NP, F32)
            qs = rope(matmul(h, diff_w_in, j, 0, D, NP, NS, F32), cos, sin)
            ks = rope(matmul(h, diff_w_in, j, D, D, NP, NS, F32), cos, sin)
            vs = matmul(h, diff_w_in, j, 2 * D, D, NP, NS, BF16)
            lam_init = 0.8 - 0.6 * math.exp(-0.3 * i)
            rows = lambda n: pl.BlockSpec((n, D), lambda b, t: (b, 0))
            op = diff_attention(qp, [(kp, vp, rows(SEQ))], diff_lambda[j], diff_subln_g[j], lam_init,
                                BATCH, SEQ, SEQ)
            ctx = pl.BlockSpec((None, None, PAST, D), lambda b, t, j=j: (b, j, 0, 0))
            os_ = diff_attention(qs, [(cache_diff_k.reshape(DEC_BATCH, -1, PAST, D),
                                       cache_diff_v.reshape(DEC_BATCH, -1, PAST, D), ctx),
                                      (ks, vs, rows(DEC_SEQ))],
                                 diff_lambda[j], diff_subln_g[j], lam_init, DEC_BATCH, DEC_SEQ, 256)
            new_dk.append(kp.reshape(BATCH, SEQ, DIFF_HEADS, 2 * DIFF_QK))
            new_dv.append(vp.reshape(BATCH, SEQ, DIFF_HEADS, DIFF_V))
            w_out = diff_w_out[j]
        elif kind == 1:
            qp = matmul(h, na_w_in, j, 0, D, 0, NP, BF16)
            kp = matmul(h, na_w_in, j, D, D, 0, NP, F32)
            vp = matmul(h, na_w_in, j, 2 * D, D, 0, NP, F32)
            qs = matmul(h, na_w_in, j, 0, D, NP, NS, BF16)
            ks = matmul(h, na_w_in, j, D, D, NP, NS, BF16)
            vs = matmul(h, na_w_in, j, 2 * D, D, NP, NS, BF16)
            op = na_context_attention(qp, kp, vp)
            os_ = na_latent_attention(qs, ks, vs,
                                      cache_na_k[:, j].reshape(DEC_BATCH, PAST, D).astype(BF16),
                                      cache_na_v[:, j].reshape(DEC_BATCH, PAST, D).astype(BF16),
                                      _na_bias_tables(na_rpb[j]))
            new_nk.append(kp.reshape(BATCH, SEQ, NA_HEADS, NA_DH))
            new_nv.append(vp.reshape(BATCH, SEQ, NA_HEADS, NA_DH))
            w_out = na_w_out[j]
        else:
            qd = RET_HEADS * RET_QK
            vd = RET_HEADS * RET_V
            log_g = -jax.nn.softplus(-ret_decay[j].astype(F32))
            outs = []
            for row0, nrows, nb, l in ((0, NP, BATCH, SEQ), (NP, NS, DEC_BATCH, DEC_SEQ)):
                q_ = matmul(h, ret_w_in, j, 0, qd, row0, nrows, BF16)
                k_ = matmul(h, ret_w_in, j, qd, qd, row0, nrows, F32)
                v_ = matmul(h, ret_w_in, j, 2 * qd, vd, row0, nrows, BF16)
                g_ = matmul(h, ret_w_in, j, 2 * qd + vd, vd, row0, nrows, F32)
                if row0 == 0:
                    o_, st = retention(q_, k_, v_, g_, log_g, nb, l, None, True)
                    new_rs.append(st)
                else:
                    o_ = retention(q_, k_, v_, g_, log_g, nb, l, state_ret[:, j], False)
                outs.append(o_)
            op, os_ = outs
            w_out = ret_w_out[j]
        if w_out.shape[0] == D:
            x1, hp, e_t, w_t, rank_t, cnt = outproj_ln_split(
                op, os_, w_out.astype(BF16), x, mods, ln_g4, ln_b4,
                _split_bf16(moe_router[i].T), moe_router_bias[i].astype(F32), i)
        else:
            x1, hp, e_t, w_t, rank_t, cnt = outproj_ln(
                jnp.concatenate([op, os_], axis=0), w_out.astype(BF16), x, mods, ln_g4, ln_b4,
                _split_bf16(moe_router[i].T), moe_router_bias[i].astype(F32), i)
        dest_tiles, block_e, first, nused, zero_start, nxt_e, par = _slot_plan(e_t, rank_t, cnt)
        xs = moe_dispatch(hp, dest_tiles, zero_start, nused)
        y_slots = moe_experts(xs, block_e, first, nused, nxt_e, par, moe_w_gate, moe_w_up, moe_w_down, i)
        x, h = moe_combine(y_slots, dest_tiles, w_t.T, hp, x1,
                           shared_w_gate[i].astype(BF16), shared_w_up[i].astype(BF16),
                           shared_w_down[i].astype(BF16), mods, ln_g4, ln_b4, i)
    return (x[:NP].reshape(BATCH, SEQ, D), x[NP:].reshape(DEC_BATCH, DEC_SEQ, D),
            jnp.stack(new_dk, axis=1), jnp.stack(new_dv, axis=1),
            jnp.stack(new_nk, axis=1), jnp.stack(new_nv, axis=1), jnp.stack(new_rs, axis=1))
```

```python
import functools
import math

import jax
import jax.numpy as jnp
import numpy as np
from jax import lax
from jax.experimental import pallas as pl
from jax.experimental.pallas import tpu as pltpu

F32 = jnp.float32
BF16 = jnp.bfloat16
U32 = jnp.uint32
I32 = jnp.int32

D = 2048
BATCH = 32
SEQ = 256
DEPTH = 4
DEC_BATCH = 8
DEC_SEQ = 1024
PAST = 512
NP = BATCH * SEQ
NS = DEC_BATCH * DEC_SEQ
NT = NP + NS
GRID_W = 64
N_MIXERS = 3
DIFF_HEADS = 8
DIFF_QK = 128
DIFF_V = 256
NA_HEADS = 16
NA_DH = 128
NA_KH = 8
NA_KH_MAX = 8
NA_KW = 16
RET_HEADS = 8
RET_QK = 256
RET_V = 512
RET_CHUNK = 128
N_EXPERTS = 64
N_GROUPS = 8
EXPERTS_PER_GROUP = 8
TOPK_GROUPS = 4
TOP_K = 6
D_EXPERT = 512
D_SHARED = 512
ROUTED_SCALE = 2.5
ROPE_BASE = 10000.0
LN_EPS = 1e-5
ALPHA = (2 * DEPTH) ** 0.25
N_COND = 16

VMEM_LIMIT = 56 * 1024 * 1024

MOE_TB = 256
MOE_NBLK = (NT * TOP_K) // MOE_TB + N_EXPERTS
MOE_CAP = MOE_NBLK * MOE_TB
RT_TM = 512
CMB_TM = 256
PK_S = 8
PK_L = D // 2 // PK_S
Y_S = D // 128
PK_P = PK_S + 1
Y_P = Y_S + 1
HI_MASK = np.uint32(0xFFFF0000)


def _cparams(sem):
    return pltpu.CompilerParams(dimension_semantics=sem, vmem_limit_bytes=VMEM_LIMIT)


def _cond_row(row0):
    return jnp.where(row0 < NP, 0, 1 + (row0 - NP) // DEC_SEQ)


def _mod_spec(layer, chunk, tm, moff=0):
    return pl.BlockSpec((None, None, None, 1, D),
                        lambda m, *_: (layer, _cond_row((m + moff) * tm), chunk, 0, 0))


def _sigmoid(x):
    return 1.0 / (1.0 + jnp.exp(-x))


def _silu(x):
    return x * _sigmoid(x)


def _mod_kernel(cond_ref, w_ref, b_ref, o_ref):
    c = _silu(cond_ref[...]).astype(BF16)
    o_ref[...] = jnp.dot(c, w_ref[...].astype(BF16), preferred_element_type=F32) + b_ref[...]


def modulation_table(cond, ada_w, ada_b):
    tn = 1024
    n6 = 6 * D
    out = pl.pallas_call(
        _mod_kernel,
        grid=(DEPTH, n6 // tn),
        in_specs=[pl.BlockSpec((N_COND, D), lambda l, n: (0, 0)),
                  pl.BlockSpec((None, D, tn), lambda l, n: (l, 0, n)),
                  pl.BlockSpec((None, 1, tn), lambda l, n: (l, 0, n))],
        out_specs=pl.BlockSpec((None, N_COND, tn), lambda l, n: (l, 0, n)),
        out_shape=jax.ShapeDtypeStruct((DEPTH, N_COND, n6), F32),
        compiler_params=_cparams(("arbitrary", "arbitrary")),
        name="modulation_table",
    )(cond, ada_w, ada_b.reshape(DEPTH, 1, n6))
    return out.reshape(DEPTH, N_COND, 6, 1, D)


def _split_rows_specs(tm):
    np_tiles, ns_tiles = NP // tm, NS // tm
    return (pl.BlockSpec((tm, D), lambda m, *_: (jnp.minimum(m, np_tiles - 1), 0)),
            pl.BlockSpec((tm, D), lambda m, *_: (jnp.clip(m - np_tiles, 0, ns_tiles - 1), 0)))


def _modulate_kernel(np_tiles, xp_ref, xs_ref, sh_ref, sc_ref, o_ref):
    x = jnp.where(pl.program_id(0) < np_tiles, xp_ref[...], xs_ref[...])
    o_ref[...] = (x * (1.0 + sc_ref[...]) + sh_ref[...]).astype(o_ref.dtype)


def modulate(xp, xs, mods, layer):
    tm = 512
    return pl.pallas_call(
        functools.partial(_modulate_kernel, NP // tm),
        grid=(NT // tm,),
        in_specs=[*_split_rows_specs(tm), _mod_spec(layer, 0, tm), _mod_spec(layer, 1, tm)],
        out_specs=pl.BlockSpec((tm, D), lambda m: (m, 0)),
        out_shape=jax.ShapeDtypeStruct((NT, D), BF16),
        compiler_params=_cparams(("arbitrary",)),
        name="modulate",
    )(xp, xs, mods, mods)


def _mm_kernel(a_ref, w_ref, o_ref, wb_ref):
    @pl.when(pl.program_id(1) == 0)
    def _():
        wb_ref[...] = w_ref[...].astype(BF16)

    o_ref[...] = jnp.dot(a_ref[...], wb_ref[...], preferred_element_type=F32).astype(o_ref.dtype)


def matmul(a, w, layer, col0, ncols, row0, nrows, out_dtype):
    k = a.shape[1]
    tm, tn = 1024, 1024
    moff = row0 // tm
    joff = col0 // tn
    return pl.pallas_call(
        _mm_kernel,
        grid=(ncols // tn, nrows // tm),
        in_specs=[pl.BlockSpec((tm, k), lambda j, m: (m + moff, 0)),
                  pl.BlockSpec((None, k, tn), lambda j, m: (layer, 0, j + joff))],
        out_specs=pl.BlockSpec((tm, tn), lambda j, m: (m, j)),
        out_shape=jax.ShapeDtypeStruct((nrows, ncols), out_dtype),
        scratch_shapes=[pltpu.VMEM((k, tn), BF16)],
        compiler_params=_cparams(("arbitrary", "arbitrary")),
        name="in_proj",
    )(a, w)


def _rope_tables():
    half = DIFF_QK // 2
    t = jnp.arange(DEC_SEQ)
    inv = ROPE_BASE ** (-jnp.arange(0, half, 2, dtype=F32) / half)

    def cs(pos):
        ang = pos.astype(F32)[:, None] * inv[None, :]
        return jnp.cos(ang), jnp.sin(ang)

    cr, sr = cs(t // GRID_W)
    cc, sc = cs(t % GRID_W)
    cos = jnp.concatenate([cr, cr, cc, cc], axis=-1)
    sin = jnp.concatenate([-sr, sr, -sc, sc], axis=-1)
    return cos, sin


def _rope_kernel(x_ref, cos_ref, sin_ref, o_ref):
    cos = cos_ref[...]
    sin = sin_ref[...]
    lane = lax.broadcasted_iota(jnp.int32, cos.shape, 1)
    first = (lane % (DIFF_QK // 2)) < (DIFF_QK // 4)
    for g in range(D // DIFF_QK):
        xg = x_ref[:, g * DIFF_QK:(g + 1) * DIFF_QK]
        sw = jnp.where(first, pltpu.roll(xg, DIFF_QK - DIFF_QK // 4, 1), pltpu.roll(xg, DIFF_QK // 4, 1))
        o_ref[:, g * DIFF_QK:(g + 1) * DIFF_QK] = (xg * cos + sw * sin).astype(o_ref.dtype)


def rope(x, cos, sin):
    tm = 512
    nt = DEC_SEQ // tm
    return pl.pallas_call(
        _rope_kernel,
        grid=(NS // tm,),
        in_specs=[pl.BlockSpec((tm, D), lambda m: (m, 0)),
                  pl.BlockSpec((tm, DIFF_QK), lambda m: (m % nt, 0)),
                  pl.BlockSpec((tm, DIFF_QK), lambda m: (m % nt, 0))],
        out_specs=pl.BlockSpec((tm, D), lambda m: (m, 0)),
        out_shape=jax.ShapeDtypeStruct((NS, D), BF16),
        compiler_params=_cparams(("arbitrary",)),
        name="rope",
    )(x, cos, sin)


def _qkt(q, k):
    return lax.dot_general(q, k, (((1,), (1,)), ((), ())), preferred_element_type=F32)


def _softmax(s):
    m = jnp.max(s, axis=-1, keepdims=True)
    p = jnp.exp(s - m)
    return p / jnp.sum(p, axis=-1, keepdims=True)


def _softmax_parts(parts):
    m = jnp.max(parts[0], axis=-1, keepdims=True)
    for s in parts[1:]:
        m = jnp.maximum(m, jnp.max(s, axis=-1, keepdims=True))
    ps = [jnp.exp(s - m) for s in parts]
    den = jnp.sum(ps[0], axis=-1, keepdims=True)
    for p in ps[1:]:
        den = den + jnp.sum(p, axis=-1, keepdims=True)
    return [p / den for p in ps]


def _diff_kernel(lam_init, nsrc, lamp_ref, g_ref, q_ref, *refs):
    kv_refs, o_ref = refs[:2 * nsrc], refs[2 * nsrc]
    lp = lamp_ref[...]
    lam = (jnp.exp(jnp.sum(lp[0:1] * lp[1:2], axis=-1, keepdims=True))
           - jnp.exp(jnp.sum(lp[2:3] * lp[3:4], axis=-1, keepdims=True)) + lam_init)
    scale = DIFF_QK ** -0.5
    g = g_ref[...]
    for h in range(DIFF_HEADS):
        c0 = h * DIFF_V
        q = q_ref[:, c0:c0 + DIFF_V].astype(BF16)
        ks = [kv_refs[2 * i][:, c0:c0 + DIFF_V].astype(BF16) for i in range(nsrc)]
        a1 = _softmax_parts([_qkt(q[:, :DIFF_QK], k[:, :DIFF_QK]) * scale for k in ks])
        a2 = _softmax_parts([_qkt(q[:, DIFF_QK:], k[:, DIFF_QK:]) * scale for k in ks])
        o = None
        for i in range(nsrc):
            a = (a1[i] - lam * a2[i]).astype(BF16)
            v = kv_refs[2 * i + 1][:, c0:c0 + DIFF_V].astype(BF16)
            part = jnp.dot(a, v, preferred_element_type=F32)
            o = part if o is None else o + part
        o = o * lax.rsqrt(jnp.mean(jnp.square(o), axis=-1, keepdims=True) + LN_EPS) * g
        o_ref[:, c0:c0 + DIFF_V] = (o * (1.0 - lam_init)).astype(o_ref.dtype)


def diff_attention(q, kv_srcs, lam_p, subln_g, lam_init, nb, lq, tq):
    nq = lq // tq
    in_specs = [pl.BlockSpec((4, DIFF_QK), lambda b, i: (0, 0)),
                pl.BlockSpec((1, DIFF_V), lambda b, i: (0, 0)),
                pl.BlockSpec((tq, D), lambda b, i: (b * nq + i, 0))]
    args = [lam_p, subln_g.reshape(1, DIFF_V), q]
    for k, v, spec in kv_srcs:
        in_specs += [spec, spec]
        args += [k, v]
    return pl.pallas_call(
        functools.partial(_diff_kernel, lam_init, len(kv_srcs)),
        grid=(nb, nq),
        in_specs=in_specs,
        out_specs=pl.BlockSpec((tq, D), lambda b, i: (b * nq + i, 0)),
        out_shape=jax.ShapeDtypeStruct((nb * lq, D), BF16),
        compiler_params=_cparams(("arbitrary", "arbitrary")),
        name="diff_attention",
    )(*args)


def _na_ctx_kernel(q_ref, k_ref, v_ref, o_ref):
    scale = NA_DH ** -0.5
    for h in range(NA_HEADS):
        c0 = h * NA_DH
        q = q_ref[:, c0:c0 + NA_DH].astype(BF16)
        k = k_ref[:, c0:c0 + NA_DH].astype(BF16)
        v = v_ref[:, c0:c0 + NA_DH].astype(BF16)
        p = _softmax(_qkt(q, k) * scale).astype(BF16)
        o_ref[:, c0:c0 + NA_DH] = jnp.dot(p, v, preferred_element_type=F32).astype(o_ref.dtype)


def na_context_attention(q, k, v):
    spec = pl.BlockSpec((SEQ, D), lambda b: (b, 0))
    return pl.pallas_call(
        _na_ctx_kernel,
        grid=(BATCH,),
        in_specs=[spec, spec, spec],
        out_specs=spec,
        out_shape=jax.ShapeDtypeStruct((NP, D), BF16),
        compiler_params=_cparams(("arbitrary",)),
        name="na_context_attention",
    )(q, k, v)


NA_ROWS = DEC_SEQ // GRID_W
NA_NLOC = NA_KH * GRID_W


def _na_window_row(r):
    return jnp.clip(r - NA_KH // 2, 0, NA_ROWS - NA_KH)


def _na_bias_tables(rpb):
    col = jnp.arange(GRID_W)
    col_start = jnp.clip(col - NA_KW // 2, 0, GRID_W - NA_KW)
    col_ok = (col[None, :] >= col_start[:, None]) & (col[None, :] < col_start[:, None] + NA_KW)
    dc_idx = jnp.clip(col[None, :] - col[:, None] + NA_KW - 1, 0, 2 * NA_KW - 2)
    hot = (dc_idx[:, :, None] == jnp.arange(2 * NA_KW - 1)[None, None, :]).astype(F32)
    full = jnp.einsum('hrd,qkd->hqrk', rpb.astype(F32), hot, precision=lax.Precision.HIGHEST)
    full = jnp.where(col_ok[None, :, None, :], full, -jnp.inf)
    full = full.reshape(NA_HEADS, GRID_W, (2 * NA_KH_MAX - 1) * GRID_W)
    tabs = [full[:, :, (NA_KH_MAX - 1 - o) * GRID_W:(NA_KH_MAX - 1 - o) * GRID_W + NA_NLOC]
            for o in range(NA_KH)]
    return jnp.stack(tabs)


def _na_lat_kernel(q_ref, k_ref, v_ref, kc_ref, vc_ref, bias_ref, o_ref, pctx_ref, oloc_ref):
    scale = NA_DH ** -0.5
    s_ctx_all = _qkt(q_ref[...], kc_ref[...]) * scale
    for r in range(NA_ROWS):
        r0 = min(max(r - NA_KH // 2, 0), NA_ROWS - NA_KH)
        rows = slice(r * GRID_W, (r + 1) * GRID_W)
        win = slice(r0 * GRID_W, r0 * GRID_W + NA_NLOC)
        s_loc = _qkt(q_ref[rows, :], k_ref[win, :]) * scale + bias_ref[r - r0]
        s_ctx = s_ctx_all[rows, :]
        m = jnp.maximum(jnp.max(s_loc, axis=-1, keepdims=True), jnp.max(s_ctx, axis=-1, keepdims=True))
        p_loc = jnp.exp(s_loc - m)
        p_ctx = jnp.exp(s_ctx - m)
        den = jnp.sum(p_loc, axis=-1, keepdims=True) + jnp.sum(p_ctx, axis=-1, keepdims=True)
        oloc_ref[rows, :] = jnp.dot((p_loc / den).astype(BF16), v_ref[win, :], preferred_element_type=F32)
        pctx_ref[rows, :] = (p_ctx / den).astype(BF16)
    o = oloc_ref[...] + jnp.dot(pctx_ref[...], vc_ref[...], preferred_element_type=F32)
    o_ref[...] = o.astype(o_ref.dtype)


def na_latent_attention(q, k, v, k_ctx, v_ctx, bias_tabs):
    head = pl.BlockSpec((DEC_SEQ, NA_DH), lambda b, h: (b, h))
    ctx = pl.BlockSpec((None, PAST, NA_DH), lambda b, h: (b, 0, h))
    return pl.pallas_call(
        _na_lat_kernel,
        grid=(DEC_BATCH, NA_HEADS),
        in_specs=[head, head, head, ctx, ctx,
                  pl.BlockSpec((NA_KH, None, GRID_W, NA_NLOC), lambda b, h: (0, h, 0, 0))],
        out_specs=head,
        out_shape=jax.ShapeDtypeStruct((NS, D), BF16),
        scratch_shapes=[pltpu.VMEM((DEC_SEQ, PAST), BF16), pltpu.VMEM((DEC_SEQ, NA_DH), F32)],
        compiler_params=_cparams(("arbitrary", "arbitrary")),
        name="na_latent_attention",
    )(q, k, v, k_ctx, v_ctx, bias_tabs)


def _ret_kernel(nc, has_state, emit_state, logg_ref, q_ref, k_ref, v_ref, g_ref, *rest):
    rest = list(rest)
    s0_ref = rest.pop(0) if has_state else None
    o_ref = rest.pop(0)
    st_ref = rest.pop(0) if emit_state else None
    oacc, sacc = rest
    h = pl.program_id(1)
    lgf = logg_ref[0, h]
    lgb = logg_ref[1, h]
    c = RET_CHUNK
    pos = lax.broadcasted_iota(jnp.int32, (c, 1), 0).astype(F32)
    diff = (lax.broadcasted_iota(jnp.int32, (c, c), 0) - lax.broadcasted_iota(jnp.int32, (c, c), 1)).astype(F32)
    dmask_f = jnp.where(diff >= 0, jnp.exp(jnp.maximum(diff, 0.0) * lgf), 0.0)
    dmask_b = jnp.where(diff <= 0, jnp.exp(jnp.maximum(-diff, 0.0) * lgb), 0.0)
    qdec_f = jnp.exp((pos + 1.0) * lgf)
    kdec_f = jnp.exp((c - 1.0 - pos) * lgf)
    qdec_b = jnp.exp((c - pos) * lgb)
    kdec_b = jnp.exp(pos * lgb)
    cd_f = jnp.exp(jnp.full((1, 1), c, F32) * lgf)
    cd_b = jnp.exp(jnp.full((1, 1), c, F32) * lgb)

    def chunk(i):
        sl = slice(i * c, (i + 1) * c)
        return q_ref[sl, :].astype(BF16), k_ref[sl, :] * (RET_QK ** -0.5), v_ref[sl, :].astype(BF16)

    def state_update(kc, kdec, vb, cd):
        kd = (kc * kdec).T.astype(BF16)
        sacc[...] = sacc[...] * cd + jnp.dot(kd, vb, preferred_element_type=F32)

    if has_state:
        sacc[...] = s0_ref[0]
    else:
        sacc[...] = jnp.zeros_like(sacc)
    for i in range(nc):
        qb, kc, vb = chunk(i)
        qk = _qkt(qb, kc.astype(BF16))
        o = (jnp.dot((qk * dmask_f).astype(BF16), vb, preferred_element_type=F32)
             + jnp.dot((qk * dmask_b).astype(BF16), vb, preferred_element_type=F32)
             + jnp.dot(qb, sacc[...].astype(BF16), preferred_element_type=F32) * qdec_f)
        oacc[i * c:(i + 1) * c, :] = o
        state_update(kc, kdec_f, vb, cd_f)
    if emit_state:
        st_ref[0] = sacc[...]

    if has_state:
        sacc[...] = s0_ref[1]
    else:
        sacc[...] = jnp.zeros_like(sacc)
    for i in reversed(range(nc)):
        qb, kc, vb = chunk(i)
        oacc[i * c:(i + 1) * c, :] += jnp.dot(qb, sacc[...].astype(BF16), preferred_element_type=F32) * qdec_b
        state_update(kc, kdec_b, vb, cd_b)
    if emit_state:
        st_ref[1] = sacc[...]

    o = oacc[...]
    mu = jnp.mean(o, axis=-1, keepdims=True)
    var = jnp.mean(jnp.square(o - mu), axis=-1, keepdims=True)
    o = (o - mu) * lax.rsqrt(var + LN_EPS)
    o_ref[...] = (o * _silu(g_ref[...])).astype(o_ref.dtype)


def retention(q, k, v, g, log_g, nb, l, state=None, emit_state=False):
    nc = l // RET_CHUNK
    has_state = state is not None
    in_specs = [pl.BlockSpec(memory_space=pltpu.SMEM),
                pl.BlockSpec((l, RET_QK), lambda b, h: (b, h)),
                pl.BlockSpec((l, RET_QK), lambda b, h: (b, h)),
                pl.BlockSpec((l, RET_V), lambda b, h: (b, h)),
                pl.BlockSpec((l, RET_V), lambda b, h: (b, h))]
    args = [log_g, q, k, v, g]
    st_spec = pl.BlockSpec((None, 2, None, RET_QK, RET_V), lambda b, h: (b, 0, h, 0, 0))
    if has_state:
        in_specs.append(st_spec)
        args.append(state)
    out_specs = [pl.BlockSpec((l, RET_V), lambda b, h: (b, h))]
    out_shape = [jax.ShapeDtypeStruct((nb * l, RET_HEADS * RET_V), BF16)]
    if emit_state:
        out_specs.append(st_spec)
        out_shape.append(jax.ShapeDtypeStruct((nb, 2, RET_HEADS, RET_QK, RET_V), F32))
    res = pl.pallas_call(
        functools.partial(_ret_kernel, nc, has_state, emit_state),
        grid=(nb, RET_HEADS),
        in_specs=in_specs,
        out_specs=out_specs,
        out_shape=out_shape,
        scratch_shapes=[pltpu.VMEM((l, RET_V), F32), pltpu.VMEM((RET_QK, RET_V), F32)],
        compiler_params=_cparams(("arbitrary", "arbitrary")),
        name="retention",
    )(*args)
    return res if emit_state else res[0]


def _layer_norm(z, g, b):
    mu = jnp.mean(z, axis=-1, keepdims=True)
    var = jnp.mean(jnp.square(z - mu), axis=-1, keepdims=True)
    return (z - mu) * lax.rsqrt(var + LN_EPS) * g + b


def _pack_rows(hn, hp_ref):
    tm = hn.shape[0]
    half = D // 2
    lo = lax.bitcast_convert_type(hn[:, :half].astype(BF16).astype(F32), U32)
    hi = lax.bitcast_convert_type(hn[:, half:].astype(BF16).astype(F32), U32)
    w = (lo >> 16) | (hi & HI_MASK)
    for s in range(PK_S):
        hp_ref[pl.ds(s, tm, stride=PK_P), :] = w[:, s * PK_L:(s + 1) * PK_L]
    hp_ref[pl.ds(PK_S, tm, stride=PK_P), :] = jnp.zeros((tm, PK_L), U32)


def _unpack_rows(hp_ref, tm):
    lo, hi = [], []
    for s in range(PK_S):
        w = hp_ref[pl.ds(s, tm, stride=PK_P), :]
        lo.append(lax.bitcast_convert_type(w << 16, F32).astype(BF16))
        hi.append(lax.bitcast_convert_type(w & HI_MASK, F32).astype(BF16))
    return jnp.concatenate(lo + hi, axis=1)


def _route_tile(hn, wrt_ref, rb_ref, e_ref, w_ref, rank_ref, cnt_ref):
    tm = hn.shape[0]
    neg = -jnp.inf
    hb = hn.astype(BF16)
    hl = (hn - hb.astype(F32)).astype(BF16)
    logits = _qkt(wrt_ref[0], hb) + _qkt(wrt_ref[0], hl) + _qkt(wrt_ref[1], hb)
    s = _sigmoid(logits)
    biased = s + rb_ref[...]
    io8 = lax.broadcasted_iota(I32, (EXPERTS_PER_GROUP, tm), 0).astype(F32)
    slabs, gscore = [], []
    for g in range(N_GROUPS):
        slab = biased[g * EXPERTS_PER_GROUP:(g + 1) * EXPERTS_PER_GROUP, :]
        m1 = jnp.max(slab, axis=0, keepdims=True)
        i1 = jnp.min(jnp.where(slab == m1, io8, float(EXPERTS_PER_GROUP)), axis=0, keepdims=True)
        m2 = jnp.max(jnp.where(io8 == i1, neg, slab), axis=0, keepdims=True)
        slabs.append(slab)
        gscore.append(m1 + m2)
    masked = []
    for g in range(N_GROUPS):
        ahead = jnp.zeros_like(gscore[g])
        for g2 in range(N_GROUPS):
            if g2 == g:
                continue
            better = (gscore[g2] > gscore[g]) | ((gscore[g2] == gscore[g]) if g2 < g else False)
            ahead = ahead + jnp.where(better, 1.0, 0.0)
        masked.append(jnp.where(ahead < float(TOPK_GROUPS), slabs[g], neg))
    v = jnp.concatenate(masked, axis=0)
    io = lax.broadcasted_iota(I32, (N_EXPERTS, tm), 0).astype(F32)
    idxs, ws, hots = [], [], []
    for _ in range(TOP_K):
        m = jnp.max(v, axis=0, keepdims=True)
        idx = jnp.min(jnp.where(v == m, io, float(N_EXPERTS)), axis=0, keepdims=True)
        hot = io == idx
        ws.append(jnp.sum(jnp.where(hot, s, 0.0), axis=0, keepdims=True))
        v = jnp.where(hot, neg, v)
        idxs.append(idx)
        hots.append(hot)
    wsum = ws[0]
    for k in range(1, TOP_K):
        wsum = wsum + ws[k]
    chosen = jnp.where(hots[0], 1.0, 0.0)
    for k in range(1, TOP_K):
        chosen = chosen + jnp.where(hots[k], 1.0, 0.0)
    upper = jnp.where(lax.broadcasted_iota(I32, (tm, tm), 0) < lax.broadcasted_iota(I32, (tm, tm), 1), 1.0, 0.0)
    before = jnp.dot(chosen.astype(BF16), upper.astype(BF16), preferred_element_type=F32)
    zero_row = jnp.zeros((1, tm), F32)
    for k in range(8):
        if k < TOP_K:
            e_ref[k:k + 1, :] = idxs[k].astype(I32)
            w_ref[k:k + 1, :] = ws[k] / wsum * ROUTED_SCALE
            rank_ref[k:k + 1, :] = jnp.sum(jnp.where(hots[k], before, 0.0), axis=0, keepdims=True).astype(I32)
        else:
            e_ref[k:k + 1, :] = zero_row.astype(I32)
            w_ref[k:k + 1, :] = zero_row
            rank_ref[k:k + 1, :] = zero_row.astype(I32)
    cnt = jnp.sum(chosen, axis=1, keepdims=True)
    cnt_ref[...] = jnp.broadcast_to(cnt, (N_EXPERTS, 128)).astype(I32)


def _outproj_kernel(nk, o_ref, w_ref, x_ref, gate_ref, lng_ref, lnb_ref, sh_ref, sc_ref, wrt_ref, rb_ref,
                    xo_ref, hp_ref, e_ref, wt_ref, rank_ref, cnt_ref, acc_ref):
    kk = pl.program_id(1)
    part = jnp.dot(o_ref[...], w_ref[...], preferred_element_type=F32)

    @pl.when(kk == 0)
    def _():
        acc_ref[...] = part

    @pl.when(kk > 0)
    def _():
        acc_ref[...] += part

    @pl.when(kk == nk - 1)
    def _():
        xn = _layer_norm(ALPHA * x_ref[...] + gate_ref[...] * acc_ref[...], lng_ref[...], lnb_ref[...])
        xo_ref[...] = xn
        hn = xn * (1.0 + sc_ref[...]) + sh_ref[...]
        _pack_rows(hn, hp_ref)
        _route_tile(hn, wrt_ref, rb_ref, e_ref, wt_ref, rank_ref, cnt_ref)


def outproj_ln(o, w_out, x, mods, ln_g, ln_b, w_router_t, b_router, layer):
    k = o.shape[1]
    tk = 2048 if k == 2048 else 1024
    nk = k // tk
    tm = RT_TM
    ntile = NT // tm
    row = lambda m, kk: (m, 0)
    col = lambda m, kk: (0, m)
    return pl.pallas_call(
        functools.partial(_outproj_kernel, nk),
        grid=(ntile, nk),
        in_specs=[pl.BlockSpec((tm, tk), lambda m, kk: (m, kk)),
                  pl.BlockSpec((tk, D), lambda m, kk: (kk, 0),
                               pipeline_mode=pl.Buffered(1) if nk == 1 else None),
                  pl.BlockSpec((tm, D), row),
                  _mod_spec(layer, 2, tm),
                  pl.BlockSpec((None, None, 1, D), lambda m, kk: (layer, 0, 0, 0)),
                  pl.BlockSpec((None, None, 1, D), lambda m, kk: (layer, 0, 0, 0)),
                  _mod_spec(layer, 3, tm), _mod_spec(layer, 4, tm),
                  pl.BlockSpec((2, N_EXPERTS, D), lambda m, kk: (0, 0, 0)),
                  pl.BlockSpec((N_EXPERTS, 1), lambda m, kk: (0, 0))],
        out_specs=[pl.BlockSpec((tm, D), row),
                   pl.BlockSpec((tm * PK_P, PK_L), row),
                   pl.BlockSpec((8, tm), col), pl.BlockSpec((8, tm), col), pl.BlockSpec((8, tm), col),
                   pl.BlockSpec((None, N_EXPERTS, 128), lambda m, kk: (m, 0, 0))],
        out_shape=[jax.ShapeDtypeStruct((NT, D), F32),
                   jax.ShapeDtypeStruct((NT * PK_P, PK_L), U32),
                   jax.ShapeDtypeStruct((8, NT), I32), jax.ShapeDtypeStruct((8, NT), F32),
                   jax.ShapeDtypeStruct((8, NT), I32),
                   jax.ShapeDtypeStruct((ntile, N_EXPERTS, 128), I32)],
        scratch_shapes=[pltpu.VMEM((tm, D), F32)],
        compiler_params=_cparams(("arbitrary", "arbitrary")),
        name="outproj_ln_route",
    )(o, w_out, x, mods, ln_g, ln_b, mods, mods, w_router_t, b_router.reshape(N_EXPERTS, 1))


def _outproj_split_kernel(np_tiles, x_split, op_ref, os_ref, w_ref, *refs):
    if x_split:
        xp_ref, xs_ref, *refs = refs
    else:
        x_ref, *refs = refs
    (gate_ref, lng_ref, lnb_ref, sh_ref, sc_ref, wrt_ref, rb_ref,
     xo_ref, hp_ref, e_ref, wt_ref, rank_ref, cnt_ref) = refs
    m = pl.program_id(0)
    o_blk = jnp.where(m < np_tiles, op_ref[...], os_ref[...])
    y = jnp.dot(o_blk, w_ref[...], preferred_element_type=F32)
    x = jnp.where(m < np_tiles, xp_ref[...], xs_ref[...]) if x_split else x_ref[...]
    xn = _layer_norm(ALPHA * x + gate_ref[...] * y, lng_ref[...], lnb_ref[...])
    xo_ref[...] = xn
    hn = xn * (1.0 + sc_ref[...]) + sh_ref[...]
    _pack_rows(hn, hp_ref)
    _route_tile(hn, wrt_ref, rb_ref, e_ref, wt_ref, rank_ref, cnt_ref)


def outproj_ln_split(op, os_, w_out, x, mods, ln_g, ln_b, w_router_t, b_router, layer):
    tm = RT_TM
    ntile = NT // tm
    np_tiles = NP // tm
    row = lambda m: (m, 0)
    col = lambda m: (0, m)
    mod = lambda chunk: _mod_spec(layer, chunk, tm)
    const2 = lambda m: (0, 0)
    x_split = isinstance(x, tuple)
    x_args = list(x) if x_split else [x]
    x_specs = list(_split_rows_specs(tm)) if x_split else [pl.BlockSpec((tm, D), row)]
    return pl.pallas_call(
        functools.partial(_outproj_split_kernel, np_tiles, x_split),
        grid=(ntile,),
        in_specs=[*_split_rows_specs(tm),
                  pl.BlockSpec((D, D), const2, pipeline_mode=pl.Buffered(1)),
                  *x_specs,
                  mod(2),
                  pl.BlockSpec((None, None, 1, D), lambda m: (layer, 0, 0, 0)),
                  pl.BlockSpec((None, None, 1, D), lambda m: (layer, 0, 0, 0)),
                  mod(3), mod(4),
                  pl.BlockSpec((2, N_EXPERTS, D), lambda m: (0, 0, 0)),
                  pl.BlockSpec((N_EXPERTS, 1), const2)],
        out_specs=[pl.BlockSpec((tm, D), row),
                   pl.BlockSpec((tm * PK_P, PK_L), row),
                   pl.BlockSpec((8, tm), col), pl.BlockSpec((8, tm), col), pl.BlockSpec((8, tm), col),
                   pl.BlockSpec((None, N_EXPERTS, 128), lambda m: (m, 0, 0))],
        out_shape=[jax.ShapeDtypeStruct((NT, D), F32),
                   jax.ShapeDtypeStruct((NT * PK_P, PK_L), U32),
                   jax.ShapeDtypeStruct((8, NT), I32), jax.ShapeDtypeStruct((8, NT), F32),
                   jax.ShapeDtypeStruct((8, NT), I32),
                   jax.ShapeDtypeStruct((ntile, N_EXPERTS, 128), I32)],
        compiler_params=_cparams(("arbitrary",)),
        name="outproj_ln_route_split",
    )(op, os_, w_out, *x_args, mods, ln_g, ln_b, mods, mods, w_router_t, b_router.reshape(N_EXPERTS, 1))


def _slot_plan(e_t, rank_t, cnt):
    cnt = cnt[:, :, 0]
    tile_base = jnp.cumsum(cnt, axis=0) - cnt
    counts = jnp.sum(cnt, axis=0)
    padded = (counts + MOE_TB - 1) // MOE_TB * MOE_TB
    pad_end = jnp.cumsum(padded)
    pad_start = pad_end - padded
    offs = jnp.repeat(pad_start[None, :] + tile_base, NT // cnt.shape[0], axis=0)
    hot = e_t[:TOP_K, :, None] == jnp.arange(N_EXPERTS, dtype=I32)[None, None, :]
    dest = jnp.sum(jnp.where(hot, offs[None], 0), axis=-1) + rank_t[:TOP_K]
    ntile = NT // CMB_TM
    dest_tiles = dest.reshape(TOP_K, ntile, CMB_TM).transpose(1, 0, 2).reshape(ntile, 1, TOP_K * CMB_TM)
    nused = pad_end[-1] // MOE_TB
    blk = jnp.minimum(jnp.arange(MOE_NBLK, dtype=I32), nused - 1) * MOE_TB
    block_e = jnp.minimum(jnp.sum((pad_end[None, :] <= blk[:, None]).astype(I32), axis=1), N_EXPERTS - 1)
    first = jnp.concatenate([jnp.ones((1,), I32), (block_e[1:] != block_e[:-1]).astype(I32)])
    zero_start = pad_start + counts
    ids = jnp.arange(N_EXPERTS, dtype=I32)
    used = counts > 0
    later = jnp.where(used[None, :] & (ids[None, :] > ids[:, None]), ids[None, :], N_EXPERTS)
    nxt = jnp.min(later, axis=1)
    nxt = jnp.where(nxt == N_EXPERTS, ids, nxt)
    par = (jnp.cumsum(used.astype(I32)) - 1) % 2
    owner = block_e[:, None] == ids[None, :]
    nxt_b = jnp.sum(jnp.where(owner, nxt[None, :], 0), axis=1)
    par_b = jnp.sum(jnp.where(owner, par[None, :], 0), axis=1)
    return (dest_tiles.astype(I32), block_e.astype(I32), first, nused.reshape(1).astype(I32),
            zero_start.astype(I32), nxt_b.astype(I32), par_b.astype(I32))


_DISPATCH_ROWS = TOP_K * CMB_TM
_CMB_ROWS = 32


def _dispatch_kernel(zs_ref, nused_ref, dest_ref, hp_ref, wsg_ref, wsu_ref, wsd_ref, xs_hbm, shared_ref,
                     stage, zbuf, zsem, sem):
    i = pl.program_id(0)
    n = pl.num_programs(0)
    slot = i % 2

    @pl.when(i == 0)
    def _():
        zbuf[...] = jnp.zeros_like(zbuf)

        def zcopy(slot0, nslot):
            return pltpu.make_async_copy(zbuf.at[pl.ds(0, nslot * PK_P)],
                                         xs_hbm.at[pl.ds(slot0 * PK_P, nslot * PK_P)], zsem)

        def pad_pieces(e, wait):
            slot0 = zs_ref[e]
            npad = (MOE_TB - slot0 % MOE_TB) % MOE_TB
            piece = MOE_TB // 2
            while piece >= 1:
                @pl.when((npad & piece) != 0)
                def _(slot0=slot0, piece=piece):
                    cp = zcopy(slot0, piece)
                    cp.wait() if wait else cp.start()
                slot0 = slot0 + (npad & piece)
                piece //= 2

        def zstart(e, carry):
            pad_pieces(e, False)
            return carry

        def zwait(e, carry):
            pad_pieces(e, True)
            return carry

        def tstart(b, carry):
            zcopy(b * MOE_TB, MOE_TB).start()
            return carry

        def twait(b, carry):
            zcopy(b * MOE_TB, MOE_TB).wait()
            return carry

        lax.fori_loop(0, N_EXPERTS, zstart, 0)
        lax.fori_loop(nused_ref[0], MOE_NBLK + 1, tstart, 0)
        lax.fori_loop(0, N_EXPERTS, zwait, 0)
        lax.fori_loop(nused_ref[0], MOE_NBLK + 1, twait, 0)

    stage[slot] = hp_ref[...]

    def issue(r, carry):
        for k in range(TOP_K):
            dst = dest_ref[0, 0, k * CMB_TM + r] * PK_P
            pltpu.make_async_copy(stage.at[slot, pl.ds(r * PK_P, PK_P)], xs_hbm.at[pl.ds(dst, PK_P)],
                                  sem.at[slot]).start(priority=k % 2)
        return carry

    lax.fori_loop(0, CMB_TM, issue, 0, unroll=2)

    hb = _unpack_rows(hp_ref, CMB_TM)
    a = (_silu(jnp.dot(hb, wsg_ref[...], preferred_element_type=F32))
         * jnp.dot(hb, wsu_ref[...], preferred_element_type=F32))
    shared_ref[...] = jnp.dot(a.astype(BF16), wsd_ref[...], preferred_element_type=F32)

    def drain(s):
        for k in range(TOP_K):
            pltpu.make_async_copy(stage.at[s], xs_hbm.at[pl.ds(0, CMB_TM * PK_P)], sem.at[s]).wait()

    @pl.when(i > 0)
    def _():
        drain(1 - slot)

    @pl.when(i == n - 1)
    def _():
        drain(slot)


def moe_dispatch(hp, dest_tiles, zero_start, nused, ws_gate, ws_up, ws_down):
    ntile = NT // CMB_TM
    const = lambda i, zs, nu: (0, 0)
    grid_spec = pltpu.PrefetchScalarGridSpec(
        num_scalar_prefetch=2,
        grid=(ntile,),
        in_specs=[pl.BlockSpec((1, 1, _DISPATCH_ROWS), lambda i, zs, nu: (i, 0, 0), memory_space=pltpu.SMEM),
                  pl.BlockSpec((CMB_TM * PK_P, PK_L), lambda i, zs, nu: (i, 0)),
                  pl.BlockSpec((D, D_SHARED), const, pipeline_mode=pl.Buffered(1)),
                  pl.BlockSpec((D, D_SHARED), const, pipeline_mode=pl.Buffered(1)),
                  pl.BlockSpec((D_SHARED, D), const, pipeline_mode=pl.Buffered(1))],
        out_specs=[pl.BlockSpec(memory_space=pl.ANY),
                   pl.BlockSpec((CMB_TM, D), lambda i, zs, nu: (i, 0))],
        scratch_shapes=[pltpu.VMEM((2, CMB_TM * PK_P, PK_L), U32),
                        pltpu.VMEM((MOE_TB * PK_P, PK_L), U32),
                        pltpu.SemaphoreType.DMA(()), pltpu.SemaphoreType.DMA((2,))],
    )
    return pl.pallas_call(
        _dispatch_kernel,
        grid_spec=grid_spec,
        out_shape=[jax.ShapeDtypeStruct(((MOE_CAP + MOE_TB) * PK_P, PK_L), U32),
                   jax.ShapeDtypeStruct((NT, D), F32)],
        compiler_params=_cparams(("arbitrary",)),
        name="moe_dispatch",
    )(zero_start, nused, dest_tiles, hp, ws_gate, ws_up, ws_down)


def _moe_kernel(layer, be_ref, first_ref, nused_ref, nxt_ref, par_ref, xs_ref, wg_hbm, wu_hbm, wd_hbm, y_ref,
                wgs, wus, wds, wgb, wub, wdb, sem):
    i = pl.program_id(0)

    def weight_copies(e, s):
        return (pltpu.make_async_copy(wg_hbm.at[layer, e], wgs.at[s], sem.at[s]),
                pltpu.make_async_copy(wu_hbm.at[layer, e], wus.at[s], sem.at[s]),
                pltpu.make_async_copy(wd_hbm.at[layer, e], wds.at[s], sem.at[s]))

    @pl.when(i == 0)
    def _():
        for cp in weight_copies(be_ref[0], par_ref[0]):
            cp.start()

    @pl.when(first_ref[i] == 1)
    def _():
        e, s = be_ref[i], par_ref[i]
        for cp in weight_copies(e, s):
            cp.wait()

        @pl.when(nxt_ref[i] != e)
        def _():
            for cp in weight_copies(nxt_ref[i], 1 - s):
                cp.start()

        wgb[...] = wgs[s].astype(BF16)
        wub[...] = wus[s].astype(BF16)
        wdb[...] = wds[s].astype(BF16)

    @pl.when(i < nused_ref[0])
    def _():
        x = _unpack_rows(xs_ref, MOE_TB)
        a = (_silu(jnp.dot(x, wgb[...], preferred_element_type=F32))
             * jnp.dot(x, wub[...], preferred_element_type=F32))
        y = jnp.dot(a.astype(BF16), wdb[...], preferred_element_type=F32)
        for j in range(Y_S):
            y_ref[pl.ds(j, MOE_TB, stride=Y_P), :] = y[:, j * 128:(j + 1) * 128]
        y_ref[pl.ds(Y_S, MOE_TB, stride=Y_P), :] = jnp.zeros((MOE_TB, 128), F32)

    @pl.when(i >= nused_ref[0])
    def _():
        y_ref[...] = jnp.zeros_like(y_ref)


def moe_experts(xs, block_e, first, nused, nxt_e, par, w_gate, w_up, w_down, layer):
    blk = lambda i, be, fi, nu, nx, pa: (jnp.minimum(i, nu[0] - 1), 0)
    grid_spec = pltpu.PrefetchScalarGridSpec(
        num_scalar_prefetch=5,
        grid=(MOE_NBLK,),
        in_specs=[pl.BlockSpec((MOE_TB * PK_P, PK_L), blk),
                  pl.BlockSpec(memory_space=pl.ANY), pl.BlockSpec(memory_space=pl.ANY),
                  pl.BlockSpec(memory_space=pl.ANY)],
        out_specs=pl.BlockSpec((MOE_TB * Y_P, 128), lambda i, be, fi, nu, nx, pa: (i, 0)),
        scratch_shapes=[pltpu.VMEM((2, D, D_EXPERT), F32), pltpu.VMEM((2, D, D_EXPERT), F32),
                        pltpu.VMEM((2, D_EXPERT, D), F32),
                        pltpu.VMEM((D, D_EXPERT), BF16), pltpu.VMEM((D, D_EXPERT), BF16),
                        pltpu.VMEM((D_EXPERT, D), BF16),
                        pltpu.SemaphoreType.DMA((2,))],
    )
    return pl.pallas_call(
        functools.partial(_moe_kernel, layer),
        grid_spec=grid_spec,
        out_shape=jax.ShapeDtypeStruct((MOE_CAP * Y_P, 128), F32),
        compiler_params=_cparams(("arbitrary",)),
        name="moe_experts",
    )(block_e, first, nused, nxt_e, par, xs, w_gate, w_up, w_down)


def _combine_kernel(has_next, dest_ref, destn_ref, y_hbm, w_ref, shared_ref, x_ref,
                    gate_ref, lng_ref, lnb_ref, *rest):
    if has_next:
        sh_ref, sc_ref, xo_ref, ho_ref, buf, routed_ref, sem = rest
    else:
        xo_ref, buf, routed_ref, sem = rest
    i = pl.program_id(0)
    n = pl.num_programs(0)
    slot = i % 2
    nrow = TOP_K * CMB_TM

    def gather(dref, s):
        def body(r, carry):
            for k in range(TOP_K):
                src = dref[0, 0, k * CMB_TM + r] * Y_P
                dst = (k * CMB_TM + r) * Y_P
                pltpu.make_async_copy(y_hbm.at[pl.ds(src, Y_S)], buf.at[s, pl.ds(dst, Y_S)],
                                      sem.at[s]).start(priority=k % 2)
            return carry
        lax.fori_loop(0, CMB_TM, body, 0, unroll=2)

    @pl.when(i == 0)
    def _():
        gather(dest_ref, 0)

    @pl.when(i + 1 < n)
    def _():
        gather(destn_ref, 1 - slot)

    pltpu.make_async_copy(y_hbm.at[pl.ds(0, nrow * Y_S)], buf.at[slot, pl.ds(0, nrow * Y_S)],
                          sem.at[slot]).wait()
    for r0 in range(0, CMB_TM, _CMB_ROWS):
        wsub = w_ref[r0:r0 + _CMB_ROWS, :]
        wb = [jnp.broadcast_to(wsub[:, k:k + 1], (_CMB_ROWS, 128)) for k in range(TOP_K)]
        for j in range(Y_S):
            acc = buf[slot, pl.ds(r0 * Y_P + j, _CMB_ROWS, stride=Y_P), :] * wb[0]
            for k in range(1, TOP_K):
                acc = acc + buf[slot, pl.ds((k * CMB_TM + r0) * Y_P + j, _CMB_ROWS, stride=Y_P), :] * wb[k]
            routed_ref[r0:r0 + _CMB_ROWS, j * 128:(j + 1) * 128] = acc
    routed = routed_ref[...]
    xn = _layer_norm(ALPHA * x_ref[...] + gate_ref[...] * (routed + shared_ref[...]), lng_ref[...], lnb_ref[...])
    xo_ref[...] = xn
    if has_next:
        ho_ref[...] = (xn * (1.0 + sc_ref[...]) + sh_ref[...]).astype(ho_ref.dtype)


def moe_combine(y_slots, dest_tiles, w_tok, shared, x1, mods, ln_g, ln_b, layer):
    has_next = layer + 1 < DEPTH
    tm = CMB_TM
    ntile = NT // tm
    row = lambda m: (m, 0)
    in_specs = [pl.BlockSpec((1, 1, TOP_K * tm), lambda m: (m, 0, 0), memory_space=pltpu.SMEM),
                pl.BlockSpec((1, 1, TOP_K * tm), lambda m: (jnp.minimum(m + 1, ntile - 1), 0, 0),
                             memory_space=pltpu.SMEM),
                pl.BlockSpec(memory_space=pl.ANY),
                pl.BlockSpec((tm, 8), row),
                pl.BlockSpec((tm, D), row), pl.BlockSpec((tm, D), row),
                _mod_spec(layer, 5, tm),
                pl.BlockSpec((None, None, 1, D), lambda m: (layer, 1, 0, 0)),
                pl.BlockSpec((None, None, 1, D), lambda m: (layer, 1, 0, 0))]
    args = [dest_tiles, dest_tiles, y_slots, w_tok, shared, x1, mods, ln_g, ln_b]
    out_specs = [pl.BlockSpec((tm, D), row)]
    out_shape = [jax.ShapeDtypeStruct((NT, D), F32)]
    if has_next:
        in_specs += [_mod_spec(layer + 1, 0, tm), _mod_spec(layer + 1, 1, tm)]
        args += [mods, mods]
        out_specs.append(pl.BlockSpec((tm, D), row))
        out_shape.append(jax.ShapeDtypeStruct((NT, D), BF16))
    res = pl.pallas_call(
        functools.partial(_combine_kernel, has_next),
        grid=(ntile,),
        in_specs=in_specs,
        out_specs=out_specs,
        out_shape=out_shape,
        scratch_shapes=[pltpu.VMEM((2, TOP_K * tm * Y_P, 128), F32), pltpu.VMEM((tm, D), F32),
                        pltpu.SemaphoreType.DMA((2,))],
        compiler_params=_cparams(("arbitrary",)),
        name="moe_combine",
    )(*args)
    return (res[0], res[1]) if has_next else (res[0], None)


def _split_bf16(w):
    hi = w.astype(BF16)
    lo = (w - hi.astype(F32)).astype(BF16)
    return jnp.stack([hi, lo])


def kernel(x_prompt, x_sample, cache_diff_k, cache_diff_v, cache_na_k, cache_na_v, state_ret, c, c_ctx, ada_w, ada_b, ln_g, ln_b, diff_w_in, diff_w_out, diff_lambda, diff_subln_g, na_w_in, na_w_out, na_rpb, ret_w_in, ret_w_out, ret_decay, moe_router, moe_router_bias, moe_w_gate, moe_w_up, moe_w_down, shared_w_gate, shared_w_up, shared_w_down):
    x = (x_prompt.reshape(NP, D), x_sample.reshape(NS, D))
    cond = jnp.concatenate([c_ctx[None], c, jnp.zeros((N_COND - 1 - DEC_BATCH, D), F32)], axis=0)
    mods = modulation_table(cond, ada_w, ada_b)
    ln_g4 = ln_g.reshape(DEPTH, 2, 1, D)
    ln_b4 = ln_b.reshape(DEPTH, 2, 1, D)
    cos, sin = _rope_tables()
    h = modulate(*x, mods, 0)
    new_dk, new_dv, new_nk, new_nv, new_rs = [], [], [], [], []
    for i in range(DEPTH):
        j = i // N_MIXERS
        kind = i % N_MIXERS
        if kind == 0:
            qp = matmul(h, diff_w_in, j, 0, D, 0, NP, BF16)
            kp = matmul(h, diff_w_in, j, D, D, 0, NP, F32)
            vp = matmul(h, diff_w_in, j, 2 * D, D, 0, NP, F32)
            qs = rope(matmul(h, diff_w_in, j, 0, D, NP, NS, F32), cos, sin)
            ks = rope(matmul(h, diff_w_in, j, D, D, NP, NS, F32), cos, sin)
            vs = matmul(h, diff_w_in, j, 2 * D, D, NP, NS, BF16)
            lam_init = 0.8 - 0.6 * math.exp(-0.3 * i)
            rows = lambda n: pl.BlockSpec((n, D), lambda b, t: (b, 0))
            op = diff_attention(qp, [(kp, vp, rows(SEQ))], diff_lambda[j], diff_subln_g[j], lam_init,
                                BATCH, SEQ, SEQ)
            ctx = pl.BlockSpec((None, None, PAST, D), lambda b, t, j=j: (b, j, 0, 0))
            os_ = diff_attention(qs, [(cache_diff_k.reshape(DEC_BATCH, -1, PAST, D),
                                       cache_diff_v.reshape(DEC_BATCH, -1, PAST, D), ctx),
                                      (ks, vs, rows(DEC_SEQ))],
                                 diff_lambda[j], diff_subln_g[j], lam_init, DEC_BATCH, DEC_SEQ, 256)
            new_dk.append(kp.reshape(BATCH, SEQ, DIFF_HEADS, 2 * DIFF_QK))
            new_dv.append(vp.reshape(BATCH, SEQ, DIFF_HEADS, DIFF_V))
            w_out = diff_w_out[j]
        elif kind == 1:
            qp = matmul(h, na_w_in, j, 0, D, 0, NP, BF16)
            kp = matmul(h, na_w_in, j, D, D, 0, NP, F32)
            vp = matmul(h, na_w_in, j, 2 * D, D, 0, NP, F32)
            qs = matmul(h, na_w_in, j, 0, D, NP, NS, BF16)
            ks = matmul(h, na_w_in, j, D, D, NP, NS, BF16)
            vs = matmul(h, na_w_in, j, 2 * D, D, NP, NS, BF16)
            op = na_context_attention(qp, kp, vp)
            os_ = na_latent_attention(qs, ks, vs,
                                      cache_na_k[:, j].reshape(DEC_BATCH, PAST, D).astype(BF16),
                                      cache_na_v[:, j].reshape(DEC_BATCH, PAST, D).astype(BF16),
                                      _na_bias_tables(na_rpb[j]))
            new_nk.append(kp.reshape(BATCH, SEQ, NA_HEADS, NA_DH))
            new_nv.append(vp.reshape(BATCH, SEQ, NA_HEADS, NA_DH))
            w_out = na_w_out[j]
        else:
            qd = RET_HEADS * RET_QK
            vd = RET_HEADS * RET_V
            log_g = -jax.nn.softplus(-ret_decay[j].astype(F32))
            outs = []
            for row0, nrows, nb, l in ((0, NP, BATCH, SEQ), (NP, NS, DEC_BATCH, DEC_SEQ)):
                q_ = matmul(h, ret_w_in, j, 0, qd, row0, nrows, BF16)
                k_ = matmul(h, ret_w_in, j, qd, qd, row0, nrows, F32)
                v_ = matmul(h, ret_w_in, j, 2 * qd, vd, row0, nrows, BF16)
                g_ = matmul(h, ret_w_in, j, 2 * qd + vd, vd, row0, nrows, F32)
                if row0 == 0:
                    o_, st = retention(q_, k_, v_, g_, log_g, nb, l, None, True)
                    new_rs.append(st)
                else:
                    o_ = retention(q_, k_, v_, g_, log_g, nb, l, state_ret[:, j], False)
                outs.append(o_)
            op, os_ = outs
            w_out = ret_w_out[j]
        if w_out.shape[0] == D:
            x1, hp, e_t, w_t, rank_t, cnt = outproj_ln_split(
                op, os_, w_out.astype(BF16), x, mods, ln_g4, ln_b4,
                _split_bf16(moe_router[i].T), moe_router_bias[i].astype(F32), i)
        else:
            x1, hp, e_t, w_t, rank_t, cnt = outproj_ln(
                jnp.concatenate([op, os_], axis=0), w_out.astype(BF16), x, mods, ln_g4, ln_b4,
                _split_bf16(moe_router[i].T), moe_router_bias[i].astype(F32), i)
        dest_tiles, block_e, first, nused, zero_start, nxt_e, par = _slot_plan(e_t, rank_t, cnt)
        xs, shared = moe_dispatch(hp, dest_tiles, zero_start, nused, shared_w_gate[i].astype(BF16),
                                  shared_w_up[i].astype(BF16), shared_w_down[i].astype(BF16))
        y_slots = moe_experts(xs, block_e, first, nused, nxt_e, par, moe_w_gate, moe_w_up, moe_w_down, i)
        x, h = moe_combine(y_slots, dest_tiles, w_t.T, shared, x1, mods, ln_g4, ln_b4, i)
    return (x[:NP].reshape(BATCH, SEQ, D), x[NP:].reshape(DEC_BATCH, DEC_SEQ, D),
            jnp.stack(new_dk, axis=1), jnp.stack(new_dv, axis=1),
            jnp.stack(new_nk, axis=1), jnp.stack(new_nv, axis=1), jnp.stack(new_rs, axis=1))
```

```python
import functools
import math

import jax
import jax.numpy as jnp
import numpy as np
from jax import lax
from jax.experimental import pallas as pl
from jax.experimental.pallas import tpu as pltpu

F32 = jnp.float32
BF16 = jnp.bfloat16
U32 = jnp.uint32
I32 = jnp.int32

D = 2048
BATCH = 32
SEQ = 256
DEPTH = 4
DEC_BATCH = 8
DEC_SEQ = 1024
PAST = 512
NP = BATCH * SEQ
NS = DEC_BATCH * DEC_SEQ
NT = NP + NS
GRID_W = 64
N_MIXERS = 3
DIFF_HEADS = 8
DIFF_QK = 128
DIFF_V = 256
NA_HEADS = 16
NA_DH = 128
NA_KH = 8
NA_KH_MAX = 8
NA_KW = 16
RET_HEADS = 8
RET_QK = 256
RET_V = 512
RET_CHUNK = 128
N_EXPERTS = 64
N_GROUPS = 8
EXPERTS_PER_GROUP = 8
TOPK_GROUPS = 4
TOP_K = 6
D_EXPERT = 512
D_SHARED = 512
ROUTED_SCALE = 2.5
ROPE_BASE = 10000.0
LN_EPS = 1e-5
ALPHA = (2 * DEPTH) ** 0.25
N_COND = 16

VMEM_LIMIT = 56 * 1024 * 1024

MOE_TB = 256
MOE_NBLK = (NT * TOP_K) // MOE_TB + N_EXPERTS
MOE_CAP = MOE_NBLK * MOE_TB
RT_TM = 512
CMB_TM = 256
PK_S = 8
PK_L = D // 2 // PK_S
Y_S = D // 128
PK_P = PK_S + 1
Y_P = Y_S + 1
HI_MASK = np.uint32(0xFFFF0000)


def _cparams(sem):
    return pltpu.CompilerParams(dimension_semantics=sem, vmem_limit_bytes=VMEM_LIMIT)


def _cond_row(row0):
    return jnp.where(row0 < NP, 0, 1 + (row0 - NP) // DEC_SEQ)


def _mod_spec(layer, chunk, tm, moff=0):
    return pl.BlockSpec((None, None, None, 1, D),
                        lambda m, *_: (layer, _cond_row((m + moff) * tm), chunk, 0, 0))


def _sigmoid(x):
    return 1.0 / (1.0 + jnp.exp(-x))


def _silu(x):
    return x * _sigmoid(x)


def _mod_kernel(cond_ref, w_ref, b_ref, o_ref):
    c = _silu(cond_ref[...]).astype(BF16)
    o_ref[...] = jnp.dot(c, w_ref[...].astype(BF16), preferred_element_type=F32) + b_ref[...]


def modulation_table(cond, ada_w, ada_b):
    tn = 1024
    n6 = 6 * D
    out = pl.pallas_call(
        _mod_kernel,
        grid=(DEPTH, n6 // tn),
        in_specs=[pl.BlockSpec((N_COND, D), lambda l, n: (0, 0)),
                  pl.BlockSpec((None, D, tn), lambda l, n: (l, 0, n)),
                  pl.BlockSpec((None, 1, tn), lambda l, n: (l, 0, n))],
        out_specs=pl.BlockSpec((None, N_COND, tn), lambda l, n: (l, 0, n)),
        out_shape=jax.ShapeDtypeStruct((DEPTH, N_COND, n6), F32),
        compiler_params=_cparams(("arbitrary", "arbitrary")),
        name="modulation_table",
    )(cond, ada_w, ada_b.reshape(DEPTH, 1, n6))
    return out.reshape(DEPTH, N_COND, 6, 1, D)


def _split_rows_specs(tm):
    np_tiles, ns_tiles = NP // tm, NS // tm
    return (pl.BlockSpec((tm, D), lambda m, *_: (jnp.minimum(m, np_tiles - 1), 0)),
            pl.BlockSpec((tm, D), lambda m, *_: (jnp.clip(m - np_tiles, 0, ns_tiles - 1), 0)))


def _modulate_kernel(np_tiles, xp_ref, xs_ref, sh_ref, sc_ref, o_ref):
    x = jnp.where(pl.program_id(0) < np_tiles, xp_ref[...], xs_ref[...])
    o_ref[...] = (x * (1.0 + sc_ref[...]) + sh_ref[...]).astype(o_ref.dtype)


def modulate(xp, xs, mods, layer):
    tm = 512
    return pl.pallas_call(
        functools.partial(_modulate_kernel, NP // tm),
        grid=(NT // tm,),
        in_specs=[*_split_rows_specs(tm), _mod_spec(layer, 0, tm), _mod_spec(layer, 1, tm)],
        out_specs=pl.BlockSpec((tm, D), lambda m: (m, 0)),
        out_shape=jax.ShapeDtypeStruct((NT, D), BF16),
        compiler_params=_cparams(("arbitrary",)),
        name="modulate",
    )(xp, xs, mods, mods)


def _mm_kernel(a_ref, w_ref, o_ref, wb_ref):
    @pl.when(pl.program_id(1) == 0)
    def _():
        wb_ref[...] = w_ref[...].astype(BF16)

    o_ref[...] = jnp.dot(a_ref[...], wb_ref[...], preferred_element_type=F32).astype(o_ref.dtype)


def matmul(a, w, layer, col0, ncols, row0, nrows, out_dtype):
    k = a.shape[1]
    tm, tn = 1024, 1024
    moff = row0 // tm
    joff = col0 // tn
    return pl.pallas_call(
        _mm_kernel,
        grid=(ncols // tn, nrows // tm),
        in_specs=[pl.BlockSpec((tm, k), lambda j, m: (m + moff, 0)),
                  pl.BlockSpec((None, k, tn), lambda j, m: (layer, 0, j + joff))],
        out_specs=pl.BlockSpec((tm, tn), lambda j, m: (m, j)),
        out_shape=jax.ShapeDtypeStruct((nrows, ncols), out_dtype),
        scratch_shapes=[pltpu.VMEM((k, tn), BF16)],
        compiler_params=_cparams(("arbitrary", "arbitrary")),
        name="in_proj",
    )(a, w)


def _rope_tables():
    half = DIFF_QK // 2
    t = jnp.arange(DEC_SEQ)
    inv = ROPE_BASE ** (-jnp.arange(0, half, 2, dtype=F32) / half)

    def cs(pos):
        ang = pos.astype(F32)[:, None] * inv[None, :]
        return jnp.cos(ang), jnp.sin(ang)

    cr, sr = cs(t // GRID_W)
    cc, sc = cs(t % GRID_W)
    cos = jnp.concatenate([cr, cr, cc, cc], axis=-1)
    sin = jnp.concatenate([-sr, sr, -sc, sc], axis=-1)
    return cos, sin


def _rope_store(x, cos, sin, o_ref):
    lane = lax.broadcasted_iota(jnp.int32, cos.shape, 1)
    first = (lane % (DIFF_QK // 2)) < (DIFF_QK // 4)
    for g in range(x.shape[1] // DIFF_QK):
        xg = x[:, g * DIFF_QK:(g + 1) * DIFF_QK]
        sw = jnp.where(first, pltpu.roll(xg, DIFF_QK - DIFF_QK // 4, 1), pltpu.roll(xg, DIFF_QK // 4, 1))
        o_ref[:, g * DIFF_QK:(g + 1) * DIFF_QK] = (xg * cos + sw * sin).astype(o_ref.dtype)


def _mm_rope_kernel(a_ref, w_ref, cos_ref, sin_ref, o_ref, wb_ref):
    @pl.when(pl.program_id(1) == 0)
    def _():
        wb_ref[...] = w_ref[...].astype(BF16)

    y = jnp.dot(a_ref[...], wb_ref[...], preferred_element_type=F32)
    _rope_store(y, cos_ref[...], sin_ref[...], o_ref)


def matmul_rope(a, w, layer, col0, ncols, cos, sin):
    k = a.shape[1]
    tm, tn = DEC_SEQ, 1024
    moff = NP // tm
    joff = col0 // tn
    tab = pl.BlockSpec((DEC_SEQ, DIFF_QK), lambda j, m: (0, 0))
    return pl.pallas_call(
        _mm_rope_kernel,
        grid=(ncols // tn, NS // tm),
        in_specs=[pl.BlockSpec((tm, k), lambda j, m: (m + moff, 0)),
                  pl.BlockSpec((None, k, tn), lambda j, m: (layer, 0, j + joff)),
                  tab, tab],
        out_specs=pl.BlockSpec((tm, tn), lambda j, m: (m, j)),
        out_shape=jax.ShapeDtypeStruct((NS, ncols), BF16),
        scratch_shapes=[pltpu.VMEM((k, tn), BF16)],
        compiler_params=_cparams(("arbitrary", "arbitrary")),
        name="in_proj_rope",
    )(a, w, cos, sin)


def _qkt(q, k):
    return lax.dot_general(q, k, (((1,), (1,)), ((), ())), preferred_element_type=F32)


def _softmax(s):
    m = jnp.max(s, axis=-1, keepdims=True)
    p = jnp.exp(s - m)
    return p / jnp.sum(p, axis=-1, keepdims=True)


def _softmax_parts(parts):
    m = jnp.max(parts[0], axis=-1, keepdims=True)
    for s in parts[1:]:
        m = jnp.maximum(m, jnp.max(s, axis=-1, keepdims=True))
    ps = [jnp.exp(s - m) for s in parts]
    den = jnp.sum(ps[0], axis=-1, keepdims=True)
    for p in ps[1:]:
        den = den + jnp.sum(p, axis=-1, keepdims=True)
    return [p / den for p in ps]


def _diff_kernel(lam_init, nsrc, lamp_ref, g_ref, q_ref, *refs):
    kv_refs, o_ref = refs[:2 * nsrc], refs[2 * nsrc]
    lp = lamp_ref[...]
    lam = (jnp.exp(jnp.sum(lp[0:1] * lp[1:2], axis=-1, keepdims=True))
           - jnp.exp(jnp.sum(lp[2:3] * lp[3:4], axis=-1, keepdims=True)) + lam_init)
    scale = DIFF_QK ** -0.5
    g = g_ref[...]
    for h in range(DIFF_HEADS):
        c0 = h * DIFF_V
        q = q_ref[:, c0:c0 + DIFF_V].astype(BF16)
        ks = [kv_refs[2 * i][:, c0:c0 + DIFF_V].astype(BF16) for i in range(nsrc)]
        a1 = _softmax_parts([_qkt(q[:, :DIFF_QK], k[:, :DIFF_QK]) * scale for k in ks])
        a2 = _softmax_parts([_qkt(q[:, DIFF_QK:], k[:, DIFF_QK:]) * scale for k in ks])
        o = None
        for i in range(nsrc):
            a = (a1[i] - lam * a2[i]).astype(BF16)
            v = kv_refs[2 * i + 1][:, c0:c0 + DIFF_V].astype(BF16)
            part = jnp.dot(a, v, preferred_element_type=F32)
            o = part if o is None else o + part
        o = o * lax.rsqrt(jnp.mean(jnp.square(o), axis=-1, keepdims=True) + LN_EPS) * g
        o_ref[:, c0:c0 + DIFF_V] = (o * (1.0 - lam_init)).astype(o_ref.dtype)


def diff_attention(q, kv_srcs, lam_p, subln_g, lam_init, nb, lq, tq):
    nq = lq // tq
    in_specs = [pl.BlockSpec((4, DIFF_QK), lambda b, i: (0, 0)),
                pl.BlockSpec((1, DIFF_V), lambda b, i: (0, 0)),
                pl.BlockSpec((tq, D), lambda b, i: (b * nq + i, 0))]
    args = [lam_p, subln_g.reshape(1, DIFF_V), q]
    for k, v, spec in kv_srcs:
        in_specs += [spec, spec]
        args += [k, v]
    return pl.pallas_call(
        functools.partial(_diff_kernel, lam_init, len(kv_srcs)),
        grid=(nb, nq),
        in_specs=in_specs,
        out_specs=pl.BlockSpec((tq, D), lambda b, i: (b * nq + i, 0)),
        out_shape=jax.ShapeDtypeStruct((nb * lq, D), BF16),
        compiler_params=_cparams(("arbitrary", "arbitrary")),
        name="diff_attention",
    )(*args)


def _na_ctx_kernel(q_ref, k_ref, v_ref, o_ref):
    scale = NA_DH ** -0.5
    for h in range(NA_HEADS):
        c0 = h * NA_DH
        q = q_ref[:, c0:c0 + NA_DH].astype(BF16)
        k = k_ref[:, c0:c0 + NA_DH].astype(BF16)
        v = v_ref[:, c0:c0 + NA_DH].astype(BF16)
        p = _softmax(_qkt(q, k) * scale).astype(BF16)
        o_ref[:, c0:c0 + NA_DH] = jnp.dot(p, v, preferred_element_type=F32).astype(o_ref.dtype)


def na_context_attention(q, k, v):
    spec = pl.BlockSpec((SEQ, D), lambda b: (b, 0))
    return pl.pallas_call(
        _na_ctx_kernel,
        grid=(BATCH,),
        in_specs=[spec, spec, spec],
        out_specs=spec,
        out_shape=jax.ShapeDtypeStruct((NP, D), BF16),
        compiler_params=_cparams(("arbitrary",)),
        name="na_context_attention",
    )(q, k, v)


NA_ROWS = DEC_SEQ // GRID_W
NA_NLOC = NA_KH * GRID_W


def _na_window_row(r):
    return jnp.clip(r - NA_KH // 2, 0, NA_ROWS - NA_KH)


def _na_bias_tables(rpb):
    col = jnp.arange(GRID_W)
    col_start = jnp.clip(col - NA_KW // 2, 0, GRID_W - NA_KW)
    col_ok = (col[None, :] >= col_start[:, None]) & (col[None, :] < col_start[:, None] + NA_KW)
    dc_idx = jnp.clip(col[None, :] - col[:, None] + NA_KW - 1, 0, 2 * NA_KW - 2)
    hot = (dc_idx[:, :, None] == jnp.arange(2 * NA_KW - 1)[None, None, :]).astype(F32)
    full = jnp.einsum('hrd,qkd->hqrk', rpb.astype(F32), hot, precision=lax.Precision.HIGHEST)
    full = jnp.where(col_ok[None, :, None, :], full, -jnp.inf)
    full = full.reshape(NA_HEADS, GRID_W, (2 * NA_KH_MAX - 1) * GRID_W)
    tabs = [full[:, :, (NA_KH_MAX - 1 - o) * GRID_W:(NA_KH_MAX - 1 - o) * GRID_W + NA_NLOC]
            for o in range(NA_KH)]
    return jnp.stack(tabs)


def _na_lat_kernel(q_ref, k_ref, v_ref, kc_ref, vc_ref, bias_ref, o_ref, pctx_ref, oloc_ref):
    scale = NA_DH ** -0.5
    s_ctx_all = _qkt(q_ref[...], kc_ref[...]) * scale
    for r in range(NA_ROWS):
        r0 = min(max(r - NA_KH // 2, 0), NA_ROWS - NA_KH)
        rows = slice(r * GRID_W, (r + 1) * GRID_W)
        win = slice(r0 * GRID_W, r0 * GRID_W + NA_NLOC)
        s_loc = _qkt(q_ref[rows, :], k_ref[win, :]) * scale + bias_ref[r - r0]
        s_ctx = s_ctx_all[rows, :]
        m = jnp.maximum(jnp.max(s_loc, axis=-1, keepdims=True), jnp.max(s_ctx, axis=-1, keepdims=True))
        p_loc = jnp.exp(s_loc - m)
        p_ctx = jnp.exp(s_ctx - m)
        den = jnp.sum(p_loc, axis=-1, keepdims=True) + jnp.sum(p_ctx, axis=-1, keepdims=True)
        oloc_ref[rows, :] = jnp.dot((p_loc / den).astype(BF16), v_ref[win, :], preferred_element_type=F32)
        pctx_ref[rows, :] = (p_ctx / den).astype(BF16)
    o = oloc_ref[...] + jnp.dot(pctx_ref[...], vc_ref[...], preferred_element_type=F32)
    o_ref[...] = o.astype(o_ref.dtype)


def na_latent_attention(q, k, v, k_ctx, v_ctx, bias_tabs):
    head = pl.BlockSpec((DEC_SEQ, NA_DH), lambda b, h: (b, h))
    ctx = pl.BlockSpec((None, PAST, NA_DH), lambda b, h: (b, 0, h))
    return pl.pallas_call(
        _na_lat_kernel,
        grid=(DEC_BATCH, NA_HEADS),
        in_specs=[head, head, head, ctx, ctx,
                  pl.BlockSpec((NA_KH, None, GRID_W, NA_NLOC), lambda b, h: (0, h, 0, 0))],
        out_specs=head,
        out_shape=jax.ShapeDtypeStruct((NS, D), BF16),
        scratch_shapes=[pltpu.VMEM((DEC_SEQ, PAST), BF16), pltpu.VMEM((DEC_SEQ, NA_DH), F32)],
        compiler_params=_cparams(("arbitrary", "arbitrary")),
        name="na_latent_attention",
    )(q, k, v, k_ctx, v_ctx, bias_tabs)


def _ret_kernel(nc, has_state, emit_state, logg_ref, q_ref, k_ref, v_ref, g_ref, *rest):
    rest = list(rest)
    s0_ref = rest.pop(0) if has_state else None
    o_ref = rest.pop(0)
    st_ref = rest.pop(0) if emit_state else None
    oacc, sacc = rest
    h = pl.program_id(1)
    lgf = logg_ref[0, h]
    lgb = logg_ref[1, h]
    c = RET_CHUNK
    pos = lax.broadcasted_iota(jnp.int32, (c, 1), 0).astype(F32)
    diff = (lax.broadcasted_iota(jnp.int32, (c, c), 0) - lax.broadcasted_iota(jnp.int32, (c, c), 1)).astype(F32)
    dmask_f = jnp.where(diff >= 0, jnp.exp(jnp.maximum(diff, 0.0) * lgf), 0.0)
    dmask_b = jnp.where(diff <= 0, jnp.exp(jnp.maximum(-diff, 0.0) * lgb), 0.0)
    qdec_f = jnp.exp((pos + 1.0) * lgf)
    kdec_f = jnp.exp((c - 1.0 - pos) * lgf)
    qdec_b = jnp.exp((c - pos) * lgb)
    kdec_b = jnp.exp(pos * lgb)
    cd_f = jnp.exp(jnp.full((1, 1), c, F32) * lgf)
    cd_b = jnp.exp(jnp.full((1, 1), c, F32) * lgb)

    def chunk(i):
        sl = slice(i * c, (i + 1) * c)
        return q_ref[sl, :].astype(BF16), k_ref[sl, :] * (RET_QK ** -0.5), v_ref[sl, :].astype(BF16)

    def state_update(kc, kdec, vb, cd):
        kd = (kc * kdec).T.astype(BF16)
        sacc[...] = sacc[...] * cd + jnp.dot(kd, vb, preferred_element_type=F32)

    if has_state:
        sacc[...] = s0_ref[0]
    else:
        sacc[...] = jnp.zeros_like(sacc)
    for i in range(nc):
        qb, kc, vb = chunk(i)
        qk = _qkt(qb, kc.astype(BF16))
        o = (jnp.dot((qk * dmask_f).astype(BF16), vb, preferred_element_type=F32)
             + jnp.dot((qk * dmask_b).astype(BF16), vb, preferred_element_type=F32)
             + jnp.dot(qb, sacc[...].astype(BF16), preferred_element_type=F32) * qdec_f)
        oacc[i * c:(i + 1) * c, :] = o
        state_update(kc, kdec_f, vb, cd_f)
    if emit_state:
        st_ref[0] = sacc[...]

    if has_state:
        sacc[...] = s0_ref[1]
    else:
        sacc[...] = jnp.zeros_like(sacc)
    for i in reversed(range(nc)):
        qb, kc, vb = chunk(i)
        oacc[i * c:(i + 1) * c, :] += jnp.dot(qb, sacc[...].astype(BF16), preferred_element_type=F32) * qdec_b
        state_update(kc, kdec_b, vb, cd_b)
    if emit_state:
        st_ref[1] = sacc[...]

    o = oacc[...]
    mu = jnp.mean(o, axis=-1, keepdims=True)
    var = jnp.mean(jnp.square(o - mu), axis=-1, keepdims=True)
    o = (o - mu) * lax.rsqrt(var + LN_EPS)
    o_ref[...] = (o * _silu(g_ref[...])).astype(o_ref.dtype)


def retention(q, k, v, g, log_g, nb, l, state=None, emit_state=False):
    nc = l // RET_CHUNK
    has_state = state is not None
    in_specs = [pl.BlockSpec(memory_space=pltpu.SMEM),
                pl.BlockSpec((l, RET_QK), lambda b, h: (b, h)),
                pl.BlockSpec((l, RET_QK), lambda b, h: (b, h)),
                pl.BlockSpec((l, RET_V), lambda b, h: (b, h)),
                pl.BlockSpec((l, RET_V), lambda b, h: (b, h))]
    args = [log_g, q, k, v, g]
    st_spec = pl.BlockSpec((None, 2, None, RET_QK, RET_V), lambda b, h: (b, 0, h, 0, 0))
    if has_state:
        in_specs.append(st_spec)
        args.append(state)
    out_specs = [pl.BlockSpec((l, RET_V), lambda b, h: (b, h))]
    out_shape = [jax.ShapeDtypeStruct((nb * l, RET_HEADS * RET_V), BF16)]
    if emit_state:
        out_specs.append(st_spec)
        out_shape.append(jax.ShapeDtypeStruct((nb, 2, RET_HEADS, RET_QK, RET_V), F32))
    res = pl.pallas_call(
        functools.partial(_ret_kernel, nc, has_state, emit_state),
        grid=(nb, RET_HEADS),
        in_specs=in_specs,
        out_specs=out_specs,
        out_shape=out_shape,
        scratch_shapes=[pltpu.VMEM((l, RET_V), F32), pltpu.VMEM((RET_QK, RET_V), F32)],
        compiler_params=_cparams(("arbitrary", "arbitrary")),
        name="retention",
    )(*args)
    return res if emit_state else res[0]


def _layer_norm(z, g, b):
    mu = jnp.mean(z, axis=-1, keepdims=True)
    var = jnp.mean(jnp.square(z - mu), axis=-1, keepdims=True)
    return (z - mu) * lax.rsqrt(var + LN_EPS) * g + b


def _pack_rows(hn, hp_ref):
    tm = hn.shape[0]
    half = D // 2
    lo = lax.bitcast_convert_type(hn[:, :half].astype(BF16).astype(F32), U32)
    hi = lax.bitcast_convert_type(hn[:, half:].astype(BF16).astype(F32), U32)
    w = (lo >> 16) | (hi & HI_MASK)
    for s in range(PK_S):
        hp_ref[pl.ds(s, tm, stride=PK_P), :] = w[:, s * PK_L:(s + 1) * PK_L]
    hp_ref[pl.ds(PK_S, tm, stride=PK_P), :] = jnp.zeros((tm, PK_L), U32)


def _unpack_rows(hp_ref, tm):
    lo, hi = [], []
    for s in range(PK_S):
        w = hp_ref[pl.ds(s, tm, stride=PK_P), :]
        lo.append(lax.bitcast_convert_type(w << 16, F32).astype(BF16))
        hi.append(lax.bitcast_convert_type(w & HI_MASK, F32).astype(BF16))
    return jnp.concatenate(lo + hi, axis=1)


def _route_tile(hn, wrt_ref, rb_ref, e_ref, w_ref, rank_ref, cnt_ref):
    tm = hn.shape[0]
    neg = -jnp.inf
    hb = hn.astype(BF16)
    hl = (hn - hb.astype(F32)).astype(BF16)
    logits = _qkt(wrt_ref[0], hb) + _qkt(wrt_ref[0], hl) + _qkt(wrt_ref[1], hb)
    s = _sigmoid(logits)
    biased = s + rb_ref[...]
    io8 = lax.broadcasted_iota(I32, (EXPERTS_PER_GROUP, tm), 0).astype(F32)
    slabs, gscore = [], []
    for g in range(N_GROUPS):
        slab = biased[g * EXPERTS_PER_GROUP:(g + 1) * EXPERTS_PER_GROUP, :]
        m1 = jnp.max(slab, axis=0, keepdims=True)
        i1 = jnp.min(jnp.where(slab == m1, io8, float(EXPERTS_PER_GROUP)), axis=0, keepdims=True)
        m2 = jnp.max(jnp.where(io8 == i1, neg, slab), axis=0, keepdims=True)
        slabs.append(slab)
        gscore.append(m1 + m2)
    masked = []
    for g in range(N_GROUPS):
        ahead = jnp.zeros_like(gscore[g])
        for g2 in range(N_GROUPS):
            if g2 == g:
                continue
            better = (gscore[g2] > gscore[g]) | ((gscore[g2] == gscore[g]) if g2 < g else False)
            ahead = ahead + jnp.where(better, 1.0, 0.0)
        masked.append(jnp.where(ahead < float(TOPK_GROUPS), slabs[g], neg))
    v = jnp.concatenate(masked, axis=0)
    io = lax.broadcasted_iota(I32, (N_EXPERTS, tm), 0).astype(F32)
    idxs, ws, hots = [], [], []
    for _ in range(TOP_K):
        m = jnp.max(v, axis=0, keepdims=True)
        idx = jnp.min(jnp.where(v == m, io, float(N_EXPERTS)), axis=0, keepdims=True)
        hot = io == idx
        ws.append(jnp.sum(jnp.where(hot, s, 0.0), axis=0, keepdims=True))
        v = jnp.where(hot, neg, v)
        idxs.append(idx)
        hots.append(hot)
    wsum = ws[0]
    for k in range(1, TOP_K):
        wsum = wsum + ws[k]
    chosen = jnp.where(hots[0], 1.0, 0.0)
    for k in range(1, TOP_K):
        chosen = chosen + jnp.where(hots[k], 1.0, 0.0)
    upper = jnp.where(lax.broadcasted_iota(I32, (tm, tm), 0) < lax.broadcasted_iota(I32, (tm, tm), 1), 1.0, 0.0)
    before = jnp.dot(chosen.astype(BF16), upper.astype(BF16), preferred_element_type=F32)
    zero_row = jnp.zeros((1, tm), F32)
    for k in range(8):
        if k < TOP_K:
            e_ref[k:k + 1, :] = idxs[k].astype(I32)
            w_ref[k:k + 1, :] = ws[k] / wsum * ROUTED_SCALE
            rank_ref[k:k + 1, :] = jnp.sum(jnp.where(hots[k], before, 0.0), axis=0, keepdims=True).astype(I32)
        else:
            e_ref[k:k + 1, :] = zero_row.astype(I32)
            w_ref[k:k + 1, :] = zero_row
            rank_ref[k:k + 1, :] = zero_row.astype(I32)
    cnt = jnp.sum(chosen, axis=1, keepdims=True)
    cnt_ref[...] = jnp.broadcast_to(cnt, (N_EXPERTS, 128)).astype(I32)


def _outproj_kernel(nk, np_tiles, op_ref, os_ref, w_ref, x_ref, gate_ref, lng_ref, lnb_ref, sh_ref, sc_ref,
                    wrt_ref, rb_ref, xo_ref, hp_ref, e_ref, wt_ref, rank_ref, cnt_ref, acc_ref):
    kk = pl.program_id(1)
    o_blk = jnp.where(pl.program_id(0) < np_tiles, op_ref[...], os_ref[...])
    part = jnp.dot(o_blk, w_ref[...], preferred_element_type=F32)

    @pl.when(kk == 0)
    def _():
        acc_ref[...] = part

    @pl.when(kk > 0)
    def _():
        acc_ref[...] += part

    @pl.when(kk == nk - 1)
    def _():
        xn = _layer_norm(ALPHA * x_ref[...] + gate_ref[...] * acc_ref[...], lng_ref[...], lnb_ref[...])
        xo_ref[...] = xn
        hn = xn * (1.0 + sc_ref[...]) + sh_ref[...]
        _pack_rows(hn, hp_ref)
        _route_tile(hn, wrt_ref, rb_ref, e_ref, wt_ref, rank_ref, cnt_ref)


def outproj_ln(op, os_, w_out, x, mods, ln_g, ln_b, w_router_t, b_router, layer):
    k = op.shape[1]
    tk = 1024
    nk = k // tk
    tm = RT_TM
    ntile = NT // tm
    np_tiles, ns_tiles = NP // tm, NS // tm
    row = lambda m, kk: (m, 0)
    col = lambda m, kk: (0, m)
    return pl.pallas_call(
        functools.partial(_outproj_kernel, nk, np_tiles),
        grid=(ntile, nk),
        in_specs=[pl.BlockSpec((tm, tk), lambda m, kk: (jnp.minimum(m, np_tiles - 1), kk)),
                  pl.BlockSpec((tm, tk), lambda m, kk: (jnp.clip(m - np_tiles, 0, ns_tiles - 1), kk)),
                  pl.BlockSpec((tk, D), lambda m, kk: (kk, 0)),
                  pl.BlockSpec((tm, D), row),
                  _mod_spec(layer, 2, tm),
                  pl.BlockSpec((None, None, 1, D), lambda m, kk: (layer, 0, 0, 0)),
                  pl.BlockSpec((None, None, 1, D), lambda m, kk: (layer, 0, 0, 0)),
                  _mod_spec(layer, 3, tm), _mod_spec(layer, 4, tm),
                  pl.BlockSpec((2, N_EXPERTS, D), lambda m, kk: (0, 0, 0)),
                  pl.BlockSpec((N_EXPERTS, 1), lambda m, kk: (0, 0))],
        out_specs=[pl.BlockSpec((tm, D), row),
                   pl.BlockSpec((tm * PK_P, PK_L), row),
                   pl.BlockSpec((8, tm), col), pl.BlockSpec((8, tm), col), pl.BlockSpec((8, tm), col),
                   pl.BlockSpec((None, N_EXPERTS, 128), lambda m, kk: (m, 0, 0))],
        out_shape=[jax.ShapeDtypeStruct((NT, D), F32),
                   jax.ShapeDtypeStruct((NT * PK_P, PK_L), U32),
                   jax.ShapeDtypeStruct((8, NT), I32), jax.ShapeDtypeStruct((8, NT), F32),
                   jax.ShapeDtypeStruct((8, NT), I32),
                   jax.ShapeDtypeStruct((ntile, N_EXPERTS, 128), I32)],
        scratch_shapes=[pltpu.VMEM((tm, D), F32)],
        compiler_params=_cparams(("arbitrary", "arbitrary")),
        name="outproj_ln_route",
    )(op, os_, w_out, x, mods, ln_g, ln_b, mods, mods, w_router_t, b_router.reshape(N_EXPERTS, 1))


def _outproj_split_kernel(np_tiles, x_split, op_ref, os_ref, w_ref, *refs):
    if x_split:
        xp_ref, xs_ref, *refs = refs
    else:
        x_ref, *refs = refs
    (gate_ref, lng_ref, lnb_ref, sh_ref, sc_ref, wrt_ref, rb_ref,
     xo_ref, hp_ref, e_ref, wt_ref, rank_ref, cnt_ref) = refs
    m = pl.program_id(0)
    o_blk = jnp.where(m < np_tiles, op_ref[...], os_ref[...])
    y = jnp.dot(o_blk, w_ref[...], preferred_element_type=F32)
    x = jnp.where(m < np_tiles, xp_ref[...], xs_ref[...]) if x_split else x_ref[...]
    xn = _layer_norm(ALPHA * x + gate_ref[...] * y, lng_ref[...], lnb_ref[...])
    xo_ref[...] = xn
    hn = xn * (1.0 + sc_ref[...]) + sh_ref[...]
    _pack_rows(hn, hp_ref)
    _route_tile(hn, wrt_ref, rb_ref, e_ref, wt_ref, rank_ref, cnt_ref)


def outproj_ln_split(op, os_, w_out, x, mods, ln_g, ln_b, w_router_t, b_router, layer):
    tm = RT_TM
    ntile = NT // tm
    np_tiles = NP // tm
    row = lambda m: (m, 0)
    col = lambda m: (0, m)
    mod = lambda chunk: _mod_spec(layer, chunk, tm)
    const2 = lambda m: (0, 0)
    x_split = isinstance(x, tuple)
    x_args = list(x) if x_split else [x]
    x_specs = list(_split_rows_specs(tm)) if x_split else [pl.BlockSpec((tm, D), row)]
    return pl.pallas_call(
        functools.partial(_outproj_split_kernel, np_tiles, x_split),
        grid=(ntile,),
        in_specs=[*_split_rows_specs(tm),
                  pl.BlockSpec((D, D), const2, pipeline_mode=pl.Buffered(1)),
                  *x_specs,
                  mod(2),
                  pl.BlockSpec((None, None, 1, D), lambda m: (layer, 0, 0, 0)),
                  pl.BlockSpec((None, None, 1, D), lambda m: (layer, 0, 0, 0)),
                  mod(3), mod(4),
                  pl.BlockSpec((2, N_EXPERTS, D), lambda m: (0, 0, 0)),
                  pl.BlockSpec((N_EXPERTS, 1), const2)],
        out_specs=[pl.BlockSpec((tm, D), row),
                   pl.BlockSpec((tm * PK_P, PK_L), row),
                   pl.BlockSpec((8, tm), col), pl.BlockSpec((8, tm), col), pl.BlockSpec((8, tm), col),
                   pl.BlockSpec((None, N_EXPERTS, 128), lambda m: (m, 0, 0))],
        out_shape=[jax.ShapeDtypeStruct((NT, D), F32),
                   jax.ShapeDtypeStruct((NT * PK_P, PK_L), U32),
                   jax.ShapeDtypeStruct((8, NT), I32), jax.ShapeDtypeStruct((8, NT), F32),
                   jax.ShapeDtypeStruct((8, NT), I32),
                   jax.ShapeDtypeStruct((ntile, N_EXPERTS, 128), I32)],
        compiler_params=_cparams(("arbitrary",)),
        name="outproj_ln_route_split",
    )(op, os_, w_out, *x_args, mods, ln_g, ln_b, mods, mods, w_router_t, b_router.reshape(N_EXPERTS, 1))


def _slot_plan(e_t, rank_t, cnt):
    cnt = cnt[:, :, 0]
    tile_base = jnp.cumsum(cnt, axis=0) - cnt
    counts = jnp.sum(cnt, axis=0)
    padded = (counts + MOE_TB - 1) // MOE_TB * MOE_TB
    pad_end = jnp.cumsum(padded)
    pad_start = pad_end - padded
    offs = jnp.repeat(pad_start[None, :] + tile_base, NT // cnt.shape[0], axis=0)
    hot = e_t[:TOP_K, :, None] == jnp.arange(N_EXPERTS, dtype=I32)[None, None, :]
    dest = jnp.sum(jnp.where(hot, offs[None], 0), axis=-1) + rank_t[:TOP_K]
    ntile = NT // CMB_TM
    dest_tiles = dest.reshape(TOP_K, ntile, CMB_TM).transpose(1, 0, 2).reshape(ntile, 1, TOP_K * CMB_TM)
    nused = pad_end[-1] // MOE_TB
    blk = jnp.minimum(jnp.arange(MOE_NBLK, dtype=I32), nused - 1) * MOE_TB
    block_e = jnp.minimum(jnp.sum((pad_end[None, :] <= blk[:, None]).astype(I32), axis=1), N_EXPERTS - 1)
    first = jnp.concatenate([jnp.ones((1,), I32), (block_e[1:] != block_e[:-1]).astype(I32)])
    zero_start = pad_start + counts
    ids = jnp.arange(N_EXPERTS, dtype=I32)
    used = counts > 0
    later = jnp.where(used[None, :] & (ids[None, :] > ids[:, None]), ids[None, :], N_EXPERTS)
    nxt = jnp.min(later, axis=1)
    nxt = jnp.where(nxt == N_EXPERTS, ids, nxt)
    par = (jnp.cumsum(used.astype(I32)) - 1) % 2
    owner = block_e[:, None] == ids[None, :]
    nxt_b = jnp.sum(jnp.where(owner, nxt[None, :], 0), axis=1)
    par_b = jnp.sum(jnp.where(owner, par[None, :], 0), axis=1)
    return (dest_tiles.astype(I32), block_e.astype(I32), first, nused.reshape(1).astype(I32),
            zero_start.astype(I32), nxt_b.astype(I32), par_b.astype(I32))


_DISPATCH_ROWS = TOP_K * CMB_TM
_CMB_ROWS = 32


def _dispatch_kernel(zs_ref, nused_ref, dest_ref, hp_ref, wsg_ref, wsu_ref, wsd_ref, xs_hbm, shared_ref,
                     stage, zbuf, zsem, sem):
    i = pl.program_id(0)
    n = pl.num_programs(0)
    slot = i % 2

    @pl.when(i == 0)
    def _():
        zbuf[...] = jnp.zeros_like(zbuf)

        def zcopy(slot0, nslot):
            return pltpu.make_async_copy(zbuf.at[pl.ds(0, nslot * PK_P)],
                                         xs_hbm.at[pl.ds(slot0 * PK_P, nslot * PK_P)], zsem)

        def pad_pieces(e, wait):
            slot0 = zs_ref[e]
            npad = (MOE_TB - slot0 % MOE_TB) % MOE_TB
            piece = MOE_TB // 2
            while piece >= 1:
                @pl.when((npad & piece) != 0)
                def _(slot0=slot0, piece=piece):
                    cp = zcopy(slot0, piece)
                    cp.wait() if wait else cp.start()
                slot0 = slot0 + (npad & piece)
                piece //= 2

        def zstart(e, carry):
            pad_pieces(e, False)
            return carry

        def zwait(e, carry):
            pad_pieces(e, True)
            return carry

        def tstart(b, carry):
            zcopy(b * MOE_TB, MOE_TB).start()
            return carry

        def twait(b, carry):
            zcopy(b * MOE_TB, MOE_TB).wait()
            return carry

        lax.fori_loop(0, N_EXPERTS, zstart, 0)
        lax.fori_loop(nused_ref[0], MOE_NBLK + 1, tstart, 0)
        lax.fori_loop(0, N_EXPERTS, zwait, 0)
        lax.fori_loop(nused_ref[0], MOE_NBLK + 1, twait, 0)

    stage[slot] = hp_ref[...]

    def issue(r, carry):
        for k in range(TOP_K):
            dst = dest_ref[0, 0, k * CMB_TM + r] * PK_P
            pltpu.make_async_copy(stage.at[slot, pl.ds(r * PK_P, PK_P)], xs_hbm.at[pl.ds(dst, PK_P)],
                                  sem.at[slot]).start(priority=k % 2)
        return carry

    lax.fori_loop(0, CMB_TM, issue, 0, unroll=2)

    hb = _unpack_rows(hp_ref, CMB_TM)
    a = (_silu(jnp.dot(hb, wsg_ref[...], preferred_element_type=F32))
         * jnp.dot(hb, wsu_ref[...], preferred_element_type=F32))
    shared_ref[...] = jnp.dot(a.astype(BF16), wsd_ref[...], preferred_element_type=F32)

    def drain(s):
        for k in range(TOP_K):
            pltpu.make_async_copy(stage.at[s], xs_hbm.at[pl.ds(0, CMB_TM * PK_P)], sem.at[s]).wait()

    @pl.when(i > 0)
    def _():
        drain(1 - slot)

    @pl.when(i == n - 1)
    def _():
        drain(slot)


def moe_dispatch(hp, dest_tiles, zero_start, nused, ws_gate, ws_up, ws_down):
    ntile = NT // CMB_TM
    const = lambda i, zs, nu: (0, 0)
    grid_spec = pltpu.PrefetchScalarGridSpec(
        num_scalar_prefetch=2,
        grid=(ntile,),
        in_specs=[pl.BlockSpec((1, 1, _DISPATCH_ROWS), lambda i, zs, nu: (i, 0, 0), memory_space=pltpu.SMEM),
                  pl.BlockSpec((CMB_TM * PK_P, PK_L), lambda i, zs, nu: (i, 0)),
                  pl.BlockSpec((D, D_SHARED), const, pipeline_mode=pl.Buffered(1)),
                  pl.BlockSpec((D, D_SHARED), const, pipeline_mode=pl.Buffered(1)),
                  pl.BlockSpec((D_SHARED, D), const, pipeline_mode=pl.Buffered(1))],
        out_specs=[pl.BlockSpec(memory_space=pl.ANY),
                   pl.BlockSpec((CMB_TM, D), lambda i, zs, nu: (i, 0))],
        scratch_shapes=[pltpu.VMEM((2, CMB_TM * PK_P, PK_L), U32),
                        pltpu.VMEM((MOE_TB * PK_P, PK_L), U32),
                        pltpu.SemaphoreType.DMA(()), pltpu.SemaphoreType.DMA((2,))],
    )
    return pl.pallas_call(
        _dispatch_kernel,
        grid_spec=grid_spec,
        out_shape=[jax.ShapeDtypeStruct(((MOE_CAP + MOE_TB) * PK_P, PK_L), U32),
                   jax.ShapeDtypeStruct((NT, D), F32)],
        compiler_params=_cparams(("arbitrary",)),
        name="moe_dispatch",
    )(zero_start, nused, dest_tiles, hp, ws_gate, ws_up, ws_down)


def _moe_kernel(layer, be_ref, first_ref, nused_ref, nxt_ref, par_ref, xs_ref, wg_hbm, wu_hbm, wd_hbm, y_ref,
                wgs, wus, wds, wgb, wub, wdb, sem):
    i = pl.program_id(0)

    def weight_copies(e, s):
        return (pltpu.make_async_copy(wg_hbm.at[layer, e], wgs.at[s], sem.at[s]),
                pltpu.make_async_copy(wu_hbm.at[layer, e], wus.at[s], sem.at[s]),
                pltpu.make_async_copy(wd_hbm.at[layer, e], wds.at[s], sem.at[s]))

    @pl.when(i == 0)
    def _():
        for cp in weight_copies(be_ref[0], par_ref[0]):
            cp.start()

    @pl.when(first_ref[i] == 1)
    def _():
        e, s = be_ref[i], par_ref[i]
        for cp in weight_copies(e, s):
            cp.wait()

        @pl.when(nxt_ref[i] != e)
        def _():
            for cp in weight_copies(nxt_ref[i], 1 - s):
                cp.start()

        wgb[...] = wgs[s].astype(BF16)
        wub[...] = wus[s].astype(BF16)
        wdb[...] = wds[s].astype(BF16)

    @pl.when(i < nused_ref[0])
    def _():
        x = _unpack_rows(xs_ref, MOE_TB)
        a = (_silu(jnp.dot(x, wgb[...], preferred_element_type=F32))
             * jnp.dot(x, wub[...], preferred_element_type=F32))
        y = jnp.dot(a.astype(BF16), wdb[...], preferred_element_type=F32)
        for j in range(Y_S):
            y_ref[pl.ds(j, MOE_TB, stride=Y_P), :] = y[:, j * 128:(j + 1) * 128]
        y_ref[pl.ds(Y_S, MOE_TB, stride=Y_P), :] = jnp.zeros((MOE_TB, 128), F32)

    @pl.when(i >= nused_ref[0])
    def _():
        y_ref[...] = jnp.zeros_like(y_ref)


def moe_experts(xs, block_e, first, nused, nxt_e, par, w_gate, w_up, w_down, layer):
    blk = lambda i, be, fi, nu, nx, pa: (jnp.minimum(i, nu[0] - 1), 0)
    grid_spec = pltpu.PrefetchScalarGridSpec(
        num_scalar_prefetch=5,
        grid=(MOE_NBLK,),
        in_specs=[pl.BlockSpec((MOE_TB * PK_P, PK_L), blk),
                  pl.BlockSpec(memory_space=pl.ANY), pl.BlockSpec(memory_space=pl.ANY),
                  pl.BlockSpec(memory_space=pl.ANY)],
        out_specs=pl.BlockSpec((MOE_TB * Y_P, 128), lambda i, be, fi, nu, nx, pa: (i, 0)),
        scratch_shapes=[pltpu.VMEM((2, D, D_EXPERT), F32), pltpu.VMEM((2, D, D_EXPERT), F32),
                        pltpu.VMEM((2, D_EXPERT, D), F32),
                        pltpu.VMEM((D, D_EXPERT), BF16), pltpu.VMEM((D, D_EXPERT), BF16),
                        pltpu.VMEM((D_EXPERT, D), BF16),
                        pltpu.SemaphoreType.DMA((2,))],
    )
    return pl.pallas_call(
        functools.partial(_moe_kernel, layer),
        grid_spec=grid_spec,
        out_shape=jax.ShapeDtypeStruct((MOE_CAP * Y_P, 128), F32),
        compiler_params=_cparams(("arbitrary",)),
        name="moe_experts",
    )(block_e, first, nused, nxt_e, par, xs, w_gate, w_up, w_down)


def _combine_kernel(has_next, dest_ref, destn_ref, y_hbm, w_ref, shared_ref, x_ref,
                    gate_ref, lng_ref, lnb_ref, *rest):
    if has_next:
        sh_ref, sc_ref, xo_ref, ho_ref, buf, routed_ref, sem = rest
    else:
        xo_ref, buf, routed_ref, sem = rest
    i = pl.program_id(0)
    n = pl.num_programs(0)
    slot = i % 2
    nrow = TOP_K * CMB_TM

    def gather(dref, s):
        def body(r, carry):
            for k in range(TOP_K):
                src = dref[0, 0, k * CMB_TM + r] * Y_P
                dst = (k * CMB_TM + r) * Y_P
                pltpu.make_async_copy(y_hbm.at[pl.ds(src, Y_S)], buf.at[s, pl.ds(dst, Y_S)],
                                      sem.at[s]).start(priority=k % 2)
            return carry
        lax.fori_loop(0, CMB_TM, body, 0, unroll=2)

    @pl.when(i == 0)
    def _():
        gather(dest_ref, 0)

    @pl.when(i + 1 < n)
    def _():
        gather(destn_ref, 1 - slot)

    pltpu.make_async_copy(y_hbm.at[pl.ds(0, nrow * Y_S)], buf.at[slot, pl.ds(0, nrow * Y_S)],
                          sem.at[slot]).wait()
    for r0 in range(0, CMB_TM, _CMB_ROWS):
        wsub = w_ref[r0:r0 + _CMB_ROWS, :]
        wb = [jnp.broadcast_to(wsub[:, k:k + 1], (_CMB_ROWS, 128)) for k in range(TOP_K)]
        for j in range(Y_S):
            acc = buf[slot, pl.ds(r0 * Y_P + j, _CMB_ROWS, stride=Y_P), :] * wb[0]
            for k in range(1, TOP_K):
                acc = acc + buf[slot, pl.ds((k * CMB_TM + r0) * Y_P + j, _CMB_ROWS, stride=Y_P), :] * wb[k]
            routed_ref[r0:r0 + _CMB_ROWS, j * 128:(j + 1) * 128] = acc
    routed = routed_ref[...]
    xn = _layer_norm(ALPHA * x_ref[...] + gate_ref[...] * (routed + shared_ref[...]), lng_ref[...], lnb_ref[...])
    xo_ref[...] = xn
    if has_next:
        ho_ref[...] = (xn * (1.0 + sc_ref[...]) + sh_ref[...]).astype(ho_ref.dtype)


def moe_combine(y_slots, dest_tiles, w_tok, shared, x1, mods, ln_g, ln_b, layer):
    has_next = layer + 1 < DEPTH
    tm = CMB_TM
    ntile = NT // tm
    row = lambda m: (m, 0)
    in_specs = [pl.BlockSpec((1, 1, TOP_K * tm), lambda m: (m, 0, 0), memory_space=pltpu.SMEM),
                pl.BlockSpec((1, 1, TOP_K * tm), lambda m: (jnp.minimum(m + 1, ntile - 1), 0, 0),
                             memory_space=pltpu.SMEM),
                pl.BlockSpec(memory_space=pl.ANY),
                pl.BlockSpec((tm, 8), row),
                pl.BlockSpec((tm, D), row), pl.BlockSpec((tm, D), row),
                _mod_spec(layer, 5, tm),
                pl.BlockSpec((None, None, 1, D), lambda m: (layer, 1, 0, 0)),
                pl.BlockSpec((None, None, 1, D), lambda m: (layer, 1, 0, 0))]
    args = [dest_tiles, dest_tiles, y_slots, w_tok, shared, x1, mods, ln_g, ln_b]
    out_specs = [pl.BlockSpec((tm, D), row)]
    out_shape = [jax.ShapeDtypeStruct((NT, D), F32)]
    if has_next:
        in_specs += [_mod_spec(layer + 1, 0, tm), _mod_spec(layer + 1, 1, tm)]
        args += [mods, mods]
        out_specs.append(pl.BlockSpec((tm, D), row))
        out_shape.append(jax.ShapeDtypeStruct((NT, D), BF16))
    res = pl.pallas_call(
        functools.partial(_combine_kernel, has_next),
        grid=(ntile,),
        in_specs=in_specs,
        out_specs=out_specs,
        out_shape=out_shape,
        scratch_shapes=[pltpu.VMEM((2, TOP_K * tm * Y_P, 128), F32), pltpu.VMEM((tm, D), F32),
                        pltpu.SemaphoreType.DMA((2,))],
        compiler_params=_cparams(("arbitrary",)),
        name="moe_combine",
    )(*args)
    return (res[0], res[1]) if has_next else (res[0], None)


def _split_bf16(w):
    hi = w.astype(BF16)
    lo = (w - hi.astype(F32)).astype(BF16)
    return jnp.stack([hi, lo])


def kernel(x_prompt, x_sample, cache_diff_k, cache_diff_v, cache_na_k, cache_na_v, state_ret, c, c_ctx, ada_w, ada_b, ln_g, ln_b, diff_w_in, diff_w_out, diff_lambda, diff_subln_g, na_w_in, na_w_out, na_rpb, ret_w_in, ret_w_out, ret_decay, moe_router, moe_router_bias, moe_w_gate, moe_w_up, moe_w_down, shared_w_gate, shared_w_up, shared_w_down):
    x = (x_prompt.reshape(NP, D), x_sample.reshape(NS, D))
    cond = jnp.concatenate([c_ctx[None], c, jnp.zeros((N_COND - 1 - DEC_BATCH, D), F32)], axis=0)
    mods = modulation_table(cond, ada_w, ada_b)
    ln_g4 = ln_g.reshape(DEPTH, 2, 1, D)
    ln_b4 = ln_b.reshape(DEPTH, 2, 1, D)
    cos, sin = _rope_tables()
    h = modulate(*x, mods, 0)
    new_dk, new_dv, new_nk, new_nv, new_rs = [], [], [], [], []
    for i in range(DEPTH):
        j = i // N_MIXERS
        kind = i % N_MIXERS
        if kind == 0:
            qp = matmul(h, diff_w_in, j, 0, D, 0, NP, BF16)
            kp = matmul(h, diff_w_in, j, D, D, 0, NP, F32)
            vp = matmul(h, diff_w_in, j, 2 * D, D, 0, NP, F32)
            qs = matmul_rope(h, diff_w_in, j, 0, D, cos, sin)
            ks = matmul_rope(h, diff_w_in, j, D, D, cos, sin)
            vs = matmul(h, diff_w_in, j, 2 * D, D, NP, NS, BF16)
            lam_init = 0.8 - 0.6 * math.exp(-0.3 * i)
            rows = lambda n: pl.BlockSpec((n, D), lambda b, t: (b, 0))
            op = diff_attention(qp, [(kp, vp, rows(SEQ))], diff_lambda[j], diff_subln_g[j], lam_init,
                                BATCH, SEQ, SEQ)
            ctx = pl.BlockSpec((None, None, PAST, D), lambda b, t, j=j: (b, j, 0, 0))
            os_ = diff_attention(qs, [(cache_diff_k.reshape(DEC_BATCH, -1, PAST, D),
                                       cache_diff_v.reshape(DEC_BATCH, -1, PAST, D), ctx),
                                      (ks, vs, rows(DEC_SEQ))],
                                 diff_lambda[j], diff_subln_g[j], lam_init, DEC_BATCH, DEC_SEQ, 256)
            new_dk.append(kp.reshape(BATCH, SEQ, DIFF_HEADS, 2 * DIFF_QK))
            new_dv.append(vp.reshape(BATCH, SEQ, DIFF_HEADS, DIFF_V))
            w_out = diff_w_out[j]
        elif kind == 1:
            qp = matmul(h, na_w_in, j, 0, D, 0, NP, BF16)
            kp = matmul(h, na_w_in, j, D, D, 0, NP, F32)
            vp = matmul(h, na_w_in, j, 2 * D, D, 0, NP, F32)
            qs = matmul(h, na_w_in, j, 0, D, NP, NS, BF16)
            ks = matmul(h, na_w_in, j, D, D, NP, NS, BF16)
            vs = matmul(h, na_w_in, j, 2 * D, D, NP, NS, BF16)
            op = na_context_attention(qp, kp, vp)
            os_ = na_latent_attention(qs, ks, vs,
                                      cache_na_k[:, j].reshape(DEC_BATCH, PAST, D).astype(BF16),
                                      cache_na_v[:, j].reshape(DEC_BATCH, PAST, D).astype(BF16),
                                      _na_bias_tables(na_rpb[j]))
            new_nk.append(kp.reshape(BATCH, SEQ, NA_HEADS, NA_DH))
            new_nv.append(vp.reshape(BATCH, SEQ, NA_HEADS, NA_DH))
            w_out = na_w_out[j]
        else:
            qd = RET_HEADS * RET_QK
            vd = RET_HEADS * RET_V
            log_g = -jax.nn.softplus(-ret_decay[j].astype(F32))
            outs = []
            for row0, nrows, nb, l in ((0, NP, BATCH, SEQ), (NP, NS, DEC_BATCH, DEC_SEQ)):
                q_ = matmul(h, ret_w_in, j, 0, qd, row0, nrows, BF16)
                k_ = matmul(h, ret_w_in, j, qd, qd, row0, nrows, F32)
                v_ = matmul(h, ret_w_in, j, 2 * qd, vd, row0, nrows, BF16)
                g_ = matmul(h, ret_w_in, j, 2 * qd + vd, vd, row0, nrows, F32)
                if row0 == 0:
                    o_, st = retention(q_, k_, v_, g_, log_g, nb, l, None, True)
                    new_rs.append(st)
                else:
                    o_ = retention(q_, k_, v_, g_, log_g, nb, l, state_ret[:, j], False)
                outs.append(o_)
            op, os_ = outs
            w_out = ret_w_out[j]
        if w_out.shape[0] == D:
            x1, hp, e_t, w_t, rank_t, cnt = outproj_ln_split(
                op, os_, w_out.astype(BF16), x, mods, ln_g4, ln_b4,
                _split_bf16(moe_router[i].T), moe_router_bias[i].astype(F32), i)
        else:
            x1, hp, e_t, w_t, rank_t, cnt = outproj_ln(
                op, os_, w_out.astype(BF16), x, mods, ln_g4, ln_b4,
                _split_bf16(moe_router[i].T), moe_router_bias[i].astype(F32), i)
        dest_tiles, block_e, first, nused, zero_start, nxt_e, par = _slot_plan(e_t, rank_t, cnt)
        xs, shared = moe_dispatch(hp, dest_tiles, zero_start, nused, shared_w_gate[i].astype(BF16),
                                  shared_w_up[i].astype(BF16), shared_w_down[i].astype(BF16))
        y_slots = moe_experts(xs, block_e, first, nused, nxt_e, par, moe_w_gate, moe_w_up, moe_w_down, i)
        x, h = moe_combine(y_slots, dest_tiles, w_t.T, shared, x1, mods, ln_g4, ln_b4, i)
    return (x[:NP].reshape(BATCH, SEQ, D), x[NP:].reshape(DEC_BATCH, DEC_SEQ, D),
            jnp.stack(new_dk, axis=1), jnp.stack(new_dv, axis=1),
            jnp.stack(new_nk, axis=1), jnp.stack(new_nv, axis=1), jnp.stack(new_rs, axis=1))
```

```python
import functools
import math

import jax
import jax.numpy as jnp
import numpy as np
from jax import lax
from jax.experimental import pallas as pl
from jax.experimental.pallas import tpu as pltpu

F32 = jnp.float32
BF16 = jnp.bfloat16
U32 = jnp.uint32
I32 = jnp.int32

D = 2048
BATCH = 32
SEQ = 256
DEPTH = 4
DEC_BATCH = 8
DEC_SEQ = 1024
PAST = 512
NP = BATCH * SEQ
NS = DEC_BATCH * DEC_SEQ
NT = NP + NS
GRID_W = 64
N_MIXERS = 3
DIFF_HEADS = 8
DIFF_QK = 128
DIFF_V = 256
NA_HEADS = 16
NA_DH = 128
NA_KH = 8
NA_KH_MAX = 8
NA_KW = 16
RET_HEADS = 8
RET_QK = 256
RET_V = 512
RET_CHUNK = 128
N_EXPERTS = 64
N_GROUPS = 8
EXPERTS_PER_GROUP = 8
TOPK_GROUPS = 4
TOP_K = 6
D_EXPERT = 512
D_SHARED = 512
ROUTED_SCALE = 2.5
ROPE_BASE = 10000.0
LN_EPS = 1e-5
ALPHA = (2 * DEPTH) ** 0.25
N_COND = 16

VMEM_LIMIT = 56 * 1024 * 1024

MOE_TB = 256
MOE_NBLK = (NT * TOP_K + N_EXPERTS * (MOE_TB - 1)) // MOE_TB
MOE_CAP = MOE_NBLK * MOE_TB
RT_TM = 512
CMB_TM = 256
PK_S = 8
PK_L = D // 2 // PK_S
Y_S = D // 128
PK_P = PK_S + 1
Y_P = Y_S + 1
HI_MASK = np.uint32(0xFFFF0000)


def _cparams(sem):
    return pltpu.CompilerParams(dimension_semantics=sem, vmem_limit_bytes=VMEM_LIMIT)


def _cond_row(row0):
    return jnp.where(row0 < NP, 0, 1 + (row0 - NP) // DEC_SEQ)


def _mod_spec(layer, chunk, tm, moff=0):
    return pl.BlockSpec((None, None, None, 1, D),
                        lambda m, *_: (layer, _cond_row((m + moff) * tm), chunk, 0, 0))


def _sigmoid(x):
    return 1.0 / (1.0 + jnp.exp(-x))


def _silu(x):
    return x * _sigmoid(x)


def _mod_kernel(cond_ref, w_ref, b_ref, o_ref):
    c = _silu(cond_ref[...]).astype(BF16)
    o_ref[...] = jnp.dot(c, w_ref[...].astype(BF16), preferred_element_type=F32) + b_ref[...]


def modulation_table(cond, ada_w, ada_b):
    tn = 1024
    n6 = 6 * D
    out = pl.pallas_call(
        _mod_kernel,
        grid=(DEPTH, n6 // tn),
        in_specs=[pl.BlockSpec((N_COND, D), lambda l, n: (0, 0)),
                  pl.BlockSpec((None, D, tn), lambda l, n: (l, 0, n)),
                  pl.BlockSpec((None, 1, tn), lambda l, n: (l, 0, n))],
        out_specs=pl.BlockSpec((None, N_COND, tn), lambda l, n: (l, 0, n)),
        out_shape=jax.ShapeDtypeStruct((DEPTH, N_COND, n6), F32),
        compiler_params=_cparams(("arbitrary", "arbitrary")),
        name="modulation_table",
    )(cond, ada_w, ada_b.reshape(DEPTH, 1, n6))
    return out.reshape(DEPTH, N_COND, 6, 1, D)


def _split_rows_specs(tm):
    np_tiles, ns_tiles = NP // tm, NS // tm
    return (pl.BlockSpec((tm, D), lambda m, *_: (jnp.minimum(m, np_tiles - 1), 0)),
            pl.BlockSpec((tm, D), lambda m, *_: (jnp.clip(m - np_tiles, 0, ns_tiles - 1), 0)))


def _modulate_kernel(np_tiles, xp_ref, xs_ref, sh_ref, sc_ref, o_ref):
    x = jnp.where(pl.program_id(0) < np_tiles, xp_ref[...], xs_ref[...])
    o_ref[...] = (x * (1.0 + sc_ref[...]) + sh_ref[...]).astype(o_ref.dtype)


def modulate(xp, xs, mods, layer):
    tm = 512
    return pl.pallas_call(
        functools.partial(_modulate_kernel, NP // tm),
        grid=(NT // tm,),
        in_specs=[*_split_rows_specs(tm), _mod_spec(layer, 0, tm), _mod_spec(layer, 1, tm)],
        out_specs=pl.BlockSpec((tm, D), lambda m: (m, 0)),
        out_shape=jax.ShapeDtypeStruct((NT, D), BF16),
        compiler_params=_cparams(("arbitrary",)),
        name="modulate",
    )(xp, xs, mods, mods)


def _mm_kernel(a_ref, w_ref, o_ref, wb_ref):
    @pl.when(pl.program_id(1) == 0)
    def _():
        wb_ref[...] = w_ref[...].astype(BF16)

    o_ref[...] = jnp.dot(a_ref[...], wb_ref[...], preferred_element_type=F32).astype(o_ref.dtype)


def matmul(a, w, layer, col0, ncols, row0, nrows, out_dtype):
    k = a.shape[1]
    tm, tn = 1024, 1024
    moff = row0 // tm
    joff = col0 // tn
    return pl.pallas_call(
        _mm_kernel,
        grid=(ncols // tn, nrows // tm),
        in_specs=[pl.BlockSpec((tm, k), lambda j, m: (m + moff, 0)),
                  pl.BlockSpec((None, k, tn), lambda j, m: (layer, 0, j + joff))],
        out_specs=pl.BlockSpec((tm, tn), lambda j, m: (m, j)),
        out_shape=jax.ShapeDtypeStruct((nrows, ncols), out_dtype),
        scratch_shapes=[pltpu.VMEM((k, tn), BF16)],
        compiler_params=_cparams(("arbitrary", "arbitrary")),
        name="in_proj",
    )(a, w)


def _rope_tables():
    half = DIFF_QK // 2
    t = jnp.arange(DEC_SEQ)
    inv = ROPE_BASE ** (-jnp.arange(0, half, 2, dtype=F32) / half)

    def cs(pos):
        ang = pos.astype(F32)[:, None] * inv[None, :]
        return jnp.cos(ang), jnp.sin(ang)

    cr, sr = cs(t // GRID_W)
    cc, sc = cs(t % GRID_W)
    cos = jnp.concatenate([cr, cr, cc, cc], axis=-1)
    sin = jnp.concatenate([-sr, sr, -sc, sc], axis=-1)
    return cos, sin


def _rope_store(x, cos, sin, o_ref):
    lane = lax.broadcasted_iota(jnp.int32, cos.shape, 1)
    first = (lane % (DIFF_QK // 2)) < (DIFF_QK // 4)
    for g in range(x.shape[1] // DIFF_QK):
        xg = x[:, g * DIFF_QK:(g + 1) * DIFF_QK]
        sw = jnp.where(first, pltpu.roll(xg, DIFF_QK - DIFF_QK // 4, 1), pltpu.roll(xg, DIFF_QK // 4, 1))
        o_ref[:, g * DIFF_QK:(g + 1) * DIFF_QK] = (xg * cos + sw * sin).astype(o_ref.dtype)


def _mm_rope_kernel(a_ref, w_ref, cos_ref, sin_ref, o_ref, wb_ref):
    @pl.when(pl.program_id(1) == 0)
    def _():
        wb_ref[...] = w_ref[...].astype(BF16)

    y = jnp.dot(a_ref[...], wb_ref[...], preferred_element_type=F32)
    _rope_store(y, cos_ref[...], sin_ref[...], o_ref)


def matmul_rope(a, w, layer, col0, ncols, cos, sin):
    k = a.shape[1]
    tm, tn = DEC_SEQ, 1024
    moff = NP // tm
    joff = col0 // tn
    tab = pl.BlockSpec((DEC_SEQ, DIFF_QK), lambda j, m: (0, 0))
    return pl.pallas_call(
        _mm_rope_kernel,
        grid=(ncols // tn, NS // tm),
        in_specs=[pl.BlockSpec((tm, k), lambda j, m: (m + moff, 0)),
                  pl.BlockSpec((None, k, tn), lambda j, m: (layer, 0, j + joff)),
                  tab, tab],
        out_specs=pl.BlockSpec((tm, tn), lambda j, m: (m, j)),
        out_shape=jax.ShapeDtypeStruct((NS, ncols), BF16),
        scratch_shapes=[pltpu.VMEM((k, tn), BF16)],
        compiler_params=_cparams(("arbitrary", "arbitrary")),
        name="in_proj_rope",
    )(a, w, cos, sin)


def _qkt(q, k):
    return lax.dot_general(q, k, (((1,), (1,)), ((), ())), preferred_element_type=F32)


def _softmax(s):
    m = jnp.max(s, axis=-1, keepdims=True)
    p = jnp.exp(s - m)
    return p / jnp.sum(p, axis=-1, keepdims=True)


def _softmax_parts(parts, scale):
    m = jnp.max(parts[0], axis=-1, keepdims=True)
    for s in parts[1:]:
        m = jnp.maximum(m, jnp.max(s, axis=-1, keepdims=True))
    c = scale * math.log2(math.e)
    ps = [jnp.exp2((s - m) * c) for s in parts]
    den = jnp.sum(ps[0], axis=-1, keepdims=True)
    for p in ps[1:]:
        den = den + jnp.sum(p, axis=-1, keepdims=True)
    return [p / den for p in ps]


def _diff_kernel(lam_init, nsrc, lamp_ref, g_ref, q_ref, *refs):
    kv_refs, o_ref = refs[:2 * nsrc], refs[2 * nsrc]
    lp = lamp_ref[...]
    lam = (jnp.exp(jnp.sum(lp[0:1] * lp[1:2], axis=-1, keepdims=True))
           - jnp.exp(jnp.sum(lp[2:3] * lp[3:4], axis=-1, keepdims=True)) + lam_init)
    scale = DIFF_QK ** -0.5
    g = g_ref[...]
    for h in range(DIFF_HEADS):
        c0 = h * DIFF_V
        q = q_ref[:, c0:c0 + DIFF_V].astype(BF16)
        ks = [kv_refs[2 * i][:, c0:c0 + DIFF_V].astype(BF16) for i in range(nsrc)]
        a1 = _softmax_parts([_qkt(q[:, :DIFF_QK], k[:, :DIFF_QK]) for k in ks], scale)
        a2 = _softmax_parts([_qkt(q[:, DIFF_QK:], k[:, DIFF_QK:]) for k in ks], scale)
        o = None
        for i in range(nsrc):
            a = (a1[i] - lam * a2[i]).astype(BF16)
            v = kv_refs[2 * i + 1][:, c0:c0 + DIFF_V].astype(BF16)
            part = jnp.dot(a, v, preferred_element_type=F32)
            o = part if o is None else o + part
        o = o * lax.rsqrt(jnp.mean(jnp.square(o), axis=-1, keepdims=True) + LN_EPS) * g
        o_ref[:, c0:c0 + DIFF_V] = (o * (1.0 - lam_init)).astype(o_ref.dtype)


def diff_attention(q, kv_srcs, lam_p, subln_g, lam_init, nb, lq, tq):
    nq = lq // tq
    in_specs = [pl.BlockSpec((4, DIFF_QK), lambda b, i: (0, 0)),
                pl.BlockSpec((1, DIFF_V), lambda b, i: (0, 0)),
                pl.BlockSpec((tq, D), lambda b, i: (b * nq + i, 0))]
    args = [lam_p, subln_g.reshape(1, DIFF_V), q]
    for k, v, spec in kv_srcs:
        in_specs += [spec, spec]
        args += [k, v]
    return pl.pallas_call(
        functools.partial(_diff_kernel, lam_init, len(kv_srcs)),
        grid=(nb, nq),
        in_specs=in_specs,
        out_specs=pl.BlockSpec((tq, D), lambda b, i: (b * nq + i, 0)),
        out_shape=jax.ShapeDtypeStruct((nb * lq, D), BF16),
        compiler_params=_cparams(("arbitrary", "arbitrary")),
        name="diff_attention",
    )(*args)


def _na_ctx_kernel(q_ref, k_ref, v_ref, o_ref):
    scale = NA_DH ** -0.5
    for h in range(NA_HEADS):
        c0 = h * NA_DH
        q = q_ref[:, c0:c0 + NA_DH].astype(BF16)
        k = k_ref[:, c0:c0 + NA_DH].astype(BF16)
        v = v_ref[:, c0:c0 + NA_DH].astype(BF16)
        p = _softmax(_qkt(q, k) * scale).astype(BF16)
        o_ref[:, c0:c0 + NA_DH] = jnp.dot(p, v, preferred_element_type=F32).astype(o_ref.dtype)


def na_context_attention(q, k, v):
    spec = pl.BlockSpec((SEQ, D), lambda b: (b, 0))
    return pl.pallas_call(
        _na_ctx_kernel,
        grid=(BATCH,),
        in_specs=[spec, spec, spec],
        out_specs=spec,
        out_shape=jax.ShapeDtypeStruct((NP, D), BF16),
        compiler_params=_cparams(("arbitrary",)),
        name="na_context_attention",
    )(q, k, v)


NA_ROWS = DEC_SEQ // GRID_W
NA_NLOC = NA_KH * GRID_W


def _na_window_row(r):
    return jnp.clip(r - NA_KH // 2, 0, NA_ROWS - NA_KH)


def _na_bias_tables(rpb):
    col = jnp.arange(GRID_W)
    col_start = jnp.clip(col - NA_KW // 2, 0, GRID_W - NA_KW)
    col_ok = (col[None, :] >= col_start[:, None]) & (col[None, :] < col_start[:, None] + NA_KW)
    dc_idx = jnp.clip(col[None, :] - col[:, None] + NA_KW - 1, 0, 2 * NA_KW - 2)
    hot = (dc_idx[:, :, None] == jnp.arange(2 * NA_KW - 1)[None, None, :]).astype(F32)
    full = jnp.einsum('hrd,qkd->hqrk', rpb.astype(F32), hot, precision=lax.Precision.HIGHEST)
    full = jnp.where(col_ok[None, :, None, :], full, -jnp.inf)
    full = full.reshape(NA_HEADS, GRID_W, (2 * NA_KH_MAX - 1) * GRID_W)
    tabs = [full[:, :, (NA_KH_MAX - 1 - o) * GRID_W:(NA_KH_MAX - 1 - o) * GRID_W + NA_NLOC]
            for o in range(NA_KH)]
    return jnp.stack(tabs)


def _na_lat_kernel(q_ref, k_ref, v_ref, kc_ref, vc_ref, bias_ref, o_ref, pctx_ref, oloc_ref):
    scale = NA_DH ** -0.5
    s_ctx_all = _qkt(q_ref[...], kc_ref[...]) * scale
    for r in range(NA_ROWS):
        r0 = min(max(r - NA_KH // 2, 0), NA_ROWS - NA_KH)
        rows = slice(r * GRID_W, (r + 1) * GRID_W)
        win = slice(r0 * GRID_W, r0 * GRID_W + NA_NLOC)
        s_loc = _qkt(q_ref[rows, :], k_ref[win, :]) * scale + bias_ref[r - r0]
        s_ctx = s_ctx_all[rows, :]
        m = jnp.maximum(jnp.max(s_loc, axis=-1, keepdims=True), jnp.max(s_ctx, axis=-1, keepdims=True))
        p_loc = jnp.exp(s_loc - m)
        p_ctx = jnp.exp(s_ctx - m)
        den = jnp.sum(p_loc, axis=-1, keepdims=True) + jnp.sum(p_ctx, axis=-1, keepdims=True)
        oloc_ref[rows, :] = jnp.dot((p_loc / den).astype(BF16), v_ref[win, :], preferred_element_type=F32)
        pctx_ref[rows, :] = (p_ctx / den).astype(BF16)
    o = oloc_ref[...] + jnp.dot(pctx_ref[...], vc_ref[...], preferred_element_type=F32)
    o_ref[...] = o.astype(o_ref.dtype)


def na_latent_attention(q, k, v, k_ctx, v_ctx, bias_tabs):
    head = pl.BlockSpec((DEC_SEQ, NA_DH), lambda b, h: (b, h))
    ctx = pl.BlockSpec((None, PAST, NA_DH), lambda b, h: (b, 0, h))
    return pl.pallas_call(
        _na_lat_kernel,
        grid=(DEC_BATCH, NA_HEADS),
        in_specs=[head, head, head, ctx, ctx,
                  pl.BlockSpec((NA_KH, None, GRID_W, NA_NLOC), lambda b, h: (0, h, 0, 0))],
        out_specs=head,
        out_shape=jax.ShapeDtypeStruct((NS, D), BF16),
        scratch_shapes=[pltpu.VMEM((DEC_SEQ, PAST), BF16), pltpu.VMEM((DEC_SEQ, NA_DH), F32)],
        compiler_params=_cparams(("arbitrary", "arbitrary")),
        name="na_latent_attention",
    )(q, k, v, k_ctx, v_ctx, bias_tabs)


def _ret_kernel(nc, has_state, emit_state, logg_ref, q_ref, k_ref, v_ref, g_ref, *rest):
    rest = list(rest)
    s0_ref = rest.pop(0) if has_state else None
    o_ref = rest.pop(0)
    st_ref = rest.pop(0) if emit_state else None
    oacc, sacc = rest
    h = pl.program_id(1)
    lgf = logg_ref[0, h]
    lgb = logg_ref[1, h]
    c = RET_CHUNK
    pos = lax.broadcasted_iota(jnp.int32, (c, 1), 0).astype(F32)
    diff = (lax.broadcasted_iota(jnp.int32, (c, c), 0) - lax.broadcasted_iota(jnp.int32, (c, c), 1)).astype(F32)
    dmask_f = jnp.where(diff >= 0, jnp.exp(jnp.maximum(diff, 0.0) * lgf), 0.0)
    dmask_b = jnp.where(diff <= 0, jnp.exp(jnp.maximum(-diff, 0.0) * lgb), 0.0)
    qdec_f = jnp.exp((pos + 1.0) * lgf)
    kdec_f = jnp.exp((c - 1.0 - pos) * lgf)
    qdec_b = jnp.exp((c - pos) * lgb)
    kdec_b = jnp.exp(pos * lgb)
    cd_f = jnp.exp(jnp.full((1, 1), c, F32) * lgf)
    cd_b = jnp.exp(jnp.full((1, 1), c, F32) * lgb)

    def chunk(i):
        sl = slice(i * c, (i + 1) * c)
        return q_ref[sl, :].astype(BF16), k_ref[sl, :] * (RET_QK ** -0.5), v_ref[sl, :].astype(BF16)

    def state_update(kc, kdec, vb, cd):
        kd = (kc * kdec).T.astype(BF16)
        sacc[...] = sacc[...] * cd + jnp.dot(kd, vb, preferred_element_type=F32)

    if has_state:
        sacc[...] = s0_ref[0]
    else:
        sacc[...] = jnp.zeros_like(sacc)
    for i in range(nc):
        qb, kc, vb = chunk(i)
        qk = _qkt(qb, kc.astype(BF16))
        o = (jnp.dot((qk * dmask_f).astype(BF16), vb, preferred_element_type=F32)
             + jnp.dot((qk * dmask_b).astype(BF16), vb, preferred_element_type=F32)
             + jnp.dot(qb, sacc[...].astype(BF16), preferred_element_type=F32) * qdec_f)
        oacc[i * c:(i + 1) * c, :] = o
        state_update(kc, kdec_f, vb, cd_f)
    if emit_state:
        st_ref[0] = sacc[...]

    if has_state:
        sacc[...] = s0_ref[1]
    else:
        sacc[...] = jnp.zeros_like(sacc)
    for i in reversed(range(nc)):
        qb, kc, vb = chunk(i)
        oacc[i * c:(i + 1) * c, :] += jnp.dot(qb, sacc[...].astype(BF16), preferred_element_type=F32) * qdec_b
        state_update(kc, kdec_b, vb, cd_b)
    if emit_state:
        st_ref[1] = sacc[...]

    o = oacc[...]
    mu = jnp.mean(o, axis=-1, keepdims=True)
    var = jnp.mean(jnp.square(o - mu), axis=-1, keepdims=True)
    o = (o - mu) * lax.rsqrt(var + LN_EPS)
    o_ref[...] = (o * _silu(g_ref[...])).astype(o_ref.dtype)


def retention(q, k, v, g, log_g, nb, l, state=None, emit_state=False):
    nc = l // RET_CHUNK
    has_state = state is not None
    in_specs = [pl.BlockSpec(memory_space=pltpu.SMEM),
                pl.BlockSpec((l, RET_QK), lambda b, h: (b, h)),
                pl.BlockSpec((l, RET_QK), lambda b, h: (b, h)),
                pl.BlockSpec((l, RET_V), lambda b, h: (b, h)),
                pl.BlockSpec((l, RET_V), lambda b, h: (b, h))]
    args = [log_g, q, k, v, g]
    st_spec = pl.BlockSpec((None, 2, None, RET_QK, RET_V), lambda b, h: (b, 0, h, 0, 0))
    if has_state:
        in_specs.append(st_spec)
        args.append(state)
    out_specs = [pl.BlockSpec((l, RET_V), lambda b, h: (b, h))]
    out_shape = [jax.ShapeDtypeStruct((nb * l, RET_HEADS * RET_V), BF16)]
    if emit_state:
        out_specs.append(st_spec)
        out_shape.append(jax.ShapeDtypeStruct((nb, 2, RET_HEADS, RET_QK, RET_V), F32))
    res = pl.pallas_call(
        functools.partial(_ret_kernel, nc, has_state, emit_state),
        grid=(nb, RET_HEADS),
        in_specs=in_specs,
        out_specs=out_specs,
        out_shape=out_shape,
        scratch_shapes=[pltpu.VMEM((l, RET_V), F32), pltpu.VMEM((RET_QK, RET_V), F32)],
        compiler_params=_cparams(("arbitrary", "arbitrary")),
        name="retention",
    )(*args)
    return res if emit_state else res[0]


def _layer_norm(z, g, b):
    mu = jnp.mean(z, axis=-1, keepdims=True)
    var = jnp.mean(jnp.square(z - mu), axis=-1, keepdims=True)
    return (z - mu) * lax.rsqrt(var + LN_EPS) * g + b


def _pack_rows(hn, hp_ref):
    tm = hn.shape[0]
    half = D // 2
    lo = lax.bitcast_convert_type(hn[:, :half].astype(BF16).astype(F32), U32)
    hi = lax.bitcast_convert_type(hn[:, half:].astype(BF16).astype(F32), U32)
    w = (lo >> 16) | (hi & HI_MASK)
    for s in range(PK_S):
        hp_ref[pl.ds(s, tm, stride=PK_P), :] = w[:, s * PK_L:(s + 1) * PK_L]
    hp_ref[pl.ds(PK_S, tm, stride=PK_P), :] = jnp.zeros((tm, PK_L), U32)


def _unpack_rows(hp_ref, tm):
    lo, hi = [], []
    for s in range(PK_S):
        w = hp_ref[pl.ds(s, tm, stride=PK_P), :]
        lo.append(lax.bitcast_convert_type(w << 16, F32).astype(BF16))
        hi.append(lax.bitcast_convert_type(w & HI_MASK, F32).astype(BF16))
    return jnp.concatenate(lo + hi, axis=1)


def _route_tile(hn, wrt_ref, rb_ref, e_ref, w_ref, rank_ref, cnt_ref):
    tm = hn.shape[0]
    neg = -jnp.inf
    hb = hn.astype(BF16)
    hl = (hn - hb.astype(F32)).astype(BF16)
    logits = _qkt(wrt_ref[0], hb) + _qkt(wrt_ref[0], hl) + _qkt(wrt_ref[1], hb)
    s = _sigmoid(logits)
    biased = s + rb_ref[...]
    io8 = lax.broadcasted_iota(I32, (EXPERTS_PER_GROUP, tm), 0).astype(F32)
    slabs, gscore = [], []
    for g in range(N_GROUPS):
        slab = biased[g * EXPERTS_PER_GROUP:(g + 1) * EXPERTS_PER_GROUP, :]
        m1 = jnp.max(slab, axis=0, keepdims=True)
        i1 = jnp.min(jnp.where(slab == m1, io8, float(EXPERTS_PER_GROUP)), axis=0, keepdims=True)
        m2 = jnp.max(jnp.where(io8 == i1, neg, slab), axis=0, keepdims=True)
        slabs.append(slab)
        gscore.append(m1 + m2)
    masked = []
    for g in range(N_GROUPS):
        ahead = jnp.zeros_like(gscore[g])
        for g2 in range(N_GROUPS):
            if g2 == g:
                continue
            better = (gscore[g2] > gscore[g]) | ((gscore[g2] == gscore[g]) if g2 < g else False)
            ahead = ahead + jnp.where(better, 1.0, 0.0)
        masked.append(jnp.where(ahead < float(TOPK_GROUPS), slabs[g], neg))
    v = jnp.concatenate(masked, axis=0)
    io = lax.broadcasted_iota(I32, (N_EXPERTS, tm), 0).astype(F32)
    idxs, ws, hots = [], [], []
    for _ in range(TOP_K):
        m = jnp.max(v, axis=0, keepdims=True)
        idx = jnp.min(jnp.where(v == m, io, float(N_EXPERTS)), axis=0, keepdims=True)
        hot = io == idx
        ws.append(jnp.sum(jnp.where(hot, s, 0.0), axis=0, keepdims=True))
        v = jnp.where(hot, neg, v)
        idxs.append(idx)
        hots.append(hot)
    wsum = ws[0]
    for k in range(1, TOP_K):
        wsum = wsum + ws[k]
    chosen = jnp.where(hots[0], 1.0, 0.0)
    for k in range(1, TOP_K):
        chosen = chosen + jnp.where(hots[k], 1.0, 0.0)
    upper = jnp.where(lax.broadcasted_iota(I32, (tm, tm), 0) < lax.broadcasted_iota(I32, (tm, tm), 1), 1.0, 0.0)
    before = jnp.dot(chosen.astype(BF16), upper.astype(BF16), preferred_element_type=F32)
    zero_row = jnp.zeros((1, tm), F32)
    for k in range(8):
        if k < TOP_K:
            e_ref[k:k + 1, :] = idxs[k].astype(I32)
            w_ref[k:k + 1, :] = ws[k] / wsum * ROUTED_SCALE
            rank_ref[k:k + 1, :] = jnp.sum(jnp.where(hots[k], before, 0.0), axis=0, keepdims=True).astype(I32)
        else:
            e_ref[k:k + 1, :] = zero_row.astype(I32)
            w_ref[k:k + 1, :] = zero_row
            rank_ref[k:k + 1, :] = zero_row.astype(I32)
    cnt = jnp.sum(chosen, axis=1, keepdims=True)
    cnt_ref[...] = jnp.broadcast_to(cnt, (N_EXPERTS, 128)).astype(I32)


def _outproj_kernel(nk, np_tiles, op_ref, os_ref, w_ref, x_ref, gate_ref, lng_ref, lnb_ref, sh_ref, sc_ref,
                    wrt_ref, rb_ref, xo_ref, hp_ref, e_ref, wt_ref, rank_ref, cnt_ref, acc_ref):
    kk = pl.program_id(1)
    o_blk = jnp.where(pl.program_id(0) < np_tiles, op_ref[...], os_ref[...])
    part = jnp.dot(o_blk, w_ref[...], preferred_element_type=F32)

    @pl.when(kk == 0)
    def _():
        acc_ref[...] = part

    @pl.when(kk > 0)
    def _():
        acc_ref[...] += part

    @pl.when(kk == nk - 1)
    def _():
        xn = _layer_norm(ALPHA * x_ref[...] + gate_ref[...] * acc_ref[...], lng_ref[...], lnb_ref[...])
        xo_ref[...] = xn
        hn = xn * (1.0 + sc_ref[...]) + sh_ref[...]
        _pack_rows(hn, hp_ref)
        _route_tile(hn, wrt_ref, rb_ref, e_ref, wt_ref, rank_ref, cnt_ref)


def outproj_ln(op, os_, w_out, x, mods, ln_g, ln_b, w_router_t, b_router, layer):
    k = op.shape[1]
    tk = 1024
    nk = k // tk
    tm = RT_TM
    ntile = NT // tm
    np_tiles, ns_tiles = NP // tm, NS // tm
    row = lambda m, kk: (m, 0)
    col = lambda m, kk: (0, m)
    return pl.pallas_call(
        functools.partial(_outproj_kernel, nk, np_tiles),
        grid=(ntile, nk),
        in_specs=[pl.BlockSpec((tm, tk), lambda m, kk: (jnp.minimum(m, np_tiles - 1), kk)),
                  pl.BlockSpec((tm, tk), lambda m, kk: (jnp.clip(m - np_tiles, 0, ns_tiles - 1), kk)),
                  pl.BlockSpec((tk, D), lambda m, kk: (kk, 0)),
                  pl.BlockSpec((tm, D), row),
                  _mod_spec(layer, 2, tm),
                  pl.BlockSpec((None, None, 1, D), lambda m, kk: (layer, 0, 0, 0)),
                  pl.BlockSpec((None, None, 1, D), lambda m, kk: (layer, 0, 0, 0)),
                  _mod_spec(layer, 3, tm), _mod_spec(layer, 4, tm),
                  pl.BlockSpec((2, N_EXPERTS, D), lambda m, kk: (0, 0, 0)),
                  pl.BlockSpec((N_EXPERTS, 1), lambda m, kk: (0, 0))],
        out_specs=[pl.BlockSpec((tm, D), row),
                   pl.BlockSpec((tm * PK_P, PK_L), row),
                   pl.BlockSpec((8, tm), col), pl.BlockSpec((8, tm), col), pl.BlockSpec((8, tm), col),
                   pl.BlockSpec((None, N_EXPERTS, 128), lambda m, kk: (m, 0, 0))],
        out_shape=[jax.ShapeDtypeStruct((NT, D), F32),
                   jax.ShapeDtypeStruct((NT * PK_P, PK_L), U32),
                   jax.ShapeDtypeStruct((8, NT), I32), jax.ShapeDtypeStruct((8, NT), F32),
                   jax.ShapeDtypeStruct((8, NT), I32),
                   jax.ShapeDtypeStruct((ntile, N_EXPERTS, 128), I32)],
        scratch_shapes=[pltpu.VMEM((tm, D), F32)],
        compiler_params=_cparams(("arbitrary", "arbitrary")),
        name="outproj_ln_route",
    )(op, os_, w_out, x, mods, ln_g, ln_b, mods, mods, w_router_t, b_router.reshape(N_EXPERTS, 1))


def _outproj_split_kernel(np_tiles, x_split, op_ref, os_ref, w_ref, *refs):
    if x_split:
        xp_ref, xs_ref, *refs = refs
    else:
        x_ref, *refs = refs
    (gate_ref, lng_ref, lnb_ref, sh_ref, sc_ref, wrt_ref, rb_ref,
     xo_ref, hp_ref, e_ref, wt_ref, rank_ref, cnt_ref) = refs
    m = pl.program_id(0)
    o_blk = jnp.where(m < np_tiles, op_ref[...], os_ref[...])
    y = jnp.dot(o_blk, w_ref[...], preferred_element_type=F32)
    x = jnp.where(m < np_tiles, xp_ref[...], xs_ref[...]) if x_split else x_ref[...]
    xn = _layer_norm(ALPHA * x + gate_ref[...] * y, lng_ref[...], lnb_ref[...])
    xo_ref[...] = xn
    hn = xn * (1.0 + sc_ref[...]) + sh_ref[...]
    _pack_rows(hn, hp_ref)
    _route_tile(hn, wrt_ref, rb_ref, e_ref, wt_ref, rank_ref, cnt_ref)


def outproj_ln_split(op, os_, w_out, x, mods, ln_g, ln_b, w_router_t, b_router, layer):
    tm = RT_TM
    ntile = NT // tm
    np_tiles = NP // tm
    row = lambda m: (m, 0)
    col = lambda m: (0, m)
    mod = lambda chunk: _mod_spec(layer, chunk, tm)
    const2 = lambda m: (0, 0)
    x_split = isinstance(x, tuple)
    x_args = list(x) if x_split else [x]
    x_specs = list(_split_rows_specs(tm)) if x_split else [pl.BlockSpec((tm, D), row)]
    return pl.pallas_call(
        functools.partial(_outproj_split_kernel, np_tiles, x_split),
        grid=(ntile,),
        in_specs=[*_split_rows_specs(tm),
                  pl.BlockSpec((D, D), const2, pipeline_mode=pl.Buffered(1)),
                  *x_specs,
                  mod(2),
                  pl.BlockSpec((None, None, 1, D), lambda m: (layer, 0, 0, 0)),
                  pl.BlockSpec((None, None, 1, D), lambda m: (layer, 0, 0, 0)),
                  mod(3), mod(4),
                  pl.BlockSpec((2, N_EXPERTS, D), lambda m: (0, 0, 0)),
                  pl.BlockSpec((N_EXPERTS, 1), const2)],
        out_specs=[pl.BlockSpec((tm, D), row),
                   pl.BlockSpec((tm * PK_P, PK_L), row),
                   pl.BlockSpec((8, tm), col), pl.BlockSpec((8, tm), col), pl.BlockSpec((8, tm), col),
                   pl.BlockSpec((None, N_EXPERTS, 128), lambda m: (m, 0, 0))],
        out_shape=[jax.ShapeDtypeStruct((NT, D), F32),
                   jax.ShapeDtypeStruct((NT * PK_P, PK_L), U32),
                   jax.ShapeDtypeStruct((8, NT), I32), jax.ShapeDtypeStruct((8, NT), F32),
                   jax.ShapeDtypeStruct((8, NT), I32),
                   jax.ShapeDtypeStruct((ntile, N_EXPERTS, 128), I32)],
        compiler_params=_cparams(("arbitrary",)),
        name="outproj_ln_route_split",
    )(op, os_, w_out, *x_args, mods, ln_g, ln_b, mods, mods, w_router_t, b_router.reshape(N_EXPERTS, 1))


def _slot_plan(e_t, rank_t, cnt):
    cnt = cnt[:, :, 0]
    tile_base = jnp.cumsum(cnt, axis=0) - cnt
    counts = jnp.sum(cnt, axis=0)
    padded = (counts + MOE_TB - 1) // MOE_TB * MOE_TB
    pad_end = jnp.cumsum(padded)
    pad_start = pad_end - padded
    offs = jnp.repeat(pad_start[None, :] + tile_base, NT // cnt.shape[0], axis=0)
    hot = e_t[:TOP_K, :, None] == jnp.arange(N_EXPERTS, dtype=I32)[None, None, :]
    dest = jnp.sum(jnp.where(hot, offs[None], 0), axis=-1) + rank_t[:TOP_K]
    ntile = NT // CMB_TM
    dest_tiles = dest.reshape(TOP_K, ntile, CMB_TM).transpose(1, 0, 2).reshape(ntile, 1, TOP_K * CMB_TM)
    nused = pad_end[-1] // MOE_TB
    blk = jnp.minimum(jnp.arange(MOE_NBLK, dtype=I32), nused - 1) * MOE_TB
    block_e = jnp.minimum(jnp.sum((pad_end[None, :] <= blk[:, None]).astype(I32), axis=1), N_EXPERTS - 1)
    first = jnp.concatenate([jnp.ones((1,), I32), (block_e[1:] != block_e[:-1]).astype(I32)])
    zero_start = pad_start + counts
    ids = jnp.arange(N_EXPERTS, dtype=I32)
    used = counts > 0
    later = jnp.where(used[None, :] & (ids[None, :] > ids[:, None]), ids[None, :], N_EXPERTS)
    nxt = jnp.min(later, axis=1)
    nxt = jnp.where(nxt == N_EXPERTS, ids, nxt)
    par = (jnp.cumsum(used.astype(I32)) - 1) % 2
    owner = block_e[:, None] == ids[None, :]
    nxt_b = jnp.sum(jnp.where(owner, nxt[None, :], 0), axis=1)
    par_b = jnp.sum(jnp.where(owner, par[None, :], 0), axis=1)
    return (dest_tiles.astype(I32), block_e.astype(I32), first, nused.reshape(1).astype(I32),
            zero_start.astype(I32), nxt_b.astype(I32), par_b.astype(I32))


_DISPATCH_ROWS = TOP_K * CMB_TM
_CMB_ROWS = 32


def _dispatch_kernel(zs_ref, nused_ref, dest_ref, hp_ref, wsg_ref, wsu_ref, wsd_ref, xs_hbm, shared_ref,
                     stage, zbuf, zsem, sem):
    i = pl.program_id(0)
    n = pl.num_programs(0)
    slot = i % 2

    @pl.when(i == 0)
    def _():
        zbuf[...] = jnp.zeros_like(zbuf)

        def zcopy(slot0, nslot):
            return pltpu.make_async_copy(zbuf.at[pl.ds(0, nslot * PK_P)],
                                         xs_hbm.at[pl.ds(slot0 * PK_P, nslot * PK_P)], zsem)

        def pad_pieces(e, wait):
            slot0 = zs_ref[e]
            npad = (MOE_TB - slot0 % MOE_TB) % MOE_TB
            piece = MOE_TB // 2
            while piece >= 1:
                @pl.when((npad & piece) != 0)
                def _(slot0=slot0, piece=piece):
                    cp = zcopy(slot0, piece)
                    cp.wait() if wait else cp.start()
                slot0 = slot0 + (npad & piece)
                piece //= 2

        def zstart(e, carry):
            pad_pieces(e, False)
            return carry

        def zwait(e, carry):
            pad_pieces(e, True)
            return carry

        def tstart(b, carry):
            zcopy(b * MOE_TB, MOE_TB).start()
            return carry

        def twait(b, carry):
            zcopy(b * MOE_TB, MOE_TB).wait()
            return carry

        lax.fori_loop(0, N_EXPERTS, zstart, 0)
        lax.fori_loop(nused_ref[0], MOE_NBLK + 1, tstart, 0)
        lax.fori_loop(0, N_EXPERTS, zwait, 0)
        lax.fori_loop(nused_ref[0], MOE_NBLK + 1, twait, 0)

    stage[slot] = hp_ref[...]

    def issue(r, carry):
        for k in range(TOP_K):
            dst = dest_ref[0, 0, k * CMB_TM + r] * PK_P
            pltpu.make_async_copy(stage.at[slot, pl.ds(r * PK_P, PK_P)], xs_hbm.at[pl.ds(dst, PK_P)],
                                  sem.at[slot]).start(priority=k % 2)
        return carry

    lax.fori_loop(0, CMB_TM, issue, 0, unroll=2)

    hb = _unpack_rows(hp_ref, CMB_TM)
    a = (_silu(jnp.dot(hb, wsg_ref[...], preferred_element_type=F32))
         * jnp.dot(hb, wsu_ref[...], preferred_element_type=F32))
    shared_ref[...] = jnp.dot(a.astype(BF16), wsd_ref[...], preferred_element_type=F32)

    def drain(s):
        for k in range(TOP_K):
            pltpu.make_async_copy(stage.at[s], xs_hbm.at[pl.ds(0, CMB_TM * PK_P)], sem.at[s]).wait()

    @pl.when(i > 0)
    def _():
        drain(1 - slot)

    @pl.when(i == n - 1)
    def _():
        drain(slot)


def moe_dispatch(hp, dest_tiles, zero_start, nused, ws_gate, ws_up, ws_down):
    ntile = NT // CMB_TM
    const = lambda i, zs, nu: (0, 0)
    grid_spec = pltpu.PrefetchScalarGridSpec(
        num_scalar_prefetch=2,
        grid=(ntile,),
        in_specs=[pl.BlockSpec((1, 1, _DISPATCH_ROWS), lambda i, zs, nu: (i, 0, 0), memory_space=pltpu.SMEM),
                  pl.BlockSpec((CMB_TM * PK_P, PK_L), lambda i, zs, nu: (i, 0)),
                  pl.BlockSpec((D, D_SHARED), const, pipeline_mode=pl.Buffered(1)),
                  pl.BlockSpec((D, D_SHARED), const, pipeline_mode=pl.Buffered(1)),
                  pl.BlockSpec((D_SHARED, D), const, pipeline_mode=pl.Buffered(1))],
        out_specs=[pl.BlockSpec(memory_space=pl.ANY),
                   pl.BlockSpec((CMB_TM, D), lambda i, zs, nu: (i, 0))],
        scratch_shapes=[pltpu.VMEM((2, CMB_TM * PK_P, PK_L), U32),
                        pltpu.VMEM((MOE_TB * PK_P, PK_L), U32),
                        pltpu.SemaphoreType.DMA(()), pltpu.SemaphoreType.DMA((2,))],
    )
    return pl.pallas_call(
        _dispatch_kernel,
        grid_spec=grid_spec,
        out_shape=[jax.ShapeDtypeStruct(((MOE_CAP + MOE_TB) * PK_P, PK_L), U32),
                   jax.ShapeDtypeStruct((NT, D), F32)],
        compiler_params=_cparams(("arbitrary",)),
        name="moe_dispatch",
    )(zero_start, nused, dest_tiles, hp, ws_gate, ws_up, ws_down)


def _moe_kernel(layer, be_ref, first_ref, nused_ref, nxt_ref, par_ref, xs_ref, wg_hbm, wu_hbm, wd_hbm, y_ref,
                wgs, wus, wds, wgb, wub, wdb, sem):
    i = pl.program_id(0)

    def weight_copies(e, s):
        return (pltpu.make_async_copy(wg_hbm.at[layer, e], wgs.at[s], sem.at[s]),
                pltpu.make_async_copy(wu_hbm.at[layer, e], wus.at[s], sem.at[s]),
                pltpu.make_async_copy(wd_hbm.at[layer, e], wds.at[s], sem.at[s]))

    @pl.when(i == 0)
    def _():
        for cp in weight_copies(be_ref[0], par_ref[0]):
            cp.start()

    @pl.when(first_ref[i] == 1)
    def _():
        e, s = be_ref[i], par_ref[i]
        for cp in weight_copies(e, s):
            cp.wait()

        @pl.when(nxt_ref[i] != e)
        def _():
            for cp in weight_copies(nxt_ref[i], 1 - s):
                cp.start()

        wgb[...] = wgs[s].astype(BF16)
        wub[...] = wus[s].astype(BF16)
        wdb[...] = wds[s].astype(BF16)

    @pl.when(i < nused_ref[0])
    def _():
        x = _unpack_rows(xs_ref, MOE_TB)
        a = (_silu(jnp.dot(x, wgb[...], preferred_element_type=F32))
             * jnp.dot(x, wub[...], preferred_element_type=F32))
        y = jnp.dot(a.astype(BF16), wdb[...], preferred_element_type=F32)
        for j in range(Y_S):
            y_ref[pl.ds(j, MOE_TB, stride=Y_P), :] = y[:, j * 128:(j + 1) * 128]
        y_ref[pl.ds(Y_S, MOE_TB, stride=Y_P), :] = jnp.zeros((MOE_TB, 128), F32)

    @pl.when(i >= nused_ref[0])
    def _():
        y_ref[...] = jnp.zeros_like(y_ref)


def moe_experts(xs, block_e, first, nused, nxt_e, par, w_gate, w_up, w_down, layer):
    blk = lambda i, be, fi, nu, nx, pa: (jnp.minimum(i, nu[0] - 1), 0)
    grid_spec = pltpu.PrefetchScalarGridSpec(
        num_scalar_prefetch=5,
        grid=(MOE_NBLK,),
        in_specs=[pl.BlockSpec((MOE_TB * PK_P, PK_L), blk),
                  pl.BlockSpec(memory_space=pl.ANY), pl.BlockSpec(memory_space=pl.ANY),
                  pl.BlockSpec(memory_space=pl.ANY)],
        out_specs=pl.BlockSpec((MOE_TB * Y_P, 128), lambda i, be, fi, nu, nx, pa: (i, 0)),
        scratch_shapes=[pltpu.VMEM((2, D, D_EXPERT), F32), pltpu.VMEM((2, D, D_EXPERT), F32),
                        pltpu.VMEM((2, D_EXPERT, D), F32),
                        pltpu.VMEM((D, D_EXPERT), BF16), pltpu.VMEM((D, D_EXPERT), BF16),
                        pltpu.VMEM((D_EXPERT, D), BF16),
                        pltpu.SemaphoreType.DMA((2,))],
    )
    return pl.pallas_call(
        functools.partial(_moe_kernel, layer),
        grid_spec=grid_spec,
        out_shape=jax.ShapeDtypeStruct((MOE_CAP * Y_P, 128), F32),
        compiler_params=_cparams(("arbitrary",)),
        name="moe_experts",
    )(block_e, first, nused, nxt_e, par, xs, w_gate, w_up, w_down)


def _combine_kernel(has_next, dest_ref, destn_ref, y_hbm, w_ref, shared_ref, x_ref,
                    gate_ref, lng_ref, lnb_ref, *rest):
    if has_next:
        sh_ref, sc_ref, xo_ref, ho_ref, buf, routed_ref, sem = rest
    else:
        xo_ref, buf, routed_ref, sem = rest
    i = pl.program_id(0)
    n = pl.num_programs(0)
    slot = i % 2
    nrow = TOP_K * CMB_TM

    def gather(dref, s):
        def body(r, carry):
            for k in range(TOP_K):
                src = dref[0, 0, k * CMB_TM + r] * Y_P
                dst = (k * CMB_TM + r) * Y_P
                pltpu.make_async_copy(y_hbm.at[pl.ds(src, Y_S)], buf.at[s, pl.ds(dst, Y_S)],
                                      sem.at[s]).start(priority=k % 2)
            return carry
        lax.fori_loop(0, CMB_TM, body, 0, unroll=2)

    @pl.when(i == 0)
    def _():
        gather(dest_ref, 0)

    @pl.when(i + 1 < n)
    def _():
        gather(destn_ref, 1 - slot)

    pltpu.make_async_copy(y_hbm.at[pl.ds(0, nrow * Y_S)], buf.at[slot, pl.ds(0, nrow * Y_S)],
                          sem.at[slot]).wait()
    for r0 in range(0, CMB_TM, _CMB_ROWS):
        wsub = w_ref[r0:r0 + _CMB_ROWS, :]
        wb = [jnp.broadcast_to(wsub[:, k:k + 1], (_CMB_ROWS, 128)) for k in range(TOP_K)]
        for j in range(Y_S):
            acc = buf[slot, pl.ds(r0 * Y_P + j, _CMB_ROWS, stride=Y_P), :] * wb[0]
            for k in range(1, TOP_K):
                acc = acc + buf[slot, pl.ds((k * CMB_TM + r0) * Y_P + j, _CMB_ROWS, stride=Y_P), :] * wb[k]
            routed_ref[r0:r0 + _CMB_ROWS, j * 128:(j + 1) * 128] = acc
    routed = routed_ref[...]
    xn = _layer_norm(ALPHA * x_ref[...] + gate_ref[...] * (routed + shared_ref[...]), lng_ref[...], lnb_ref[...])
    xo_ref[...] = xn
    if has_next:
        ho_ref[...] = (xn * (1.0 + sc_ref[...]) + sh_ref[...]).astype(ho_ref.dtype)


def moe_combine(y_slots, dest_tiles, w_tok, shared, x1, mods, ln_g, ln_b, layer):
    has_next = layer + 1 < DEPTH
    tm = CMB_TM
    ntile = NT // tm
    row = lambda m: (m, 0)
    in_specs = [pl.BlockSpec((1, 1, TOP_K * tm), lambda m: (m, 0, 0), memory_space=pltpu.SMEM),
                pl.BlockSpec((1, 1, TOP_K * tm), lambda m: (jnp.minimum(m + 1, ntile - 1), 0, 0),
                             memory_space=pltpu.SMEM),
                pl.BlockSpec(memory_space=pl.ANY),
                pl.BlockSpec((tm, 8), row),
                pl.BlockSpec((tm, D), row), pl.BlockSpec((tm, D), row),
                _mod_spec(layer, 5, tm),
                pl.BlockSpec((None, None, 1, D), lambda m: (layer, 1, 0, 0)),
                pl.BlockSpec((None, None, 1, D), lambda m: (layer, 1, 0, 0))]
    args = [dest_tiles, dest_tiles, y_slots, w_tok, shared, x1, mods, ln_g, ln_b]
    out_specs = [pl.BlockSpec((tm, D), row)]
    out_shape = [jax.ShapeDtypeStruct((NT, D), F32)]
    if has_next:
        in_specs += [_mod_spec(layer + 1, 0, tm), _mod_spec(layer + 1, 1, tm)]
        args += [mods, mods]
        out_specs.append(pl.BlockSpec((tm, D), row))
        out_shape.append(jax.ShapeDtypeStruct((NT, D), BF16))
    res = pl.pallas_call(
        functools.partial(_combine_kernel, has_next),
        grid=(ntile,),
        in_specs=in_specs,
        out_specs=out_specs,
        out_shape=out_shape,
        scratch_shapes=[pltpu.VMEM((2, TOP_K * tm * Y_P, 128), F32), pltpu.VMEM((tm, D), F32),
                        pltpu.SemaphoreType.DMA((2,))],
        compiler_params=_cparams(("arbitrary",)),
        name="moe_combine",
    )(*args)
    return (res[0], res[1]) if has_next else (res[0], None)


def _split_bf16(w):
    hi = w.astype(BF16)
    lo = (w - hi.astype(F32)).astype(BF16)
    return jnp.stack([hi, lo])


def kernel(x_prompt, x_sample, cache_diff_k, cache_diff_v, cache_na_k, cache_na_v, state_ret, c, c_ctx, ada_w, ada_b, ln_g, ln_b, diff_w_in, diff_w_out, diff_lambda, diff_subln_g, na_w_in, na_w_out, na_rpb, ret_w_in, ret_w_out, ret_decay, moe_router, moe_router_bias, moe_w_gate, moe_w_up, moe_w_down, shared_w_gate, shared_w_up, shared_w_down):
    x = (x_prompt.reshape(NP, D), x_sample.reshape(NS, D))
    cond = jnp.concatenate([c_ctx[None], c, jnp.zeros((N_COND - 1 - DEC_BATCH, D), F32)], axis=0)
    mods = modulation_table(cond, ada_w, ada_b)
    ln_g4 = ln_g.reshape(DEPTH, 2, 1, D)
    ln_b4 = ln_b.reshape(DEPTH, 2, 1, D)
    cos, sin = _rope_tables()
    h = modulate(*x, mods, 0)
    new_dk, new_dv, new_nk, new_nv, new_rs = [], [], [], [], []
    for i in range(DEPTH):
        j = i // N_MIXERS
        kind = i % N_MIXERS
        if kind == 0:
            qp = matmul(h, diff_w_in, j, 0, D, 0, NP, BF16)
            kp = matmul(h, diff_w_in, j, D, D, 0, NP, F32)
            vp = matmul(h, diff_w_in, j, 2 * D, D, 0, NP, F32)
            qs = matmul_rope(h, diff_w_in, j, 0, D, cos, sin)
            ks = matmul_rope(h, diff_w_in, j, D, D, cos, sin)
            vs = matmul(h, diff_w_in, j, 2 * D, D, NP, NS, BF16)
            lam_init = 0.8 - 0.6 * math.exp(-0.3 * i)
            rows = lambda n: pl.BlockSpec((n, D), lambda b, t: (b, 0))
            op = diff_attention(qp, [(kp, vp, rows(SEQ))], diff_lambda[j], diff_subln_g[j], lam_init,
                                BATCH, SEQ, SEQ)
            ctx = pl.BlockSpec((None, None, PAST, D), lambda b, t, j=j: (b, j, 0, 0))
            os_ = diff_attention(qs, [(cache_diff_k.reshape(DEC_BATCH, -1, PAST, D),
                                       cache_diff_v.reshape(DEC_BATCH, -1, PAST, D), ctx),
                                      (ks, vs, rows(DEC_SEQ))],
                                 diff_lambda[j], diff_subln_g[j], lam_init, DEC_BATCH, DEC_SEQ, 256)
            new_dk.append(kp.reshape(BATCH, SEQ, DIFF_HEADS, 2 * DIFF_QK))
            new_dv.append(vp.reshape(BATCH, SEQ, DIFF_HEADS, DIFF_V))
            w_out = diff_w_out[j]
        elif kind == 1:
            qp = matmul(h, na_w_in, j, 0, D, 0, NP, BF16)
            kp = matmul(h, na_w_in, j, D, D, 0, NP, F32)
            vp = matmul(h, na_w_in, j, 2 * D, D, 0, NP, F32)
            qs = matmul(h, na_w_in, j, 0, D, NP, NS, BF16)
            ks = matmul(h, na_w_in, j, D, D, NP, NS, BF16)
            vs = matmul(h, na_w_in, j, 2 * D, D, NP, NS, BF16)
            op = na_context_attention(qp, kp, vp)
            os_ = na_latent_attention(qs, ks, vs,
                                      cache_na_k[:, j].reshape(DEC_BATCH, PAST, D).astype(BF16),
                                      cache_na_v[:, j].reshape(DEC_BATCH, PAST, D).astype(BF16),
                                      _na_bias_tables(na_rpb[j]))
            new_nk.append(kp.reshape(BATCH, SEQ, NA_HEADS, NA_DH))
            new_nv.append(vp.reshape(BATCH, SEQ, NA_HEADS, NA_DH))
            w_out = na_w_out[j]
        else:
            qd = RET_HEADS * RET_QK
            vd = RET_HEADS * RET_V
            log_g = -jax.nn.softplus(-ret_decay[j].astype(F32))
            outs = []
            for row0, nrows, nb, l in ((0, NP, BATCH, SEQ), (NP, NS, DEC_BATCH, DEC_SEQ)):
                q_ = matmul(h, ret_w_in, j, 0, qd, row0, nrows, BF16)
                k_ = matmul(h, ret_w_in, j, qd, qd, row0, nrows, F32)
                v_ = matmul(h, ret_w_in, j, 2 * qd, vd, row0, nrows, BF16)
                g_ = matmul(h, ret_w_in, j, 2 * qd + vd, vd, row0, nrows, F32)
                if row0 == 0:
                    o_, st = retention(q_, k_, v_, g_, log_g, nb, l, None, True)
                    new_rs.append(st)
                else:
                    o_ = retention(q_, k_, v_, g_, log_g, nb, l, state_ret[:, j], False)
                outs.append(o_)
            op, os_ = outs
            w_out = ret_w_out[j]
        if w_out.shape[0] == D:
            x1, hp, e_t, w_t, rank_t, cnt = outproj_ln_split(
                op, os_, w_out.astype(BF16), x, mods, ln_g4, ln_b4,
                _split_bf16(moe_router[i].T), moe_router_bias[i].astype(F32), i)
        else:
            x1, hp, e_t, w_t, rank_t, cnt = outproj_ln(
                op, os_, w_out.astype(BF16), x, mods, ln_g4, ln_b4,
                _split_bf16(moe_router[i].T), moe_router_bias[i].astype(F32), i)
        dest_tiles, block_e, first, nused, zero_start, nxt_e, par = _slot_plan(e_t, rank_t, cnt)
        xs, shared = moe_dispatch(hp, dest_tiles, zero_start, nused, shared_w_gate[i].astype(BF16),
                                  shared_w_up[i].astype(BF16), shared_w_down[i].astype(BF16))
        y_slots = moe_experts(xs, block_e, first, nused, nxt_e, par, moe_w_gate, moe_w_up, moe_w_down, i)
        x, h = moe_combine(y_slots, dest_tiles, w_t.T, shared, x1, mods, ln_g4, ln_b4, i)
    return (x[:NP].reshape(BATCH, SEQ, D), x[NP:].reshape(DEC_BATCH, DEC_SEQ, D),
            jnp.stack(new_dk, axis=1), jnp.stack(new_dv, axis=1),
            jnp.stack(new_nk, axis=1), jnp.stack(new_nv, axis=1), jnp.stack(new_rs, axis=1))
```
